```python
import numpy as np
import jax, jax.numpy as jnp
from jax import lax

D_MODEL = 4096
BATCH = 2
SEQ = 4096
DEPTH = 4

GRID_W = 64
CTX_LEN = 256
H_A = D_MODEL // 256
Q_LORA = D_MODEL // 4
KV_LORA = D_MODEL // 8
D_NOPE = 128
D_ROPE = 64
D_V_A = 128
H_B = D_MODEL // 1024
DQK_B = 256
DV_B = 512
CONV_K = 3
MLSTM_CHUNK = 128
M_INIT = -1e30
H_C = D_MODEL // 128
DH_C = 128
NA_KH = 8
NA_KW = 16
N_GROUPS = 4
EXPERTS_PER_GROUP = 8
N_EXPERTS = N_GROUPS * EXPERTS_PER_GROUP
TOP_K_IN_GROUP = 2
D_FF_EXPERT = 192
ROPE_THETA = 10000.0
NORM_EPS = 1e-6
Q_BLOCK = 128
W_A = H_A * D_V_A
W_B = H_B * DV_B
IN_AB = Q_LORA + KV_LORA + D_ROPE + 2 * H_B * DQK_B + 2 * W_B + 4 * H_B

kernel_name = 'hybrid_mla_mlstm_natten_hmoe_dit'


def rmsnorm(x, g):
    xf = x.astype(jnp.float32)
    y = xf * lax.rsqrt(jnp.mean(xf * xf, axis=-1, keepdims=True) + NORM_EPS)
    return (y * g.astype(jnp.float32)).astype(x.dtype)


def modulate(x, g, shift, scale):
    return rmsnorm(x, g) * (1 + scale) + shift


def rope_1d(x, pos):
    half = x.shape[-1] // 2
    inv = ROPE_THETA ** (-jnp.arange(half, dtype=jnp.float32) / half)
    ang = pos.astype(jnp.float32)[:, None] * inv[None, :]
    cos = jnp.cos(ang)[None, :, None, :]
    sin = jnp.sin(ang)[None, :, None, :]
    xf = x.astype(jnp.float32)
    x1, x2 = xf[..., :half], xf[..., half:]
    return jnp.concatenate([x1 * cos - x2 * sin, x2 * cos + x1 * sin], axis=-1).astype(x.dtype)


def rope_2d(x, rows, cols):
    d = x.shape[-1] // 2
    return jnp.concatenate([rope_1d(x[..., :d], rows), rope_1d(x[..., d:], cols)], axis=-1)


def softmax_attention(q, k, v, scale):
    s = jnp.einsum('bqhd,bkhd->bhqk', q, k, preferred_element_type=jnp.float32) * scale
    p = jax.nn.softmax(s, axis=-1).astype(v.dtype)
    return jnp.einsum('bhqk,bkhd->bqhd', p, v)


def blocked_softmax_attention(q, k, v, scale):
    b, s, h, dq = q.shape
    nb = s // Q_BLOCK
    qb = jnp.moveaxis(q.reshape(b, nb, Q_BLOCK, h, dq), 1, 0)
    ob = lax.map(lambda qi: softmax_attention(qi, k, v, scale), qb)
    return jnp.moveaxis(ob, 0, 1).reshape(b, s, h, v.shape[-1])


def dwconv_centred(x, w):
    return lax.conv_general_dilated(x, w[:, None, :].astype(x.dtype), window_strides=(1,), padding='SAME',
                                    dimension_numbers=('NWC', 'WIO', 'NWC'), feature_group_count=x.shape[-1])


def mla_heads(cq_raw, ckv_raw, kr_raw, q_norm_g, kv_norm_g, w_q_up, w_kv_up, qn_g, kn_g, rows, cols):
    b, t, _ = cq_raw.shape
    q = (rmsnorm(cq_raw, q_norm_g) @ w_q_up).reshape(b, t, H_A, D_NOPE + D_ROPE)
    kv = (rmsnorm(ckv_raw, kv_norm_g) @ w_kv_up).reshape(b, t, H_A, D_NOPE + D_V_A)
    k_nope, v = kv[..., :D_NOPE], kv[..., D_NOPE:]
    k_rope = jnp.broadcast_to(kr_raw[:, :, None, :], (b, t, H_A, D_ROPE))
    k = jnp.concatenate([k_nope, k_rope], axis=-1)
    q = rmsnorm(q, qn_g)
    k = rmsnorm(k, kn_g)
    if rows is not None:
        q = jnp.concatenate([q[..., :D_NOPE], rope_2d(q[..., D_NOPE:], rows, cols)], axis=-1)
        k = jnp.concatenate([k[..., :D_NOPE], rope_2d(k[..., D_NOPE:], rows, cols)], axis=-1)
    return q, k, v


def mlstm_scan(q, k, v, li, lf, state):
    b, h, t, dk = q.shape
    dv = v.shape[-1]
    L = min(MLSTM_CHUNK, t)
    nc = t // L
    lower = jnp.tril(jnp.ones((L, L), dtype=bool))

    def chunks(a):
        return jnp.moveaxis(a.reshape(b, h, nc, L, *a.shape[3:]), 2, 0)

    def step(carry, inp):
        C, n, m = carry
        qc, kc, vc, lic, lfc = inp
        bcum = jnp.cumsum(lfc, axis=-1)
        a = bcum + m[..., None]
        dlog = bcum[..., :, None] - bcum[..., None, :] + lic[..., None, :]
        dlog = jnp.where(lower, dlog, -jnp.inf)
        m_t = jnp.maximum(a, jnp.max(dlog, axis=-1))
        w_intra = jnp.exp(dlog - m_t[..., None])
        w_inter = jnp.exp(a - m_t)
        s = jnp.einsum('bhtd,bhsd->bhts', qc, kc) * w_intra
        num = w_inter[..., None] * jnp.einsum('bhtd,bhde->bhte', qc, C) + jnp.einsum('bhts,bhse->bhte', s, vc)
        den = w_inter * jnp.einsum('bhtd,bhd->bht', qc, n) + jnp.sum(s, axis=-1)
        hout = num / jnp.maximum(jnp.abs(den), jnp.exp(-m_t))[..., None]
        b_last = bcum[..., -1]
        g = b_last[..., None] - bcum + lic
        m_new = jnp.maximum(b_last + m, jnp.max(g, axis=-1))
        decay = jnp.exp(b_last + m - m_new)
        wg = jnp.exp(g - m_new[..., None])
        C_new = decay[..., None, None] * C + jnp.einsum('bhs,bhsd,bhse->bhde', wg, kc, vc)
        n_new = decay[..., None] * n + jnp.einsum('bhs,bhsd->bhd', wg, kc)
        return (C_new, n_new, m_new), hout

    final, hs = lax.scan(step, state, (chunks(q), chunks(k), chunks(v), chunks(li), chunks(lf)))
    return final, jnp.moveaxis(hs, 0, 2).reshape(b, h, t, dv)


def mlstm_mixer(lat, cx, out_norm_g):
    def prep(q, k, v, gates):
        b, t, _ = q.shape
        qh = q.reshape(b, t, H_B, DQK_B).transpose(0, 2, 1, 3).astype(jnp.float32)
        kh = k.reshape(b, t, H_B, DQK_B).transpose(0, 2, 1, 3).astype(jnp.float32) * (DQK_B ** -0.5)
        vh = v.reshape(b, t, H_B, DV_B).transpose(0, 2, 1, 3).astype(jnp.float32)
        g = gates.astype(jnp.float32).reshape(b, t, 2, 2, H_B).transpose(2, 3, 0, 4, 1)
        return qh, kh, vh, g[:, 0], jax.nn.log_sigmoid(g[:, 1])

    ql, kl, vl, il, fl = prep(lat[0], lat[1], lat[2], lat[4])
    qc, kc, vc, ic, fc = prep(cx[0], cx[1], cx[2], cx[4])
    b = ql.shape[0]
    init = (jnp.zeros((b, H_B, DQK_B, DV_B), jnp.float32), jnp.zeros((b, H_B, DQK_B), jnp.float32),
            jnp.full((b, H_B), M_INIT, jnp.float32))
    outs_l, outs_c = [], []
    for direction in range(2):
        flip = (lambda a: jnp.flip(a, axis=2)) if direction else (lambda a: a)
        st, hc = mlstm_scan(flip(qc), flip(kc), flip(vc), flip(ic[direction]), flip(fc[direction]), init)
        _, hl = mlstm_scan(flip(ql), flip(kl), flip(vl), flip(il[direction]), flip(fl[direction]), st)
        outs_l.append(flip(hl))
        outs_c.append(flip(hc))

    def finish(hsum, o):
        bb, _, t, _ = hsum.shape
        hn = rmsnorm(hsum.transpose(0, 2, 1, 3), out_norm_g.reshape(H_B, DV_B))
        return (jax.nn.sigmoid(o.astype(jnp.float32)) * hn.reshape(bb, t, W_B)).astype(o.dtype)

    return finish(outs_l[0] + outs_l[1], lat[3]), finish(outs_c[0] + outs_c[1], cx[3])


def ab_mixer(h_l, h_c, rows, cols, w_in, q_norm_g, kv_norm_g, w_q_up, w_kv_up, qn_g, kn_g,
             conv_w, gate_b, out_norm_g, w_out):
    sizes = [Q_LORA, KV_LORA, D_ROPE, H_B * DQK_B, H_B * DQK_B, W_B, W_B, 4 * H_B]
    cuts = np.cumsum(sizes)[:-1].tolist()
    pl = jnp.split(h_l @ w_in, cuts, axis=-1)
    pc = jnp.split(h_c @ w_in, cuts, axis=-1)
    q_l, k_l, v_l = mla_heads(pl[0], pl[1], pl[2], q_norm_g, kv_norm_g, w_q_up, w_kv_up, qn_g, kn_g, rows, cols)
    q_c, k_c, v_c = mla_heads(pc[0], pc[1], pc[2], q_norm_g, kv_norm_g, w_q_up, w_kv_up, qn_g, kn_g, None, None)
    scale = (D_NOPE + D_ROPE) ** -0.5
    a_l = blocked_softmax_attention(q_l, jnp.concatenate([k_l, k_c], axis=1), jnp.concatenate([v_l, v_c], axis=1), scale)
    a_c = softmax_attention(q_c, k_c, v_c, scale)

    def mlstm_inputs(p):
        qk = jax.nn.silu(dwconv_centred(jnp.concatenate([p[3], p[4]], axis=-1), conv_w))
        q, k = jnp.split(qk, 2, axis=-1)
        return (q, k, p[5], p[6], p[7] + gate_b)

    m_l, m_c = mlstm_mixer(mlstm_inputs(pl), mlstm_inputs(pc), out_norm_g)
    b, s, _ = h_l.shape
    bc, t, _ = h_c.shape
    y_l = jnp.concatenate([a_l.reshape(b, s, W_A), m_l], axis=-1) @ w_out
    y_c = jnp.concatenate([a_c.reshape(bc, t, W_A), m_c], axis=-1) @ w_out
    return y_l, y_c


def na_mixer(h_l, h_c, w_qkv, qn_g, kn_g, rpb, w_out, need_ctx):
    b, s, _ = h_l.shape
    bc, t, _ = h_c.shape
    dc = H_C * DH_C
    rows_n = s // GRID_W
    kh = min(NA_KH, rows_n)
    qkv = (h_l @ w_qkv).reshape(b, s, 3, H_C, DH_C)
    q_l, k_l, v_l = rmsnorm(qkv[:, :, 0], qn_g), rmsnorm(qkv[:, :, 1], kn_g), qkv[:, :, 2]
    kv_c = (h_c @ w_qkv[:, dc:]).reshape(bc, t, 2, H_C, DH_C)
    k_c, v_c = rmsnorm(kv_c[:, :, 0], kn_g), kv_c[:, :, 1]
    qg = q_l.reshape(b, rows_n, GRID_W, H_C, DH_C)
    kg = k_l.reshape(b, rows_n, GRID_W, H_C, DH_C)
    vg = v_l.reshape(b, rows_n, GRID_W, H_C, DH_C)
    col_start = np.clip(np.arange(GRID_W) - NA_KW // 2, 0, GRID_W - NA_KW)
    cidx = col_start[:, None] + np.arange(NA_KW)[None, :]
    col_off = cidx - np.arange(GRID_W)[:, None] + (NA_KW - 1)
    scale = DH_C ** -0.5
    n_nb = kh * NA_KW

    def row_block(inp):
        r, q_r = inp
        r0 = jnp.clip(r - kh // 2, 0, rows_n - kh)
        kb = lax.dynamic_slice_in_dim(kg, r0, kh, axis=1)[:, :, cidx]
        vb = lax.dynamic_slice_in_dim(vg, r0, kh, axis=1)[:, :, cidx]
        row_off = r0 + jnp.arange(kh) - r + (NA_KH - 1)
        bias = rpb[:, row_off][:, :, col_off]
        s_nb = jnp.einsum('bqhd,bkqjhd->bhqkj', q_r, kb, preferred_element_type=jnp.float32) * scale
        s_nb = (s_nb + jnp.transpose(bias, (0, 2, 1, 3))[None].astype(jnp.float32)).reshape(b, H_C, GRID_W, n_nb)
        s_cx = jnp.einsum('bqhd,bchd->bhqc', q_r, k_c, preferred_element_type=jnp.float32) * scale
        p = jax.nn.softmax(jnp.concatenate([s_nb, s_cx], axis=-1), axis=-1).astype(v_c.dtype)
        p_nb = p[..., :n_nb].reshape(b, H_C, GRID_W, kh, NA_KW)
        return jnp.einsum('bhqkj,bkqjhd->bqhd', p_nb, vb) + jnp.einsum('bhqc,bchd->bqhd', p[..., n_nb:], v_c)

    o = lax.map(row_block, (jnp.arange(rows_n), jnp.moveaxis(qg, 1, 0)))
    y_l = jnp.moveaxis(o, 0, 1).reshape(b, s, dc) @ w_out
    if not need_ctx:
        return y_l, None
    q_c = rmsnorm((h_c @ w_qkv[:, :dc]).reshape(bc, t, H_C, DH_C), qn_g)
    y_c = softmax_attention(q_c, k_c, v_c, scale).reshape(bc, t, dc) @ w_out
    return y_l, y_c


def hier_moe(h, w_rg, b_rg, w_re, b_re, w_gate, w_up, w_down):
    n = h.shape[0]
    g_logits = jnp.dot(h, w_rg, preferred_element_type=jnp.float32) + b_rg.astype(jnp.float32)
    g_prob = jax.nn.softmax(g_logits, axis=-1)
    g_idx = jnp.argmax(g_logits, axis=-1)
    p_g = jnp.take_along_axis(g_prob, g_idx[:, None], axis=-1)
    e_logits = (jnp.dot(h, w_re, preferred_element_type=jnp.float32) + b_re.astype(jnp.float32)).reshape(n, N_GROUPS, EXPERTS_PER_GROUP)
    e_sel = jnp.take_along_axis(e_logits, g_idx[:, None, None], axis=1)[:, 0]
    top_p, top_i = lax.top_k(jax.nn.softmax(e_sel, axis=-1), TOP_K_IN_GROUP)
    wts = top_p / jnp.sum(top_p, axis=-1, keepdims=True) * p_g
    ids = g_idx[:, None] * EXPERTS_PER_GROUP + top_i
    combine = jnp.sum(jax.nn.one_hot(ids, N_EXPERTS, dtype=jnp.float32) * wts[..., None], axis=1)
    a = jnp.einsum('nd,edf->nef', h, w_gate)
    u = jnp.einsum('nd,edf->nef', h, w_up)
    y = jax.nn.silu(a) * u * combine[..., None].astype(h.dtype)
    return jnp.einsum('nef,efd->nd', y, w_down)


def setup_inputs(seed: int = 0) -> dict:
    key = jax.random.key(seed)
    ks = iter(jax.random.split(key, 64))
    D = D_MODEL
    L = DEPTH
    NE = (DEPTH + 1) // 2
    NO = DEPTH // 2

    def nrm(shape, s):
        return jax.random.normal(next(ks), shape, jnp.float32) * s

    def gain(shape):
        return 1.0 + nrm(shape, 0.02)

    inp = {}
    inp['x'] = nrm((BATCH, SEQ, D), 1.0)
    inp['c'] = nrm((BATCH, D), 1.0)
    inp['ctx'] = nrm((BATCH, CTX_LEN, D), 1.0)
    inp['c_ctx'] = nrm((D,), 1.0)
    inp['ada_w'] = nrm((L, D, 6 * D), 0.5 * D ** -0.5)
    inp['ada_b'] = nrm((L, 6 * D), 0.02)
    inp['norm1_g'] = gain((L, D))
    inp['norm2_g'] = gain((L, D))
    inp['ab_w_in'] = nrm((NE, D, IN_AB), D ** -0.5)
    inp['mla_q_norm_g'] = gain((NE, Q_LORA))
    inp['mla_kv_norm_g'] = gain((NE, KV_LORA))
    inp['mla_w_q_up'] = nrm((NE, Q_LORA, H_A * (D_NOPE + D_ROPE)), Q_LORA ** -0.5)
    inp['mla_w_kv_up'] = nrm((NE, KV_LORA, H_A * (D_NOPE + D_V_A)), KV_LORA ** -0.5)
    inp['mla_qn_g'] = gain((NE, D_NOPE + D_ROPE))
    inp['mla_kn_g'] = gain((NE, D_NOPE + D_ROPE))
    inp['mlstm_conv_w'] = nrm((NE, CONV_K, 2 * H_B * DQK_B), CONV_K ** -0.5)
    ig_b = nrm((NE, 2, H_B), 0.1)
    fg_b = jnp.linspace(3.0, 6.0, H_B)[None, None, :] + nrm((NE, 2, H_B), 0.1)
    inp['mlstm_gate_b'] = jnp.stack([ig_b, fg_b], axis=2).reshape(NE, 4 * H_B)
    inp['mlstm_out_norm_g'] = gain((NE, W_B))
    inp['ab_w_out'] = nrm((NE, W_A + W_B, D), (W_A + W_B) ** -0.5)
    inp['na_w_qkv'] = nrm((NO, D, 3 * H_C * DH_C), D ** -0.5)
    inp['na_qn_g'] = gain((NO, DH_C))
    inp['na_kn_g'] = gain((NO, DH_C))
    inp['na_rpb'] = nrm((NO, H_C, 2 * NA_KH - 1, 2 * NA_KW - 1), 0.2)
    inp['na_w_out'] = nrm((NO, H_C * DH_C, D), (H_C * DH_C) ** -0.5)
    inp['moe_w_rg'] = nrm((L, D, N_GROUPS), D ** -0.5)
    inp['moe_b_rg'] = nrm((L, N_GROUPS), 0.01)
    inp['moe_w_re'] = nrm((L, D, N_EXPERTS), D ** -0.5)
    inp['moe_b_re'] = nrm((L, N_EXPERTS), 0.01)
    inp['moe_w_gate'] = nrm((L, N_EXPERTS, D, D_FF_EXPERT), D ** -0.5)
    inp['moe_w_up'] = nrm((L, N_EXPERTS, D, D_FF_EXPERT), D ** -0.5)
    inp['moe_w_down'] = nrm((L, N_EXPERTS, D_FF_EXPERT, D), D_FF_EXPERT ** -0.5)
    return inp


def reference(x, c, ctx, c_ctx, ada_w, ada_b, norm1_g, norm2_g, ab_w_in, mla_q_norm_g, mla_kv_norm_g,
              mla_w_q_up, mla_w_kv_up, mla_qn_g, mla_kn_g, mlstm_conv_w, mlstm_gate_b, mlstm_out_norm_g,
              ab_w_out, na_w_qkv, na_qn_g, na_kn_g, na_rpb, na_w_out, moe_w_rg, moe_b_rg, moe_w_re,
              moe_b_re, moe_w_gate, moe_w_up, moe_w_down):
    b, s, d = x.shape
    t = jnp.arange(s)
    rows, cols = t // GRID_W, t % GRID_W
    sc_l = jax.nn.silu(c)
    sc_c = jax.nn.silu(c_ctx)
    for layer in range(DEPTH):
        last = layer == DEPTH - 1
        j = layer // 2
        mod_l = [m[:, None, :] for m in jnp.split(sc_l @ ada_w[layer] + ada_b[layer], 6, axis=-1)]
        mod_c = jnp.split(sc_c @ ada_w[layer] + ada_b[layer], 6, axis=-1)
        h_l = modulate(x, norm1_g[layer], mod_l[0], mod_l[1])
        h_c = modulate(ctx, norm1_g[layer], mod_c[0], mod_c[1])
        if layer % 2 == 0:
            y_l, y_c = ab_mixer(h_l, h_c, rows, cols, ab_w_in[j], mla_q_norm_g[j], mla_kv_norm_g[j],
                                mla_w_q_up[j], mla_w_kv_up[j], mla_qn_g[j], mla_kn_g[j], mlstm_conv_w[j],
                                mlstm_gate_b[j], mlstm_out_norm_g[j], ab_w_out[j])
        else:
            y_l, y_c = na_mixer(h_l, h_c, na_w_qkv[j], na_qn_g[j], na_kn_g[j], na_rpb[j], na_w_out[j], not last)
        x = x + mod_l[2] * y_l
        moe_args = (moe_w_rg[layer], moe_b_rg[layer], moe_w_re[layer], moe_b_re[layer],
                    moe_w_gate[layer], moe_w_up[layer], moe_w_down[layer])
        h2_l = modulate(x, norm2_g[layer], mod_l[3], mod_l[4]).reshape(b * s, d)
        if last:
            x = x + mod_l[5] * hier_moe(h2_l, *moe_args).reshape(b, s, d)
        else:
            ctx = ctx + mod_c[2] * y_c
            h2_c = modulate(ctx, norm2_g[layer], mod_c[3], mod_c[4]).reshape(-1, d)
            y2 = hier_moe(jnp.concatenate([h2_l, h2_c], axis=0), *moe_args)
            x = x + mod_l[5] * y2[:b * s].reshape(b, s, d)
            ctx = ctx + mod_c[5] * y2[b * s:].reshape(ctx.shape)
    return x
```

```python
import functools

import numpy as np
import jax
import jax.numpy as jnp
from jax import lax
from jax.experimental import pallas as pl
from jax.experimental.pallas import tpu as pltpu

F32 = jnp.float32
BF16 = jnp.bfloat16

D_MODEL = 4096
GRID_W = 64
H_A = 16
Q_LORA = 1024
KV_LORA = 512
D_NOPE = 128
D_ROPE = 64
D_V_A = 128
H_B = 4
DQK_B = 256
DV_B = 512
MLSTM_CHUNK = 128
M_INIT = -1e30
H_C = 32
DH_C = 128
NA_KH = 8
NA_KW = 16
N_GROUPS = 4
EXPERTS_PER_GROUP = 8
N_EXPERTS = 32
D_FF_EXPERT = 192
ROPE_THETA = 10000.0
NORM_EPS = 1e-6
W_A = H_A * D_V_A
W_B = H_B * DV_B
NEG = -1e30

VMEM_LIMIT_BYTES = 56 * 1024 * 1024
ROW_BLOCK = 256
MOD_ROWS = 8

P_QK = 0
P_V = 2048
P_O = 4096
P_CQ = 6144
P_CKV = 7168
P_KR = 7680
P_GATE = 7808
P_WIDTH = 8192


def _cparams(sem):
    return pltpu.CompilerParams(dimension_semantics=sem, vmem_limit_bytes=VMEM_LIMIT_BYTES)


def _mod_row(i, rows_per_block, n_lat, seq, batch):
    return jnp.where(i < n_lat // rows_per_block, i // (seq // rows_per_block), batch)


def _ada_body(c_ref, w_ref, b_ref, o_ref):
    c = c_ref[...]
    sc = (c * jax.nn.sigmoid(c)).astype(BF16)
    o_ref[0] = jnp.dot(sc, w_ref[0].astype(BF16), preferred_element_type=F32) + b_ref[0]


def ada_mods(cvec, ada_w, ada_b, tn=512):
    depth, d, n6 = ada_w.shape
    return pl.pallas_call(
        _ada_body,
        out_shape=jax.ShapeDtypeStruct((depth, MOD_ROWS, n6), F32),
        grid=(depth, n6 // tn),
        in_specs=[
            pl.BlockSpec((MOD_ROWS, d), lambda l, j: (0, 0)),
            pl.BlockSpec((1, d, tn), lambda l, j: (l, 0, j)),
            pl.BlockSpec((1, 1, tn), lambda l, j: (l, 0, j)),
        ],
        out_specs=pl.BlockSpec((1, MOD_ROWS, tn), lambda l, j: (l, 0, j)),
        compiler_params=_cparams(("arbitrary", "arbitrary")),
        name="ada_mods",
    )(cvec, ada_w, ada_b.reshape(depth, 1, n6))


def _modulated_norm(x_ref, g_ref, sh_ref, sc_ref):
    x = x_ref[...]
    ms = jnp.mean(x * x, axis=-1, keepdims=True)
    y = x * lax.rsqrt(ms + NORM_EPS) * g_ref[...]
    return y * (1.0 + sc_ref[0]) + sh_ref[0]


def _norm_body(x_ref, g_ref, sh_ref, sc_ref, o_ref):
    o_ref[...] = _modulated_norm(x_ref, g_ref, sh_ref, sc_ref).astype(o_ref.dtype)


def _norm_router_body(x_ref, g_ref, sh_ref, sc_ref, wr_ref, br_ref, h_ref, comb_ref):
    h = _modulated_norm(x_ref, g_ref, sh_ref, sc_ref)
    h_ref[...] = h.astype(h_ref.dtype)
    logits = jnp.dot(h, wr_ref[...], precision=lax.Precision.HIGHEST,
                     preferred_element_type=F32) + br_ref[...]
    lane = lax.broadcasted_iota(jnp.int32, logits.shape, 1)
    big = jnp.int32(1 << 20)
    is_g = (lane >= N_EXPERTS) & (lane < N_EXPERTS + N_GROUPS)
    gl = jnp.where(is_g, logits, NEG)
    gmax = jnp.max(gl, axis=-1, keepdims=True)
    g_idx = jnp.min(jnp.where(gl == gmax, lane, big), axis=-1, keepdims=True) - N_EXPERTS
    p_g = 1.0 / jnp.sum(jnp.where(is_g, jnp.exp(gl - gmax), 0.0), axis=-1, keepdims=True)
    lo = g_idx * EXPERTS_PER_GROUP
    in_grp = (lane >= lo) & (lane < lo + EXPERTS_PER_GROUP)
    el = jnp.where(in_grp, logits, NEG)
    e1 = jnp.max(el, axis=-1, keepdims=True)
    i1 = jnp.min(jnp.where(el == e1, lane, big), axis=-1, keepdims=True)
    el2 = jnp.where(lane == i1, NEG, el)
    e2 = jnp.max(el2, axis=-1, keepdims=True)
    i2 = jnp.min(jnp.where(el2 == e2, lane, big), axis=-1, keepdims=True)
    t = jnp.exp(e2 - e1)
    w1 = p_g / (1.0 + t)
    w2 = w1 * t
    comb_ref[...] = jnp.where(lane == i1, w1, 0.0) + jnp.where(lane == i2, w2, 0.0)


def _norm_specs(d, mods_k, dims):
    n_lat, seq, batch = dims
    mrow = functools.partial(_mod_row, rows_per_block=ROW_BLOCK, n_lat=n_lat, seq=seq, batch=batch)
    k_shift, k_scale = mods_k
    return [
        pl.BlockSpec((ROW_BLOCK, d), lambda i: (i, 0)),
        pl.BlockSpec((1, d), lambda i: (0, 0)),
        pl.BlockSpec((1, 1, d), lambda i: (mrow(i) * 6 + k_shift, 0, 0)),
        pl.BlockSpec((1, 1, d), lambda i: (mrow(i) * 6 + k_scale, 0, 0)),
    ]


def norm_mod(x, g, mods, mods_k, dims):
    n, d = x.shape
    return pl.pallas_call(
        _norm_body,
        out_shape=jax.ShapeDtypeStruct((n, d), BF16),
        grid=(n // ROW_BLOCK,),
        in_specs=_norm_specs(d, mods_k, dims),
        out_specs=pl.BlockSpec((ROW_BLOCK, d), lambda i: (i, 0)),
        compiler_params=_cparams(("parallel",)),
        name="norm_mod",
    )(x, g.reshape(1, d), mods, mods)


def norm_router(x, g, mods, mods_k, dims, w_r, b_r):
    n, d = x.shape
    return pl.pallas_call(
        _norm_router_body,
        out_shape=(jax.ShapeDtypeStruct((n, d), BF16), jax.ShapeDtypeStruct((n, 128), F32)),
        grid=(n // ROW_BLOCK,),
        in_specs=_norm_specs(d, mods_k, dims) + [
            pl.BlockSpec((d, 128), lambda i: (0, 0)),
            pl.BlockSpec((1, 128), lambda i: (0, 0)),
        ],
        out_specs=(pl.BlockSpec((ROW_BLOCK, d), lambda i: (i, 0)),
                   pl.BlockSpec((ROW_BLOCK, 128), lambda i: (i, 0))),
        compiler_params=_cparams(("parallel",)),
        name="norm_router",
    )(x, g.reshape(1, d), mods, mods, w_r, b_r)


def _mm_body(*refs, rms, gated):
    it = iter(refs)
    a_ref = next(it)
    g_ref = next(it) if rms else None
    w_ref = next(it)
    res_ref = next(it) if gated else None
    gate_ref = next(it) if gated else None
    o_ref = next(it)
    wbf_ref = next(it)

    @pl.when(pl.program_id(1) == 0)
    def _():
        wbf_ref[...] = w_ref[...].astype(BF16)

    a = a_ref[...]
    if rms:
        af = a.astype(F32)
        ms = jnp.mean(af * af, axis=-1, keepdims=True)
        a = (af * lax.rsqrt(ms + NORM_EPS) * g_ref[...]).astype(BF16)
    acc = jnp.dot(a, wbf_ref[...], preferred_element_type=F32)
    if gated:
        acc = res_ref[...] + gate_ref[0] * acc
    o_ref[...] = acc.astype(o_ref.dtype)


def matmul(a, w, *, out_dtype, a_col=0, k=None, rms_gain=None, resid=None, mods=None, k_gate=None,
           dims=None, tm=512, tn=512):
    m = a.shape[0]
    k = w.shape[0] if k is None else k
    nw = w.shape[1]
    assert w.shape[0] == k and m % tm == 0 and nw % tn == 0
    rms = rms_gain is not None
    gated = resid is not None
    in_specs = [pl.BlockSpec((tm, k), lambda j, i: (i, a_col))]
    args = [a]
    if rms:
        in_specs.append(pl.BlockSpec((1, k), lambda j, i: (0, 0)))
        args.append(rms_gain.reshape(1, k))
    in_specs.append(pl.BlockSpec((k, tn), lambda j, i: (0, j)))
    args.append(w)
    if gated:
        n_lat, seq, batch = dims
        mrow = functools.partial(_mod_row, rows_per_block=tm, n_lat=n_lat, seq=seq, batch=batch)
        in_specs.append(pl.BlockSpec((tm, tn), lambda j, i: (i, j)))
        in_specs.append(pl.BlockSpec((1, 1, tn), lambda j, i: (mrow(i) * 6 + k_gate, 0, j)))
        args += [resid, mods]
    return pl.pallas_call(
        functools.partial(_mm_body, rms=rms, gated=gated),
        out_shape=jax.ShapeDtypeStruct((m, nw), out_dtype),
        grid=(nw // tn, m // tm),
        in_specs=in_specs,
        out_specs=pl.BlockSpec((tm, tn), lambda j, i: (i, j)),
        scratch_shapes=[pltpu.VMEM((k, tn), BF16)],
        compiler_params=_cparams(("arbitrary", "arbitrary")),
        name="matmul",
    )(*args)


def _rope_rot(pr, trig):
    if trig is not None:
        pr = pr * trig
    else:
        lane = lax.broadcasted_iota(jnp.int32, pr.shape, 1)
        pr = jnp.where(lane < D_ROPE, pr, 0.0)
    return pr + pltpu.roll(pr, D_ROPE, axis=1)


def _mla_body(*refs, n_lat_keys, n_ctx_keys, rope_q, scale, key_chunk):
    it = iter(refs)
    q_ref = next(it)
    kvl_ref = next(it) if n_lat_keys else None
    krl_ref = next(it) if n_lat_keys else None
    tkl_ref = next(it) if n_lat_keys else None
    kvc_ref = next(it)
    krc_ref = next(it)
    tq_ref = next(it) if rope_q else None
    gq_ref = next(it)
    gk_ref = next(it)
    o_ref = next(it)
    kbuf = next(it)
    vbuf = next(it)
    dqk = D_NOPE + D_ROPE

    def prep_keys(kv_ref, kr_ref, trig_ref, off, n):
        gk = gk_ref[...]

        def chunk(c, carry):
            rows = pl.ds(pl.multiple_of(c * key_chunk, key_chunk), key_chunk)
            kv = kv_ref[rows, :]
            kn = kv[:, :D_NOPE].astype(F32)
            kr = kr_ref[rows, :].astype(F32)
            lane = lax.broadcasted_iota(jnp.int32, kr.shape, 1)
            ss = (jnp.sum(kn * kn, axis=-1, keepdims=True)
                  + jnp.sum(jnp.where(lane < D_ROPE, kr * kr, 0.0), axis=-1, keepdims=True))
            r = lax.rsqrt(ss / dqk + NORM_EPS)
            trig = trig_ref[rows, :] if trig_ref is not None else None
            rot = _rope_rot(kr * (r * gk[:, D_NOPE:]), trig)
            orow = pl.ds(pl.multiple_of(off + c * key_chunk, key_chunk), key_chunk)
            kbuf[orow, :D_NOPE] = (kn * (r * gk[:, :D_NOPE])).astype(BF16)
            kbuf[orow, D_NOPE:] = rot.astype(BF16)
            vbuf[orow, :] = kv[:, D_NOPE:].astype(BF16)
            return carry

        lax.fori_loop(0, n // key_chunk, chunk, 0)

    @pl.when(pl.program_id(2) == 0)
    def _():
        if n_lat_keys:
            prep_keys(kvl_ref, krl_ref, tkl_ref, 0, n_lat_keys)
        prep_keys(kvc_ref, krc_ref, None, n_lat_keys, n_ctx_keys)

    q = q_ref[...].astype(F32)
    qn = q[:, :D_NOPE]
    qr = q[:, D_NOPE:]
    lane = lax.broadcasted_iota(jnp.int32, qr.shape, 1)
    ss = (jnp.sum(qn * qn, axis=-1, keepdims=True)
          + jnp.sum(jnp.where(lane < D_ROPE, qr * qr, 0.0), axis=-1, keepdims=True))
    r = lax.rsqrt(ss / dqk + NORM_EPS) * scale
    gq = gq_ref[...]
    rot = _rope_rot(qr * (r * gq[:, D_NOPE:]), tq_ref[...] if rope_q else None)
    rot = jnp.where(lane < D_ROPE, rot, 0.0)
    q2 = jnp.concatenate([qn * (r * gq[:, :D_NOPE]), rot], axis=1).astype(BF16)
    s = lax.dot_general(q2, kbuf[...], (((1,), (1,)), ((), ())), preferred_element_type=F32)
    m = jnp.max(s, axis=-1, keepdims=True)
    p = jnp.exp(s - m)
    l = jnp.sum(p, axis=-1, keepdims=True)
    o = jnp.dot(p.astype(BF16), vbuf[...], preferred_element_type=F32)
    o_ref[...] = (o / l).astype(o_ref.dtype)


def mla_attention(q_all, kv_all, p_all, trig, gq, gk, *, batch, seq, ctx_len, latent_queries, tq=256):
    n_lat = batch * seq
    kr_blk = P_KR // 128
    scale = (D_NOPE + D_ROPE) ** -0.5
    if latent_queries:
        nq, n_lat_keys, q_row0 = seq, seq, 0
    else:
        nq, n_lat_keys, q_row0 = ctx_len, 0, n_lat // tq
    nqb = nq // tq
    ctx_blk0 = n_lat // ctx_len
    in_specs = [pl.BlockSpec((tq, 256), lambda b, h, i: (q_row0 + b * nqb + i, h))]
    args = [q_all]
    if latent_queries:
        in_specs += [
            pl.BlockSpec((seq, 256), lambda b, h, i: (b, h)),
            pl.BlockSpec((seq, 128), lambda b, h, i: (b, kr_blk)),
            pl.BlockSpec((seq, 128), lambda b, h, i: (0, 0)),
        ]
        args += [kv_all, p_all, trig]
    in_specs += [
        pl.BlockSpec((ctx_len, 256), lambda b, h, i: (ctx_blk0 + b, h)),
        pl.BlockSpec((ctx_len, 128), lambda b, h, i: (ctx_blk0 + b, kr_blk)),
    ]
    args += [kv_all, p_all]
    if latent_queries:
        in_specs.append(pl.BlockSpec((tq, 128), lambda b, h, i: (i, 0)))
        args.append(trig)
    in_specs += [pl.BlockSpec((1, 256), lambda b, h, i: (0, 0))] * 2
    args += [gq, gk]
    nk = n_lat_keys + ctx_len
    body = functools.partial(_mla_body, n_lat_keys=n_lat_keys, n_ctx_keys=ctx_len, rope_q=latent_queries,
                             scale=scale, key_chunk=min(256, ctx_len))
    return pl.pallas_call(
        body,
        out_shape=jax.ShapeDtypeStruct((batch * nq, W_A), BF16),
        grid=(batch, H_A, nqb),
        in_specs=in_specs,
        out_specs=pl.BlockSpec((tq, D_V_A), lambda b, h, i: (b * nqb + i, h)),
        scratch_shapes=[pltpu.VMEM((nk, 256), BF16), pltpu.VMEM((nk, D_V_A), BF16)],
        compiler_params=_cparams(("arbitrary", "arbitrary", "arbitrary")),
        name="mla_attention",
    )(*args)


def _conv_silu_body(x_ref, prev_ref, next_ref, w_ref, o_ref, *, lat_blocks, seq_blocks, ctx_blocks):
    i = pl.program_id(0)
    j = jnp.where(i < lat_blocks, i % seq_blocks, (i - lat_blocks) % ctx_blocks)
    nb = jnp.where(i < lat_blocks, seq_blocks, ctx_blocks)
    x = x_ref[...]
    rows = x.shape[0]
    row = lax.broadcasted_iota(jnp.int32, x.shape, 0)
    prev_row = jnp.where(j == 0, 0.0, prev_ref[7:8, :])
    next_row = jnp.where(j == nb - 1, 0.0, next_ref[0:1, :])
    xp = jnp.where(row == 0, prev_row, pltpu.roll(x, 1, axis=0))
    xn = jnp.where(row == rows - 1, next_row, pltpu.roll(x, rows - 1, axis=0))
    w = w_ref[...]
    y = xp * w[0:1, :] + x * w[1:2, :] + xn * w[2:3, :]
    y = y * jax.nn.sigmoid(y)
    half = y.shape[1] // 2
    o_ref[:, :half] = y[:, :half].astype(o_ref.dtype)
    o_ref[:, half:] = (y[:, half:] * (DQK_B ** -0.5)).astype(o_ref.dtype)


def conv_silu(p_all, conv_w, *, n_lat, seq, ctx_len):
    n = p_all.shape[0]
    width = 2 * H_B * DQK_B
    rb = ROW_BLOCK
    nblk = n // rb
    sub = rb // 8
    last8 = n // 8 - 1
    body = functools.partial(_conv_silu_body, lat_blocks=n_lat // rb, seq_blocks=seq // rb,
                             ctx_blocks=ctx_len // rb)
    return pl.pallas_call(
        body,
        out_shape=jax.ShapeDtypeStruct((n, width), BF16),
        grid=(nblk,),
        in_specs=[
            pl.BlockSpec((rb, width), lambda i: (i, P_QK // width)),
            pl.BlockSpec((8, width), lambda i: (jnp.maximum(i * sub - 1, 0), P_QK // width)),
            pl.BlockSpec((8, width), lambda i: (jnp.minimum((i + 1) * sub, last8), P_QK // width)),
            pl.BlockSpec((3, width), lambda i: (0, 0)),
        ],
        out_specs=pl.BlockSpec((rb, width), lambda i: (i, 0)),
        compiler_params=_cparams(("parallel",)),
        name="conv_silu",
    )(p_all, p_all, p_all, conv_w)


def _log_sigmoid(x):
    return jnp.minimum(x, 0.0) - jnp.log(1.0 + jnp.exp(-jnp.abs(x)))


def _mlstm_body(q_ref, k_ref, v_ref, gc_ref, gr_ref, bc_ref, br_ref, o_ref, c_st, n_st, m_st):
    h = pl.program_id(1)
    direction = pl.program_id(2)
    c = pl.program_id(3)
    L = MLSTM_CHUNK
    hi = lax.Precision.HIGHEST

    @pl.when(c == 0)
    def _():
        c_st[...] = jnp.zeros_like(c_st)
        n_st[...] = jnp.zeros_like(n_st)
        m_st[...] = jnp.full_like(m_st, M_INIT)

    gi = direction * (2 * H_B) + h
    gcol = gc_ref[...] + bc_ref[...]
    lane = lax.broadcasted_iota(jnp.int32, gcol.shape, 1)
    li_c = jnp.sum(jnp.where(lane == gi, gcol, 0.0), axis=1, keepdims=True)
    lf_c = _log_sigmoid(jnp.sum(jnp.where(lane == gi + H_B, gcol, 0.0), axis=1, keepdims=True))
    grow = gr_ref[...] + br_ref[...]
    sub = lax.broadcasted_iota(jnp.int32, grow.shape, 0)
    li_r = jnp.sum(jnp.where(sub == gi, grow, 0.0), axis=0, keepdims=True)
    lf_r = _log_sigmoid(jnp.sum(jnp.where(sub == gi + H_B, grow, 0.0), axis=0, keepdims=True))

    t_i = lax.broadcasted_iota(jnp.int32, (L, L), 0)
    s_i = lax.broadcasted_iota(jnp.int32, (L, L), 1)
    sgn = 1 - 2 * direction
    incl = (s_i - t_i) * sgn <= 0
    incl_f = incl.astype(F32)
    incl_t = ((t_i - s_i) * sgn <= 0).astype(F32)
    bcum_c = jnp.dot(incl_f, jnp.broadcast_to(lf_c, (L, L)), precision=hi, preferred_element_type=F32)
    bcum_r = jnp.dot(jnp.broadcast_to(lf_r, (8, L)), incl_t, precision=hi, preferred_element_type=F32)[0:1, :]
    b_last = jnp.sum(lf_r, axis=1, keepdims=True)

    m_prev = m_st[...]
    a_c = bcum_c[:, 0:1] + m_prev
    dlog = jnp.where(incl, bcum_c - bcum_r + li_r, -jnp.inf)
    m_t = jnp.maximum(a_c, jnp.max(dlog, axis=1, keepdims=True))
    w_intra = jnp.exp(dlog - m_t)
    w_inter = jnp.exp(a_c - m_t)

    q = q_ref[...]
    k = k_ref[...]
    v = v_ref[...].astype(BF16)
    c_prev = c_st[...]
    n_prev = n_st[...]
    s = lax.dot_general(q, k, (((1,), (1,)), ((), ())), preferred_element_type=F32) * w_intra
    num = (w_inter * jnp.dot(q, c_prev.astype(BF16), preferred_element_type=F32)
           + jnp.dot(s.astype(BF16), v, preferred_element_type=F32))
    den = (w_inter * jnp.sum(q.astype(F32) * n_prev, axis=1, keepdims=True)
           + jnp.sum(s, axis=1, keepdims=True))
    o_ref[0] = num / jnp.maximum(jnp.abs(den), jnp.exp(-m_t))

    g_r = b_last - bcum_r + li_r
    g_c = b_last - bcum_c[:, 0:1] + li_c
    m_new = jnp.maximum(b_last + m_prev, jnp.max(g_r, axis=1, keepdims=True))
    decay = jnp.exp(b_last + m_prev - m_new)
    kw = k.astype(F32) * jnp.exp(g_c - m_new)
    c_st[...] = decay * c_prev + lax.dot_general(kw.astype(BF16), v, (((0,), (0,)), ((), ())),
                                                 preferred_element_type=F32)
    n_st[...] = decay * n_prev + jnp.sum(kw, axis=0, keepdims=True)
    m_st[...] = m_new


def mlstm_scan(qk, p_all, gates_t, gate_b, *, batch, seq, ctx_len):
    n = qk.shape[0]
    L = MLSTM_CHUNK
    cc, lc = ctx_len // L, seq // L
    n_lat_blk = batch * lc

    def rowblk(b, d, c):
        ctx_blk = n_lat_blk + b * cc + jnp.where(d == 0, c, cc - 1 - c)
        lat_blk = b * lc + jnp.where(d == 0, c - cc, lc - 1 - (c - cc))
        return jnp.where(c < cc, ctx_blk, lat_blk)

    gb_col = jnp.zeros((1, 128), F32).at[0, :4 * H_B].set(gate_b)
    gb_row = gate_b.reshape(4 * H_B, 1)
    return pl.pallas_call(
        _mlstm_body,
        out_shape=jax.ShapeDtypeStruct((2, n, W_B), F32),
        grid=(batch, H_B, 2, cc + lc),
        in_specs=[
            pl.BlockSpec((L, DQK_B), lambda b, h, d, c: (rowblk(b, d, c), h)),
            pl.BlockSpec((L, DQK_B), lambda b, h, d, c: (rowblk(b, d, c), H_B + h)),
            pl.BlockSpec((L, DV_B), lambda b, h, d, c: (rowblk(b, d, c), P_V // DV_B + h)),
            pl.BlockSpec((L, 128), lambda b, h, d, c: (rowblk(b, d, c), P_GATE // 128)),
            pl.BlockSpec((4 * H_B, L), lambda b, h, d, c: (0, rowblk(b, d, c))),
            pl.BlockSpec((1, 128), lambda b, h, d, c: (0, 0)),
            pl.BlockSpec((4 * H_B, 1), lambda b, h, d, c: (0, 0)),
        ],
        out_specs=pl.BlockSpec((1, L, DV_B), lambda b, h, d, c: (d, rowblk(b, d, c), h)),
        scratch_shapes=[pltpu.VMEM((DQK_B, DV_B), F32), pltpu.VMEM((1, DQK_B), F32), pltpu.VMEM((1, 1), F32)],
        compiler_params=_cparams(("arbitrary",) * 4),
        name="mlstm_scan",
    )(qk, qk, p_all, p_all, gates_t, gb_col, gb_row)


def _mlstm_finish_body(h_ref, o_ref, g_ref, out_ref):
    hs = h_ref[0] + h_ref[1]
    g = g_ref[...]
    og = o_ref[...]
    for hd in range(H_B):
        sl = slice(hd * DV_B, (hd + 1) * DV_B)
        x = hs[:, sl]
        ms = jnp.mean(x * x, axis=-1, keepdims=True)
        hn = x * lax.rsqrt(ms + NORM_EPS) * g[:, sl]
        out_ref[:, sl] = (jax.nn.sigmoid(og[:, sl]) * hn).astype(out_ref.dtype)


def mlstm_finish(hdir, p_all, out_norm_g):
    n = hdir.shape[1]
    rb = ROW_BLOCK
    return pl.pallas_call(
        _mlstm_finish_body,
        out_shape=jax.ShapeDtypeStruct((n, W_B), BF16),
        grid=(n // rb,),
        in_specs=[
            pl.BlockSpec((2, rb, W_B), lambda i: (0, i, 0)),
            pl.BlockSpec((rb, W_B), lambda i: (i, P_O // W_B)),
            pl.BlockSpec((1, W_B), lambda i: (0, 0)),
        ],
        out_specs=pl.BlockSpec((rb, W_B), lambda i: (i, 0)),
        compiler_params=_cparams(("parallel",)),
        name="mlstm_finish",
    )(hdir, p_all, out_norm_g.reshape(1, W_B))


NA_ROWS_PER_STEP = 4


def _na_geometry(rows_n):
    r_t = NA_ROWS_PER_STEP
    kh = min(NA_KH, rows_n)
    win = min(r_t + kh - 1, rows_n)
    tiles = rows_n // r_t
    starts, sigs, cls = [], [], []
    for t in range(tiles):
        rq0 = t * r_t
        start = int(np.clip(rq0 - kh // 2, 0, rows_n - win))
        r0 = np.clip(np.arange(rq0, rq0 + r_t) - kh // 2, 0, rows_n - kh)
        sig = (start - rq0, tuple((r0 - rq0).tolist()))
        if sig not in sigs:
            sigs.append(sig)
        starts.append(start)
        cls.append(sigs.index(sig))
    return kh, win, np.array(starts, np.int32), np.array(cls, np.int32), sigs


def na_bias_tables(rpb, rows_n):
    r_t = NA_ROWS_PER_STEP
    kh, win, _, _, sigs = _na_geometry(rows_n)
    cols = np.arange(GRID_W)
    c0 = np.clip(cols - NA_KW // 2, 0, GRID_W - NA_KW)
    col_valid = (cols[None, :] >= c0[:, None]) & (cols[None, :] < c0[:, None] + NA_KW)
    col_off = np.clip(cols[None, :] - cols[:, None] + (NA_KW - 1), 0, 2 * NA_KW - 2)
    tabs = []
    for (dstart, dr0) in sigs:
        a = np.arange(r_t)[:, None]
        j = np.arange(win)[None, :]
        krow = dstart + j
        r0 = np.array(dr0)[:, None]
        row_valid = (krow >= r0) & (krow < r0 + kh)
        row_off = np.clip(krow - a + (NA_KH - 1), 0, 2 * NA_KH - 2)
        b = rpb[:, row_off[:, None, :, None], col_off[None, :, None, :]]
        valid = row_valid[:, None, :, None] & col_valid[None, :, None, :]
        b = jnp.where(valid[None], b, NEG)
        tabs.append(b.reshape(H_C, r_t * GRID_W, win * GRID_W))
    return jnp.stack(tabs, axis=1)


def _head_rmsnorm(x, g):
    xf = x.astype(F32)
    ms = jnp.mean(xf * xf, axis=-1, keepdims=True)
    return xf * lax.rsqrt(ms + NORM_EPS) * g


def _na_body(start_ref, cls_ref, q_ref, k_ref, v_ref, kc_ref, vc_ref, bias_ref, gq_ref, gk_ref, o_ref,
             kn, knc, *, seq, ctx_len, win_rows, scale):
    i = pl.program_id(2)
    chunk = min(512, seq)

    @pl.when(i == 0)
    def _():
        gk = gk_ref[...]

        def body(c, carry):
            rows = pl.ds(pl.multiple_of(c * chunk, chunk), chunk)
            kn[rows, :] = _head_rmsnorm(k_ref[rows, :], gk).astype(BF16)
            return carry

        lax.fori_loop(0, seq // chunk, body, 0)
        knc[...] = _head_rmsnorm(kc_ref[...], gk).astype(BF16)

    q = (_head_rmsnorm(q_ref[...], gq_ref[...]) * scale).astype(BF16)
    krow0 = pl.multiple_of(start_ref[i] * GRID_W, GRID_W)
    kw = kn[pl.ds(krow0, win_rows), :]
    vw = v_ref[pl.ds(krow0, win_rows), :]
    nt = (((1,), (1,)), ((), ()))
    s = lax.dot_general(q, kw, nt, preferred_element_type=F32) + bias_ref[0, 0]
    sc = lax.dot_general(q, knc[...], nt, preferred_element_type=F32)
    m = jnp.maximum(jnp.max(s, axis=-1, keepdims=True), jnp.max(sc, axis=-1, keepdims=True))
    p = jnp.exp(s - m)
    pc = jnp.exp(sc - m)
    l = jnp.sum(p, axis=-1, keepdims=True) + jnp.sum(pc, axis=-1, keepdims=True)
    o = (jnp.dot(p.astype(BF16), vw, preferred_element_type=F32)
         + jnp.dot(pc.astype(BF16), vc_ref[...], preferred_element_type=F32))
    o_ref[...] = (o / l).astype(o_ref.dtype)


def na_attention(qkv, bias_tabs, gq, gk, *, batch, seq, ctx_len):
    rows_n = seq // GRID_W
    r_t = NA_ROWS_PER_STEP
    _, win, starts, cls, _ = _na_geometry(rows_n)
    tq = r_t * GRID_W
    tiles = rows_n // r_t
    n_lat = batch * seq
    ctx_blk0 = n_lat // ctx_len
    body = functools.partial(_na_body, seq=seq, ctx_len=ctx_len, win_rows=win * GRID_W, scale=DH_C ** -0.5)
    grid_spec = pltpu.PrefetchScalarGridSpec(
        num_scalar_prefetch=2,
        grid=(batch, H_C, tiles),
        in_specs=[
            pl.BlockSpec((tq, DH_C), lambda b, h, i, st, cl: (b * tiles + i, h)),
            pl.BlockSpec((seq, DH_C), lambda b, h, i, st, cl: (b, H_C + h)),
            pl.BlockSpec((seq, DH_C), lambda b, h, i, st, cl: (b, 2 * H_C + h)),
            pl.BlockSpec((ctx_len, DH_C), lambda b, h, i, st, cl: (ctx_blk0 + b, H_C + h)),
            pl.BlockSpec((ctx_len, DH_C), lambda b, h, i, st, cl: (ctx_blk0 + b, 2 * H_C + h)),
            pl.BlockSpec((1, 1, tq, win * GRID_W), lambda b, h, i, st, cl: (h, cl[i], 0, 0)),
            pl.BlockSpec((1, DH_C), lambda b, h, i, st, cl: (0, 0)),
            pl.BlockSpec((1, DH_C), lambda b, h, i, st, cl: (0, 0)),
        ],
        out_specs=pl.BlockSpec((tq, DH_C), lambda b, h, i, st, cl: (b * tiles + i, h)),
        scratch_shapes=[pltpu.VMEM((seq, DH_C), BF16), pltpu.VMEM((ctx_len, DH_C), BF16)],
    )
    return pl.pallas_call(
        body,
        out_shape=jax.ShapeDtypeStruct((n_lat, H_C * DH_C), BF16),
        grid_spec=grid_spec,
        compiler_params=_cparams(("arbitrary", "arbitrary", "arbitrary")),
        name="na_attention",
    )(jnp.asarray(starts), jnp.asarray(cls), qkv, qkv, qkv, qkv, qkv, bias_tabs, gq, gk)


def _ctx_attn_body(q_ref, k_ref, v_ref, gq_ref, gk_ref, o_ref, *, scale):
    q = (_head_rmsnorm(q_ref[...], gq_ref[...]) * scale).astype(BF16)
    k = _head_rmsnorm(k_ref[...], gk_ref[...]).astype(BF16)
    s = lax.dot_general(q, k, (((1,), (1,)), ((), ())), preferred_element_type=F32)
    m = jnp.max(s, axis=-1, keepdims=True)
    p = jnp.exp(s - m)
    l = jnp.sum(p, axis=-1, keepdims=True)
    o = jnp.dot(p.astype(BF16), v_ref[...], preferred_element_type=F32)
    o_ref[...] = (o / l).astype(o_ref.dtype)


def na_ctx_attention(qkv, gq, gk, *, batch, seq, ctx_len):
    ctx_blk0 = batch * seq // ctx_len
    return pl.pallas_call(
        functools.partial(_ctx_attn_body, scale=DH_C ** -0.5),
        out_shape=jax.ShapeDtypeStruct((batch * ctx_len, H_C * DH_C), BF16),
        grid=(batch, H_C),
        in_specs=[
            pl.BlockSpec((ctx_len, DH_C), lambda b, h: (ctx_blk0 + b, h)),
            pl.BlockSpec((ctx_len, DH_C), lambda b, h: (ctx_blk0 + b, H_C + h)),
            pl.BlockSpec((ctx_len, DH_C), lambda b, h: (ctx_blk0 + b, 2 * H_C + h)),
            pl.BlockSpec((1, DH_C), lambda b, h: (0, 0)),
            pl.BlockSpec((1, DH_C), lambda b, h: (0, 0)),
        ],
        out_specs=pl.BlockSpec((ctx_len, DH_C), lambda b, h: (b, h)),
        compiler_params=_cparams(("parallel", "parallel")),
        name="na_ctx_attention",
    )(qkv, qkv, qkv, gq, gk)


MOE_EXPERTS_PER_STEP = 2


def _moe_gu_body(h_ref, comb_ref, wg_ref, wu_ref, y_ref, wg_bf, wu_bf):
    j = pl.program_id(0)

    @pl.when(pl.program_id(1) == 0)
    def _():
        wg_bf[...] = wg_ref[...].astype(BF16)
        wu_bf[...] = wu_ref[...].astype(BF16)

    h = h_ref[...]
    comb = comb_ref[...]
    lane = lax.broadcasted_iota(jnp.int32, comb.shape, 1)
    for e in range(MOE_EXPERTS_PER_STEP):
        a = jnp.dot(h, wg_bf[e], preferred_element_type=F32)
        u = jnp.dot(h, wu_bf[e], preferred_element_type=F32)
        cw = jnp.sum(jnp.where(lane == j * MOE_EXPERTS_PER_STEP + e, comb, 0.0), axis=1, keepdims=True)
        y = a * jax.nn.sigmoid(a) * u * cw
        y_ref[:, e * D_FF_EXPERT:(e + 1) * D_FF_EXPERT] = y.astype(y_ref.dtype)


def moe_gate_up(h, comb, w_gate, w_up, tm=512):
    n, d = h.shape
    eps_ = MOE_EXPERTS_PER_STEP
    f = D_FF_EXPERT
    return pl.pallas_call(
        _moe_gu_body,
        out_shape=jax.ShapeDtypeStruct((n, N_EXPERTS * f), BF16),
        grid=(N_EXPERTS // eps_, n // tm),
        in_specs=[
            pl.BlockSpec((tm, d), lambda j, i: (i, 0)),
            pl.BlockSpec((tm, 128), lambda j, i: (i, 0)),
            pl.BlockSpec((eps_, d, f), lambda j, i: (j, 0, 0)),
            pl.BlockSpec((eps_, d, f), lambda j, i: (j, 0, 0)),
        ],
        out_specs=pl.BlockSpec((tm, eps_ * f), lambda j, i: (i, j)),
        scratch_shapes=[pltpu.VMEM((eps_, d, f), BF16), pltpu.VMEM((eps_, d, f), BF16)],
        compiler_params=_cparams(("arbitrary", "arbitrary")),
        name="moe_gate_up",
    )(h, comb, w_gate, w_up)


def _swap16(n=D_ROPE):
    i = np.arange(n)
    return np.where((i // 16) % 2 == 0, i + 16, i - 16)


def _ext_w_in(w):
    c_kr = Q_LORA + KV_LORA
    c_qb = c_kr + D_ROPE
    c_g = c_qb + 2 * H_B * DQK_B + 2 * W_B
    kr = w[:, c_kr:c_qb]
    used = (c_g - c_qb) + c_kr + 2 * D_ROPE
    gates = jnp.pad(w[:, c_g:], ((0, 0), (0, P_WIDTH - used - 4 * H_B)))
    return jnp.concatenate([w[:, c_qb:c_g], w[:, :c_kr], kr, kr[:, _swap16()], gates], axis=1)


def _ext_w_q_up(w):
    w3 = w.reshape(Q_LORA, H_A, D_NOPE + D_ROPE)
    return jnp.concatenate([w3, w3[:, :, D_NOPE + _swap16()]], axis=2).reshape(Q_LORA, H_A * 256)


def _ext_gain(g):
    return jnp.concatenate([g, g[D_NOPE + _swap16()]]).reshape(1, 256)


def _rope_tables(seq):
    t = jnp.arange(seq)
    half = D_ROPE // 4
    inv = ROPE_THETA ** (-jnp.arange(half, dtype=F32) / half)
    ar = (t // GRID_W).astype(F32)[:, None] * inv[None, :]
    ac = (t % GRID_W).astype(F32)[:, None] * inv[None, :]
    cr, sr, cc, sc = jnp.cos(ar), jnp.sin(ar), jnp.cos(ac), jnp.sin(ac)
    return jnp.concatenate([cr, cr, cc, cc, -sr, sr, -sc, sc], axis=1)


def kernel(x, c, ctx, c_ctx, ada_w, ada_b, norm1_g, norm2_g, ab_w_in, mla_q_norm_g, mla_kv_norm_g, mla_w_q_up, mla_w_kv_up, mla_qn_g, mla_kn_g, mlstm_conv_w, mlstm_gate_b, mlstm_out_norm_g, ab_w_out, na_w_qkv, na_qn_g, na_kn_g, na_rpb, na_w_out, moe_w_rg, moe_b_rg, moe_w_re, moe_b_re, moe_w_gate, moe_w_up, moe_w_down):
    batch, seq, d = x.shape
    ctx_len = ctx.shape[1]
    depth = ada_w.shape[0]
    n_lat = batch * seq
    dims = (n_lat, seq, batch)
    assert d == D_MODEL and batch + 1 <= MOD_ROWS and seq % 512 == 0 and ctx_len == ROW_BLOCK

    xs = jnp.concatenate([x.reshape(n_lat, d), ctx.reshape(batch * ctx_len, d)], axis=0)
    cvec = jnp.zeros((MOD_ROWS, d), F32).at[:batch].set(c).at[batch].set(c_ctx)
    mods_all = ada_mods(cvec, ada_w, ada_b).reshape(depth, MOD_ROWS * 6, 1, d)
    trig = _rope_tables(seq)

    for layer in range(depth):
        j = layer // 2
        mods = mods_all[layer]
        h = norm_mod(xs, norm1_g[layer], mods, (0, 1), dims)
        if layer % 2 == 0:
            p_all = matmul(h, _ext_w_in(ab_w_in[j]), out_dtype=F32)
            q_all = matmul(p_all, _ext_w_q_up(mla_w_q_up[j]), out_dtype=BF16, a_col=P_CQ // Q_LORA, k=Q_LORA,
                           rms_gain=mla_q_norm_g[j])
            kv_all = matmul(p_all, mla_w_kv_up[j], out_dtype=BF16, a_col=P_CKV // KV_LORA, k=KV_LORA,
                            rms_gain=mla_kv_norm_g[j])
            gq, gk = _ext_gain(mla_qn_g[j]), _ext_gain(mla_kn_g[j])
            kw = dict(batch=batch, seq=seq, ctx_len=ctx_len)
            a_l = mla_attention(q_all, kv_all, p_all, trig, gq, gk, latent_queries=True, **kw)
            a_c = mla_attention(q_all, kv_all, p_all, trig, gq, gk, latent_queries=False, **kw)
            qk = conv_silu(p_all, mlstm_conv_w[j], n_lat=n_lat, seq=seq, ctx_len=ctx_len)
            gates_t = p_all[:, P_GATE:P_GATE + 4 * H_B].T
            hdir = mlstm_scan(qk, p_all, gates_t, mlstm_gate_b[j], **kw)
            mix_b = mlstm_finish(hdir, p_all, mlstm_out_norm_g[j])
            mixed = jnp.concatenate([jnp.concatenate([a_l, a_c], axis=0), mix_b], axis=1)
            w_out = ab_w_out[j]
        else:
            qkv = matmul(h, na_w_qkv[j], out_dtype=BF16)
            gq, gk = na_qn_g[j].reshape(1, DH_C), na_kn_g[j].reshape(1, DH_C)
            kw = dict(batch=batch, seq=seq, ctx_len=ctx_len)
            o_l = na_attention(qkv, na_bias_tables(na_rpb[j], seq // GRID_W), gq, gk, **kw)
            o_c = na_ctx_attention(qkv, gq, gk, **kw)
            mixed = jnp.concatenate([o_l, o_c], axis=0)
            w_out = na_w_out[j]
        xs = matmul(mixed, w_out, out_dtype=F32, resid=xs, mods=mods, k_gate=2, dims=dims)

        w_r = jnp.pad(jnp.concatenate([moe_w_re[layer], moe_w_rg[layer]], axis=1),
                      ((0, 0), (0, 128 - N_EXPERTS - N_GROUPS)))
        b_r = jnp.pad(jnp.concatenate([moe_b_re[layer], moe_b_rg[layer]]), (0, 128 - N_EXPERTS - N_GROUPS))
        h2, comb = norm_router(xs, norm2_g[layer], mods, (3, 4), dims, w_r, b_r.reshape(1, 128))
        y = moe_gate_up(h2, comb, moe_w_gate[layer], moe_w_up[layer])
        xs = matmul(y, moe_w_down[layer].reshape(N_EXPERTS * D_FF_EXPERT, d), out_dtype=F32,
                    resid=xs, mods=mods, k_gate=5, dims=dims, tn=256)
    return xs[:n_lat].reshape(batch, seq, d)
```

```python
import functools

import numpy as np
import jax
import jax.numpy as jnp
from jax import lax
from jax.experimental import pallas as pl
from jax.experimental.pallas import tpu as pltpu

F32 = jnp.float32
BF16 = jnp.bfloat16

D_MODEL = 4096
GRID_W = 64
H_A = 16
Q_LORA = 1024
KV_LORA = 512
D_NOPE = 128
D_ROPE = 64
D_V_A = 128
H_B = 4
DQK_B = 256
DV_B = 512
MLSTM_CHUNK = 128
M_INIT = -1e30
H_C = 32
DH_C = 128
NA_KH = 8
NA_KW = 16
N_GROUPS = 4
EXPERTS_PER_GROUP = 8
N_EXPERTS = 32
D_FF_EXPERT = 192
ROPE_THETA = 10000.0
NORM_EPS = 1e-6
W_A = H_A * D_V_A
W_B = H_B * DV_B
NEG = -1e30

VMEM_LIMIT_BYTES = 56 * 1024 * 1024
ROW_BLOCK = 256
MOD_ROWS = 8

P_QK = 0
P_V = 2048
P_O = 4096
P_CQ = 6144
P_CKV = 7168
P_KR = 7680
P_GATE = 7808
P_WIDTH = 8192


def _cparams(sem):
    return pltpu.CompilerParams(dimension_semantics=sem, vmem_limit_bytes=VMEM_LIMIT_BYTES)


def _mod_row(i, rows_per_block, n_lat, seq, batch):
    return jnp.where(i < n_lat // rows_per_block, i // (seq // rows_per_block), batch)


ADA_ROW_CHUNK = 512


def _ada_body(c_ref, w_ref, b_ref, o_ref, *, n_vec):
    d, tn = w_ref.shape[1], w_ref.shape[2]
    reps = tn // 128

    def chunk(ci, accs):
        rows = pl.ds(pl.multiple_of(ci * ADA_ROW_CHUNK, ADA_ROW_CHUNK), ADA_ROW_CHUNK)
        w = w_ref[0, rows, :]
        out = []
        for m in range(n_vec):
            c = c_ref[m, rows, :]
            cm = pltpu.repeat(c * jax.nn.sigmoid(c), reps, axis=1)
            part = (w * cm).reshape(ADA_ROW_CHUNK // 8, 8, tn).sum(axis=0)
            out.append(accs[m] + part)
        return tuple(out)

    accs = lax.fori_loop(0, d // ADA_ROW_CHUNK, chunk, tuple(jnp.zeros((8, tn), F32) for _ in range(n_vec)))
    rows = [jnp.sum(a, axis=0, keepdims=True) for a in accs]
    rows += [jnp.zeros((1, tn), F32)] * (MOD_ROWS - n_vec)
    o_ref[0] = jnp.concatenate(rows, axis=0) + b_ref[0]


def ada_mods(cvec, ada_w, ada_b, tn=512):
    depth, d, n6 = ada_w.shape
    n_vec = cvec.shape[0]
    c_lanes = jnp.broadcast_to(cvec[:, :, None], (n_vec, d, 128))
    return pl.pallas_call(
        functools.partial(_ada_body, n_vec=n_vec),
        out_shape=jax.ShapeDtypeStruct((depth, MOD_ROWS, n6), F32),
        grid=(depth, n6 // tn),
        in_specs=[
            pl.BlockSpec((n_vec, d, 128), lambda l, j: (0, 0, 0)),
            pl.BlockSpec((1, d, tn), lambda l, j: (l, 0, j)),
            pl.BlockSpec((1, 1, tn), lambda l, j: (l, 0, j)),
        ],
        out_specs=pl.BlockSpec((1, MOD_ROWS, tn), lambda l, j: (l, 0, j)),
        compiler_params=_cparams(("arbitrary", "arbitrary")),
        name="ada_mods",
    )(c_lanes, ada_w, ada_b.reshape(depth, 1, n6))


def _modulated_norm(x_ref, g_ref, sh_ref, sc_ref):
    x = x_ref[...]
    ms = jnp.mean(x * x, axis=-1, keepdims=True)
    y = x * lax.rsqrt(ms + NORM_EPS) * g_ref[...]
    return y * (1.0 + sc_ref[0]) + sh_ref[0]


def _norm_body(x_ref, g_ref, sh_ref, sc_ref, o_ref):
    o_ref[...] = _modulated_norm(x_ref, g_ref, sh_ref, sc_ref).astype(o_ref.dtype)


def _norm_router_body(x_ref, g_ref, sh_ref, sc_ref, wr_ref, br_ref, h_ref, comb_ref):
    h = _modulated_norm(x_ref, g_ref, sh_ref, sc_ref)
    h_ref[...] = h.astype(h_ref.dtype)
    logits = jnp.dot(h, wr_ref[...], precision=lax.Precision.HIGHEST,
                     preferred_element_type=F32) + br_ref[...]
    lane = lax.broadcasted_iota(jnp.int32, logits.shape, 1)
    big = jnp.int32(1 << 20)
    is_g = (lane >= N_EXPERTS) & (lane < N_EXPERTS + N_GROUPS)
    gl = jnp.where(is_g, logits, NEG)
    gmax = jnp.max(gl, axis=-1, keepdims=True)
    g_idx = jnp.min(jnp.where(gl == gmax, lane, big), axis=-1, keepdims=True) - N_EXPERTS
    p_g = 1.0 / jnp.sum(jnp.where(is_g, jnp.exp(gl - gmax), 0.0), axis=-1, keepdims=True)
    lo = g_idx * EXPERTS_PER_GROUP
    in_grp = (lane >= lo) & (lane < lo + EXPERTS_PER_GROUP)
    el = jnp.where(in_grp, logits, NEG)
    e1 = jnp.max(el, axis=-1, keepdims=True)
    i1 = jnp.min(jnp.where(el == e1, lane, big), axis=-1, keepdims=True)
    el2 = jnp.where(lane == i1, NEG, el)
    e2 = jnp.max(el2, axis=-1, keepdims=True)
    i2 = jnp.min(jnp.where(el2 == e2, lane, big), axis=-1, keepdims=True)
    t = jnp.exp(e2 - e1)
    w1 = p_g / (1.0 + t)
    w2 = w1 * t
    comb_ref[...] = jnp.where(lane == i1, w1, 0.0) + jnp.where(lane == i2, w2, 0.0)


def _norm_specs(d, mods_k, dims):
    n_lat, seq, batch = dims
    mrow = functools.partial(_mod_row, rows_per_block=ROW_BLOCK, n_lat=n_lat, seq=seq, batch=batch)
    k_shift, k_scale = mods_k
    return [
        pl.BlockSpec((ROW_BLOCK, d), lambda i: (i, 0)),
        pl.BlockSpec((1, d), lambda i: (0, 0)),
        pl.BlockSpec((1, 1, d), lambda i: (mrow(i) * 6 + k_shift, 0, 0)),
        pl.BlockSpec((1, 1, d), lambda i: (mrow(i) * 6 + k_scale, 0, 0)),
    ]


def norm_mod(x, g, mods, mods_k, dims):
    n, d = x.shape
    return pl.pallas_call(
        _norm_body,
        out_shape=jax.ShapeDtypeStruct((n, d), BF16),
        grid=(n // ROW_BLOCK,),
        in_specs=_norm_specs(d, mods_k, dims),
        out_specs=pl.BlockSpec((ROW_BLOCK, d), lambda i: (i, 0)),
        compiler_params=_cparams(("parallel",)),
        name="norm_mod",
    )(x, g.reshape(1, d), mods, mods)


def norm_router(x, g, mods, mods_k, dims, w_r, b_r):
    n, d = x.shape
    return pl.pallas_call(
        _norm_router_body,
        out_shape=(jax.ShapeDtypeStruct((n, d), BF16), jax.ShapeDtypeStruct((n, 128), F32)),
        grid=(n // ROW_BLOCK,),
        in_specs=_norm_specs(d, mods_k, dims) + [
            pl.BlockSpec((d, 128), lambda i: (0, 0)),
            pl.BlockSpec((1, 128), lambda i: (0, 0)),
        ],
        out_specs=(pl.BlockSpec((ROW_BLOCK, d), lambda i: (i, 0)),
                   pl.BlockSpec((ROW_BLOCK, 128), lambda i: (i, 0))),
        compiler_params=_cparams(("parallel",)),
        name="norm_router",
    )(x, g.reshape(1, d), mods, mods, w_r, b_r)


def _mm_body(*refs, rms, gated, k1, two_a):
    it = iter(refs)
    a_ref = next(it)
    a2_ref = next(it) if two_a else None
    g_ref = next(it) if rms else None
    w_ref = next(it)
    res_ref = next(it) if gated else None
    gate_ref = next(it) if gated else None
    o_ref = next(it)
    wbf_ref = next(it)

    @pl.when(pl.program_id(1) == 0)
    def _():
        wbf_ref[...] = w_ref[...].astype(BF16)

    a = a_ref[...]
    if rms:
        af = a.astype(F32)
        ms = jnp.mean(af * af, axis=-1, keepdims=True)
        a = (af * lax.rsqrt(ms + NORM_EPS) * g_ref[...]).astype(BF16)
    if two_a:
        acc = (jnp.dot(a, wbf_ref[:k1, :], preferred_element_type=F32)
               + jnp.dot(a2_ref[...], wbf_ref[k1:, :], preferred_element_type=F32))
    else:
        acc = jnp.dot(a, wbf_ref[...], preferred_element_type=F32)
    if gated:
        acc = res_ref[...] + gate_ref[0] * acc
    o_ref[...] = acc.astype(o_ref.dtype)


def matmul(a, w, *, out_dtype, layer=None, a_col=0, k=None, a2=None, rms_gain=None, resid=None, mods=None,
           k_gate=None, dims=None, tm=512, tn=512):
    m = a.shape[0]
    kw_, nw = w.shape[-2], w.shape[-1]
    two_a = a2 is not None
    k = (a.shape[1] if two_a else kw_) if k is None else k
    assert kw_ == k + (a2.shape[1] if two_a else 0) and m % tm == 0 and nw % tn == 0
    rms = rms_gain is not None
    gated = resid is not None
    in_specs = [pl.BlockSpec((tm, k), lambda j, i: (i, a_col))]
    args = [a]
    if two_a:
        in_specs.append(pl.BlockSpec((tm, a2.shape[1]), lambda j, i: (i, 0)))
        args.append(a2)
    if rms:
        in_specs.append(pl.BlockSpec((1, k), lambda j, i: (0, 0)))
        args.append(rms_gain.reshape(1, k))
    if layer is None:
        in_specs.append(pl.BlockSpec((kw_, tn), lambda j, i: (0, j)))
    else:
        in_specs.append(pl.BlockSpec((None, kw_, tn), lambda j, i: (layer, 0, j)))
    args.append(w)
    if gated:
        n_lat, seq, batch = dims
        mrow = functools.partial(_mod_row, rows_per_block=tm, n_lat=n_lat, seq=seq, batch=batch)
        in_specs.append(pl.BlockSpec((tm, tn), lambda j, i: (i, j)))
        in_specs.append(pl.BlockSpec((1, 1, tn), lambda j, i: (mrow(i) * 6 + k_gate, 0, j)))
        args += [resid, mods]
    return pl.pallas_call(
        functools.partial(_mm_body, rms=rms, gated=gated, k1=k, two_a=two_a),
        out_shape=jax.ShapeDtypeStruct((m, nw), out_dtype),
        grid=(nw // tn, m // tm),
        in_specs=in_specs,
        out_specs=pl.BlockSpec((tm, tn), lambda j, i: (i, j)),
        scratch_shapes=[pltpu.VMEM((kw_, tn), BF16)],
        compiler_params=_cparams(("arbitrary", "arbitrary")),
        name="matmul",
    )(*args)


def _rope_rot(pr, trig):
    if trig is not None:
        pr = pr * trig
    else:
        lane = lax.broadcasted_iota(jnp.int32, pr.shape, 1)
        pr = jnp.where(lane < D_ROPE, pr, 0.0)
    return pr + pltpu.roll(pr, D_ROPE, axis=1)


def _mla_body(*refs, n_lat_keys, n_ctx_keys, rope_q, scale, key_chunk, aliased):
    it = iter(refs)
    q_ref = next(it)
    kvl_ref = next(it) if n_lat_keys else None
    krl_ref = next(it) if n_lat_keys else None
    tkl_ref = next(it) if n_lat_keys else None
    kvc_ref = next(it)
    krc_ref = next(it)
    tq_ref = next(it) if rope_q else None
    gq_ref = next(it)
    gk_ref = next(it)
    if aliased:
        next(it)
    o_ref = next(it)
    kbuf = next(it)
    vbuf = next(it)
    dqk = D_NOPE + D_ROPE

    def prep_keys(kv_ref, kr_ref, trig_ref, off, n):
        gk = gk_ref[...]

        def chunk(c, carry):
            rows = pl.ds(pl.multiple_of(c * key_chunk, key_chunk), key_chunk)
            kv = kv_ref[rows, :]
            kn = kv[:, :D_NOPE].astype(F32)
            kr = kr_ref[rows, :].astype(F32)
            lane = lax.broadcasted_iota(jnp.int32, kr.shape, 1)
            ss = (jnp.sum(kn * kn, axis=-1, keepdims=True)
                  + jnp.sum(jnp.where(lane < D_ROPE, kr * kr, 0.0), axis=-1, keepdims=True))
            r = lax.rsqrt(ss / dqk + NORM_EPS)
            trig = trig_ref[rows, :] if trig_ref is not None else None
            rot = _rope_rot(kr * (r * gk[:, D_NOPE:]), trig)
            orow = pl.ds(pl.multiple_of(off + c * key_chunk, key_chunk), key_chunk)
            kbuf[orow, :D_NOPE] = (kn * (r * gk[:, :D_NOPE])).astype(BF16)
            kbuf[orow, D_NOPE:] = rot.astype(BF16)
            vbuf[orow, :] = kv[:, D_NOPE:].astype(BF16)
            return carry

        lax.fori_loop(0, n // key_chunk, chunk, 0)

    @pl.when(pl.program_id(2) == 0)
    def _():
        if n_lat_keys:
            prep_keys(kvl_ref, krl_ref, tkl_ref, 0, n_lat_keys)
        prep_keys(kvc_ref, krc_ref, None, n_lat_keys, n_ctx_keys)

    q = q_ref[...].astype(F32)
    qn = q[:, :D_NOPE]
    qr = q[:, D_NOPE:]
    lane = lax.broadcasted_iota(jnp.int32, qr.shape, 1)
    ss = (jnp.sum(qn * qn, axis=-1, keepdims=True)
          + jnp.sum(jnp.where(lane < D_ROPE, qr * qr, 0.0), axis=-1, keepdims=True))
    r = lax.rsqrt(ss / dqk + NORM_EPS) * scale
    gq = gq_ref[...]
    rot = _rope_rot(qr * (r * gq[:, D_NOPE:]), tq_ref[...] if rope_q else None)
    rot = jnp.where(lane < D_ROPE, rot, 0.0)
    q2 = jnp.concatenate([qn * (r * gq[:, :D_NOPE]), rot], axis=1).astype(BF16)
    s = lax.dot_general(q2, kbuf[...], (((1,), (1,)), ((), ())), preferred_element_type=F32)
    m = jnp.max(s, axis=-1, keepdims=True)
    p = jnp.exp(s - m)
    l = jnp.sum(p, axis=-1, keepdims=True)
    o = jnp.dot(p.astype(BF16), vbuf[...], preferred_element_type=F32)
    o_ref[...] = (o / l).astype(o_ref.dtype)


def mla_attention(q_all, kv_all, p_all, trig, gq, gk, *, batch, seq, ctx_len, latent_queries, out_buf=None,
                  tq=256):
    n_lat = batch * seq
    kr_blk = P_KR // 128
    scale = (D_NOPE + D_ROPE) ** -0.5
    if latent_queries:
        nq, n_lat_keys, q_row0 = seq, seq, 0
    else:
        nq, n_lat_keys, q_row0 = ctx_len, 0, n_lat // tq
    nqb = nq // tq
    ctx_blk0 = n_lat // ctx_len
    in_specs = [pl.BlockSpec((tq, 256), lambda b, h, i: (q_row0 + b * nqb + i, h))]
    args = [q_all]
    if latent_queries:
        in_specs += [
            pl.BlockSpec((seq, 256), lambda b, h, i: (b, h)),
            pl.BlockSpec((seq, 128), lambda b, h, i: (b, kr_blk)),
            pl.BlockSpec((seq, 128), lambda b, h, i: (0, 0)),
        ]
        args += [kv_all, p_all, trig]
    in_specs += [
        pl.BlockSpec((ctx_len, 256), lambda b, h, i: (ctx_blk0 + b, h)),
        pl.BlockSpec((ctx_len, 128), lambda b, h, i: (ctx_blk0 + b, kr_blk)),
    ]
    args += [kv_all, p_all]
    if latent_queries:
        in_specs.append(pl.BlockSpec((tq, 128), lambda b, h, i: (i, 0)))
        args.append(trig)
    in_specs += [pl.BlockSpec((1, 256), lambda b, h, i: (0, 0))] * 2
    args += [gq, gk]
    aliases = {}
    if out_buf is not None:
        in_specs.append(pl.BlockSpec(memory_space=pl.ANY))
        aliases = {len(args): 0}
        args.append(out_buf)
    nk = n_lat_keys + ctx_len
    body = functools.partial(_mla_body, n_lat_keys=n_lat_keys, n_ctx_keys=ctx_len, rope_q=latent_queries,
                             scale=scale, key_chunk=min(256, ctx_len), aliased=out_buf is not None)
    return pl.pallas_call(
        body,
        out_shape=jax.ShapeDtypeStruct((q_all.shape[0], W_A), BF16),
        grid=(batch, H_A, nqb),
        in_specs=in_specs,
        out_specs=pl.BlockSpec((tq, D_V_A), lambda b, h, i: (q_row0 + b * nqb + i, h)),
        scratch_shapes=[pltpu.VMEM((nk, 256), BF16), pltpu.VMEM((nk, D_V_A), BF16)],
        input_output_aliases=aliases,
        compiler_params=_cparams(("arbitrary", "arbitrary", "arbitrary")),
        name="mla_attention",
    )(*args)


def _conv_silu_body(x_ref, prev_ref, next_ref, w_ref, o_ref, *, lat_blocks, seq_blocks, ctx_blocks):
    i = pl.program_id(0)
    j = jnp.where(i < lat_blocks, i % seq_blocks, (i - lat_blocks) % ctx_blocks)
    nb = jnp.where(i < lat_blocks, seq_blocks, ctx_blocks)
    x = x_ref[...]
    rows = x.shape[0]
    row = lax.broadcasted_iota(jnp.int32, x.shape, 0)
    prev_row = jnp.where(j == 0, 0.0, prev_ref[7:8, :])
    next_row = jnp.where(j == nb - 1, 0.0, next_ref[0:1, :])
    xp = jnp.where(row == 0, prev_row, pltpu.roll(x, 1, axis=0))
    xn = jnp.where(row == rows - 1, next_row, pltpu.roll(x, rows - 1, axis=0))
    w = w_ref[...]
    y = xp * w[0:1, :] + x * w[1:2, :] + xn * w[2:3, :]
    y = y * jax.nn.sigmoid(y)
    half = y.shape[1] // 2
    o_ref[:, :half] = y[:, :half].astype(o_ref.dtype)
    o_ref[:, half:] = (y[:, half:] * (DQK_B ** -0.5)).astype(o_ref.dtype)


def conv_silu(p_all, conv_w, *, n_lat, seq, ctx_len):
    n = p_all.shape[0]
    width = 2 * H_B * DQK_B
    rb = ROW_BLOCK
    nblk = n // rb
    sub = rb // 8
    last8 = n // 8 - 1
    body = functools.partial(_conv_silu_body, lat_blocks=n_lat // rb, seq_blocks=seq // rb,
                             ctx_blocks=ctx_len // rb)
    return pl.pallas_call(
        body,
        out_shape=jax.ShapeDtypeStruct((n, width), BF16),
        grid=(nblk,),
        in_specs=[
            pl.BlockSpec((rb, width), lambda i: (i, P_QK // width)),
            pl.BlockSpec((8, width), lambda i: (jnp.maximum(i * sub - 1, 0), P_QK // width)),
            pl.BlockSpec((8, width), lambda i: (jnp.minimum((i + 1) * sub, last8), P_QK // width)),
            pl.BlockSpec((3, width), lambda i: (0, 0)),
        ],
        out_specs=pl.BlockSpec((rb, width), lambda i: (i, 0)),
        compiler_params=_cparams(("parallel",)),
        name="conv_silu",
    )(p_all, p_all, p_all, conv_w)


def _log_sigmoid(x):
    return jnp.minimum(x, 0.0) - jnp.log(1.0 + jnp.exp(-jnp.abs(x)))


def _mlstm_body(q_ref, k_ref, v_ref, gc_ref, gr_ref, bc_ref, br_ref, o_ref, c_st, n_st, m_st):
    h = pl.program_id(1)
    direction = pl.program_id(2)
    c = pl.program_id(3)
    L = MLSTM_CHUNK
    hi = lax.Precision.HIGHEST

    @pl.when(c == 0)
    def _():
        c_st[...] = jnp.zeros_like(c_st)
        n_st[...] = jnp.zeros_like(n_st)
        m_st[...] = jnp.full_like(m_st, M_INIT)

    gi = direction * (2 * H_B) + h
    gcol = gc_ref[...] + bc_ref[...]
    lane = lax.broadcasted_iota(jnp.int32, gcol.shape, 1)
    li_c = jnp.sum(jnp.where(lane == gi, gcol, 0.0), axis=1, keepdims=True)
    lf_c = _log_sigmoid(jnp.sum(jnp.where(lane == gi + H_B, gcol, 0.0), axis=1, keepdims=True))
    grow = gr_ref[...] + br_ref[...]
    sub = lax.broadcasted_iota(jnp.int32, grow.shape, 0)
    li_r = jnp.sum(jnp.where(sub == gi, grow, 0.0), axis=0, keepdims=True)
    lf_r = _log_sigmoid(jnp.sum(jnp.where(sub == gi + H_B, grow, 0.0), axis=0, keepdims=True))

    t_i = lax.broadcasted_iota(jnp.int32, (L, L), 0)
    s_i = lax.broadcasted_iota(jnp.int32, (L, L), 1)
    sgn = 1 - 2 * direction
    incl = (s_i - t_i) * sgn <= 0
    incl_f = incl.astype(F32)
    incl_t = ((t_i - s_i) * sgn <= 0).astype(F32)
    bcum_c = jnp.dot(incl_f, jnp.broadcast_to(lf_c, (L, L)), precision=hi, preferred_element_type=F32)
    bcum_r = jnp.dot(jnp.broadcast_to(lf_r, (8, L)), incl_t, precision=hi, preferred_element_type=F32)[0:1, :]
    b_last = jnp.sum(lf_r, axis=1, keepdims=True)

    m_prev = m_st[...]
    a_c = bcum_c[:, 0:1] + m_prev
    dlog = jnp.where(incl, bcum_c - bcum_r + li_r, -jnp.inf)
    m_t = jnp.maximum(a_c, jnp.max(dlog, axis=1, keepdims=True))
    w_intra = jnp.exp(dlog - m_t)
    w_inter = jnp.exp(a_c - m_t)

    q = q_ref[...]
    k = k_ref[...]
    v = v_ref[...].astype(BF16)
    c_prev = c_st[...]
    n_prev = n_st[...]
    s = lax.dot_general(q, k, (((1,), (1,)), ((), ())), preferred_element_type=F32) * w_intra
    num = (w_inter * jnp.dot(q, c_prev.astype(BF16), preferred_element_type=F32)
           + jnp.dot(s.astype(BF16), v, preferred_element_type=F32))
    den = (w_inter * jnp.sum(q.astype(F32) * n_prev, axis=1, keepdims=True)
           + jnp.sum(s, axis=1, keepdims=True))
    o_ref[0] = num / jnp.maximum(jnp.abs(den), jnp.exp(-m_t))

    g_r = b_last - bcum_r + li_r
    g_c = b_last - bcum_c[:, 0:1] + li_c
    m_new = jnp.maximum(b_last + m_prev, jnp.max(g_r, axis=1, keepdims=True))
    decay = jnp.exp(b_last + m_prev - m_new)
    kw = k.astype(F32) * jnp.exp(g_c - m_new)
    c_st[...] = decay * c_prev + lax.dot_general(kw.astype(BF16), v, (((0,), (0,)), ((), ())),
                                                 preferred_element_type=F32)
    n_st[...] = decay * n_prev + jnp.sum(kw, axis=0, keepdims=True)
    m_st[...] = m_new


def mlstm_scan(qk, p_all, gates_t, gate_b, *, batch, seq, ctx_len):
    n = qk.shape[0]
    L = MLSTM_CHUNK
    cc, lc = ctx_len // L, seq // L
    n_lat_blk = batch * lc

    def rowblk(b, d, c):
        ctx_blk = n_lat_blk + b * cc + jnp.where(d == 0, c, cc - 1 - c)
        lat_blk = b * lc + jnp.where(d == 0, c - cc, lc - 1 - (c - cc))
        return jnp.where(c < cc, ctx_blk, lat_blk)

    gb_col = jnp.zeros((1, 128), F32).at[0, :4 * H_B].set(gate_b)
    gb_row = gate_b.reshape(4 * H_B, 1)
    return pl.pallas_call(
        _mlstm_body,
        out_shape=jax.ShapeDtypeStruct((2, n, W_B), F32),
        grid=(batch, H_B, 2, cc + lc),
        in_specs=[
            pl.BlockSpec((L, DQK_B), lambda b, h, d, c: (rowblk(b, d, c), h)),
            pl.BlockSpec((L, DQK_B), lambda b, h, d, c: (rowblk(b, d, c), H_B + h)),
            pl.BlockSpec((L, DV_B), lambda b, h, d, c: (rowblk(b, d, c), P_V // DV_B + h)),
            pl.BlockSpec((L, 128), lambda b, h, d, c: (rowblk(b, d, c), P_GATE // 128)),
            pl.BlockSpec((4 * H_B, L), lambda b, h, d, c: (0, rowblk(b, d, c))),
            pl.BlockSpec((1, 128), lambda b, h, d, c: (0, 0)),
            pl.BlockSpec((4 * H_B, 1), lambda b, h, d, c: (0, 0)),
        ],
        out_specs=pl.BlockSpec((1, L, DV_B), lambda b, h, d, c: (d, rowblk(b, d, c), h)),
        scratch_shapes=[pltpu.VMEM((DQK_B, DV_B), F32), pltpu.VMEM((1, DQK_B), F32), pltpu.VMEM((1, 1), F32)],
        compiler_params=_cparams(("arbitrary",) * 4),
        name="mlstm_scan",
    )(qk, qk, p_all, p_all, gates_t, gb_col, gb_row)


def _mlstm_finish_body(h_ref, o_ref, g_ref, out_ref):
    hs = h_ref[0] + h_ref[1]
    g = g_ref[...]
    og = o_ref[...]
    for hd in range(H_B):
        sl = slice(hd * DV_B, (hd + 1) * DV_B)
        x = hs[:, sl]
        ms = jnp.mean(x * x, axis=-1, keepdims=True)
        hn = x * lax.rsqrt(ms + NORM_EPS) * g[:, sl]
        out_ref[:, sl] = (jax.nn.sigmoid(og[:, sl]) * hn).astype(out_ref.dtype)


def mlstm_finish(hdir, p_all, out_norm_g):
    n = hdir.shape[1]
    rb = ROW_BLOCK
    return pl.pallas_call(
        _mlstm_finish_body,
        out_shape=jax.ShapeDtypeStruct((n, W_B), BF16),
        grid=(n // rb,),
        in_specs=[
            pl.BlockSpec((2, rb, W_B), lambda i: (0, i, 0)),
            pl.BlockSpec((rb, W_B), lambda i: (i, P_O // W_B)),
            pl.BlockSpec((1, W_B), lambda i: (0, 0)),
        ],
        out_specs=pl.BlockSpec((rb, W_B), lambda i: (i, 0)),
        compiler_params=_cparams(("parallel",)),
        name="mlstm_finish",
    )(hdir, p_all, out_norm_g.reshape(1, W_B))


NA_ROWS_PER_STEP = 4


def _na_geometry(rows_n):
    r_t = NA_ROWS_PER_STEP
    kh = min(NA_KH, rows_n)
    win = min(r_t + kh - 1, rows_n)
    tiles = rows_n // r_t
    starts, sigs, cls = [], [], []
    for t in range(tiles):
        rq0 = t * r_t
        start = int(np.clip(rq0 - kh // 2, 0, rows_n - win))
        r0 = np.clip(np.arange(rq0, rq0 + r_t) - kh // 2, 0, rows_n - kh)
        sig = (start - rq0, tuple((r0 - rq0).tolist()))
        if sig not in sigs:
            sigs.append(sig)
        starts.append(start)
        cls.append(sigs.index(sig))
    return kh, win, np.array(starts, np.int32), np.array(cls, np.int32), sigs


def na_bias_tables(rpb, rows_n):
    r_t = NA_ROWS_PER_STEP
    kh, win, _, _, sigs = _na_geometry(rows_n)
    hi = lax.Precision.HIGHEST
    cols = np.arange(GRID_W)
    c0 = np.clip(cols - NA_KW // 2, 0, GRID_W - NA_KW)
    col_valid = (cols[None, :] >= c0[:, None]) & (cols[None, :] < c0[:, None] + NA_KW)
    col_off = cols[None, :] - cols[:, None] + (NA_KW - 1)
    c_sel = (col_valid[:, :, None] & (col_off[:, :, None] == np.arange(2 * NA_KW - 1))).astype(np.float32)
    r_sel = np.zeros((len(sigs), r_t, win, 2 * NA_KH - 1), np.float32)
    valid = np.zeros((len(sigs), r_t, GRID_W, win, GRID_W), bool)
    for ci, (dstart, dr0) in enumerate(sigs):
        a = np.arange(r_t)[:, None]
        j = np.arange(win)[None, :]
        krow = dstart + j
        r0 = np.array(dr0)[:, None]
        row_valid = (krow >= r0) & (krow < r0 + kh)
        row_off = krow - a + (NA_KH - 1)
        r_sel[ci] = row_valid[:, :, None] & (row_off[:, :, None] == np.arange(2 * NA_KH - 1))
        valid[ci] = row_valid[:, None, :, None] & col_valid[None, :, None, :]
    t = jnp.einsum('hrs,cds->hrcd', rpb, c_sel, precision=hi)
    b = jnp.einsum('hrcd,kajr->hkacjd', t, r_sel, precision=hi)
    b = b + np.where(valid, 0.0, NEG).astype(np.float32)[None]
    return b.reshape(H_C, len(sigs), r_t * GRID_W, win * GRID_W)


def _head_rmsnorm(x, g):
    xf = x.astype(F32)
    ms = jnp.mean(xf * xf, axis=-1, keepdims=True)
    return xf * lax.rsqrt(ms + NORM_EPS) * g


def _na_body(start_ref, cls_ref, q_ref, k_ref, v_ref, kc_ref, vc_ref, bias_ref, gq_ref, gk_ref, o_ref,
             kn, knc, *, seq, ctx_len, win_rows, scale):
    i = pl.program_id(2)
    chunk = min(512, seq)

    @pl.when(i == 0)
    def _():
        gk = gk_ref[...]

        def body(c, carry):
            rows = pl.ds(pl.multiple_of(c * chunk, chunk), chunk)
            kn[rows, :] = _head_rmsnorm(k_ref[rows, :], gk).astype(BF16)
            return carry

        lax.fori_loop(0, seq // chunk, body, 0)
        knc[...] = _head_rmsnorm(kc_ref[...], gk).astype(BF16)

    q = (_head_rmsnorm(q_ref[...], gq_ref[...]) * scale).astype(BF16)
    krow0 = pl.multiple_of(start_ref[i] * GRID_W, GRID_W)
    kw = kn[pl.ds(krow0, win_rows), :]
    vw = v_ref[pl.ds(krow0, win_rows), :]
    nt = (((1,), (1,)), ((), ()))
    s = lax.dot_general(q, kw, nt, preferred_element_type=F32) + bias_ref[0, 0]
    sc = lax.dot_general(q, knc[...], nt, preferred_element_type=F32)
    m = jnp.maximum(jnp.max(s, axis=-1, keepdims=True), jnp.max(sc, axis=-1, keepdims=True))
    p = jnp.exp(s - m)
    pc = jnp.exp(sc - m)
    l = jnp.sum(p, axis=-1, keepdims=True) + jnp.sum(pc, axis=-1, keepdims=True)
    o = (jnp.dot(p.astype(BF16), vw, preferred_element_type=F32)
         + jnp.dot(pc.astype(BF16), vc_ref[...], preferred_element_type=F32))
    o_ref[...] = (o / l).astype(o_ref.dtype)


def na_attention(qkv, bias_tabs, gq, gk, *, batch, seq, ctx_len, out_rows):
    rows_n = seq // GRID_W
    r_t = NA_ROWS_PER_STEP
    _, win, starts, cls, _ = _na_geometry(rows_n)
    tq = r_t * GRID_W
    tiles = rows_n // r_t
    n_lat = batch * seq
    ctx_blk0 = n_lat // ctx_len
    body = functools.partial(_na_body, seq=seq, ctx_len=ctx_len, win_rows=win * GRID_W, scale=DH_C ** -0.5)
    grid_spec = pltpu.PrefetchScalarGridSpec(
        num_scalar_prefetch=2,
        grid=(batch, H_C, tiles),
        in_specs=[
            pl.BlockSpec((tq, DH_C), lambda b, h, i, st, cl: (b * tiles + i, h)),
            pl.BlockSpec((seq, DH_C), lambda b, h, i, st, cl: (b, H_C + h)),
            pl.BlockSpec((seq, DH_C), lambda b, h, i, st, cl: (b, 2 * H_C + h)),
            pl.BlockSpec((ctx_len, DH_C), lambda b, h, i, st, cl: (ctx_blk0 + b, H_C + h)),
            pl.BlockSpec((ctx_len, DH_C), lambda b, h, i, st, cl: (ctx_blk0 + b, 2 * H_C + h)),
            pl.BlockSpec((1, 1, tq, win * GRID_W), lambda b, h, i, st, cl: (h, cl[i], 0, 0)),
            pl.BlockSpec((1, DH_C), lambda b, h, i, st, cl: (0, 0)),
            pl.BlockSpec((1, DH_C), lambda b, h, i, st, cl: (0, 0)),
        ],
        out_specs=pl.BlockSpec((tq, DH_C), lambda b, h, i, st, cl: (b * tiles + i, h)),
        scratch_shapes=[pltpu.VMEM((seq, DH_C), BF16), pltpu.VMEM((ctx_len, DH_C), BF16)],
    )
    return pl.pallas_call(
        body,
        out_shape=jax.ShapeDtypeStruct((out_rows, H_C * DH_C), BF16),
        grid_spec=grid_spec,
        compiler_params=_cparams(("arbitrary", "arbitrary", "arbitrary")),
        name="na_attention",
    )(jnp.asarray(starts), jnp.asarray(cls), qkv, qkv, qkv, qkv, qkv, bias_tabs, gq, gk)


def _ctx_attn_body(q_ref, k_ref, v_ref, gq_ref, gk_ref, buf_ref, o_ref, *, scale):
    del buf_ref
    q = (_head_rmsnorm(q_ref[...], gq_ref[...]) * scale).astype(BF16)
    k = _head_rmsnorm(k_ref[...], gk_ref[...]).astype(BF16)
    s = lax.dot_general(q, k, (((1,), (1,)), ((), ())), preferred_element_type=F32)
    m = jnp.max(s, axis=-1, keepdims=True)
    p = jnp.exp(s - m)
    l = jnp.sum(p, axis=-1, keepdims=True)
    o = jnp.dot(p.astype(BF16), v_ref[...], preferred_element_type=F32)
    o_ref[...] = (o / l).astype(o_ref.dtype)


def na_ctx_attention(qkv, gq, gk, out_buf, *, batch, seq, ctx_len):
    ctx_blk0 = batch * seq // ctx_len
    return pl.pallas_call(
        functools.partial(_ctx_attn_body, scale=DH_C ** -0.5),
        out_shape=jax.ShapeDtypeStruct(out_buf.shape, out_buf.dtype),
        grid=(batch, H_C),
        in_specs=[
            pl.BlockSpec((ctx_len, DH_C), lambda b, h: (ctx_blk0 + b, h)),
            pl.BlockSpec((ctx_len, DH_C), lambda b, h: (ctx_blk0 + b, H_C + h)),
            pl.BlockSpec((ctx_len, DH_C), lambda b, h: (ctx_blk0 + b, 2 * H_C + h)),
            pl.BlockSpec((1, DH_C), lambda b, h: (0, 0)),
            pl.BlockSpec((1, DH_C), lambda b, h: (0, 0)),
            pl.BlockSpec(memory_space=pl.ANY),
        ],
        out_specs=pl.BlockSpec((ctx_len, DH_C), lambda b, h: (ctx_blk0 + b, h)),
        input_output_aliases={5: 0},
        compiler_params=_cparams(("parallel", "parallel")),
        name="na_ctx_attention",
    )(qkv, qkv, qkv, gq, gk, out_buf)


MOE_EXPERTS_PER_STEP = 2


def _moe_gu_body(h_ref, comb_ref, wg_ref, wu_ref, y_ref, wg_bf, wu_bf):
    j = pl.program_id(0)

    @pl.when(pl.program_id(1) == 0)
    def _():
        wg_bf[...] = wg_ref[...].astype(BF16)
        wu_bf[...] = wu_ref[...].astype(BF16)

    h = h_ref[...]
    comb = comb_ref[...]
    lane = lax.broadcasted_iota(jnp.int32, comb.shape, 1)
    nt = (((1,), (1,)), ((), ()))
    for e in range(MOE_EXPERTS_PER_STEP):
        a = lax.dot_general(h, wg_bf[e], nt, preferred_element_type=F32)
        u = lax.dot_general(h, wu_bf[e], nt, preferred_element_type=F32)
        cw = jnp.sum(jnp.where(lane == j * MOE_EXPERTS_PER_STEP + e, comb, 0.0), axis=1, keepdims=True)
        y = a * jax.nn.sigmoid(a) * u * cw
        y_ref[:, e * D_FF_EXPERT:(e + 1) * D_FF_EXPERT] = y.astype(y_ref.dtype)


def moe_gate_up(h, comb, w_gate, w_up, layer, tm=512):
    n, d = h.shape
    eps_ = MOE_EXPERTS_PER_STEP
    f = D_FF_EXPERT
    return pl.pallas_call(
        _moe_gu_body,
        out_shape=jax.ShapeDtypeStruct((n, N_EXPERTS * f), BF16),
        grid=(N_EXPERTS // eps_, n // tm),
        in_specs=[
            pl.BlockSpec((tm, d), lambda j, i: (i, 0)),
            pl.BlockSpec((tm, 128), lambda j, i: (i, 0)),
            pl.BlockSpec((None, eps_, f, d), lambda j, i: (layer, j, 0, 0)),
            pl.BlockSpec((None, eps_, f, d), lambda j, i: (layer, j, 0, 0)),
        ],
        out_specs=pl.BlockSpec((tm, eps_ * f), lambda j, i: (i, j)),
        scratch_shapes=[pltpu.VMEM((eps_, f, d), BF16), pltpu.VMEM((eps_, f, d), BF16)],
        compiler_params=_cparams(("arbitrary", "arbitrary")),
        name="moe_gate_up",
    )(h, comb, w_gate, w_up)


def _swap16(a):
    lead = a.shape[:-1]
    return jnp.flip(a.reshape(*lead, D_ROPE // 32, 2, 16), axis=-2).reshape(*lead, D_ROPE)


def _ext_w_in(w):
    c_kr = Q_LORA + KV_LORA
    c_qb = c_kr + D_ROPE
    c_g = c_qb + 2 * H_B * DQK_B + 2 * W_B
    kr = w[:, c_kr:c_qb]
    used = (c_g - c_qb) + c_kr + 2 * D_ROPE
    gates = jnp.pad(w[:, c_g:], ((0, 0), (0, P_WIDTH - used - 4 * H_B)))
    return jnp.concatenate([w[:, c_qb:c_g], w[:, :c_kr], kr, _swap16(kr), gates], axis=1)


def _ext_w_q_up(w):
    w3 = w.reshape(Q_LORA, H_A, D_NOPE + D_ROPE)
    return jnp.concatenate([w3, _swap16(w3[:, :, D_NOPE:])], axis=2).reshape(Q_LORA, H_A * 256)


def _ext_gain(g):
    return jnp.concatenate([g, _swap16(g[D_NOPE:])]).reshape(1, 256)


def _rope_tables(seq):
    t = jnp.arange(seq)
    half = D_ROPE // 4
    inv = ROPE_THETA ** (-jnp.arange(half, dtype=F32) / half)
    ar = (t // GRID_W).astype(F32)[:, None] * inv[None, :]
    ac = (t % GRID_W).astype(F32)[:, None] * inv[None, :]
    cr, sr, cc, sc = jnp.cos(ar), jnp.sin(ar), jnp.cos(ac), jnp.sin(ac)
    return jnp.concatenate([cr, cr, cc, cc, -sr, sr, -sc, sc], axis=1)


def kernel(x, c, ctx, c_ctx, ada_w, ada_b, norm1_g, norm2_g, ab_w_in, mla_q_norm_g, mla_kv_norm_g, mla_w_q_up, mla_w_kv_up, mla_qn_g, mla_kn_g, mlstm_conv_w, mlstm_gate_b, mlstm_out_norm_g, ab_w_out, na_w_qkv, na_qn_g, na_kn_g, na_rpb, na_w_out, moe_w_rg, moe_b_rg, moe_w_re, moe_b_re, moe_w_gate, moe_w_up, moe_w_down):
    batch, seq, d = x.shape
    ctx_len = ctx.shape[1]
    depth = ada_w.shape[0]
    n_lat = batch * seq
    dims = (n_lat, seq, batch)
    assert d == D_MODEL and batch + 1 <= MOD_ROWS and seq % 512 == 0 and ctx_len == ROW_BLOCK

    xs = jnp.concatenate([x.reshape(n_lat, d), ctx.reshape(batch * ctx_len, d)], axis=0)
    cvec = jnp.concatenate([c, c_ctx[None, :]], axis=0)
    mods_all = ada_mods(cvec, ada_w, ada_b).reshape(depth, MOD_ROWS * 6, 1, d)
    trig = _rope_tables(seq)

    w_down_all = moe_w_down.reshape(depth, N_EXPERTS * D_FF_EXPERT, d)
    w_gate_t, w_up_t = jnp.swapaxes(moe_w_gate, 2, 3), jnp.swapaxes(moe_w_up, 2, 3)
    kw = dict(batch=batch, seq=seq, ctx_len=ctx_len)
    for layer in range(depth):
        j = layer // 2
        last = layer == depth - 1
        mods = mods_all[layer]
        h = norm_mod(xs, norm1_g[layer], mods, (0, 1), dims)
        if layer % 2 == 0:
            p_all = matmul(h, _ext_w_in(ab_w_in[j]), out_dtype=F32)
            q_all = matmul(p_all, _ext_w_q_up(mla_w_q_up[j]), out_dtype=BF16, a_col=P_CQ // Q_LORA, k=Q_LORA,
                           rms_gain=mla_q_norm_g[j])
            kv_all = matmul(p_all, mla_w_kv_up, layer=j, out_dtype=BF16, a_col=P_CKV // KV_LORA, k=KV_LORA,
                            rms_gain=mla_kv_norm_g[j])
            gq, gk = _ext_gain(mla_qn_g[j]), _ext_gain(mla_kn_g[j])
            mix_a = mla_attention(q_all, kv_all, p_all, trig, gq, gk, latent_queries=True, **kw)
            mix_a = mla_attention(q_all, kv_all, p_all, trig, gq, gk, latent_queries=False, out_buf=mix_a, **kw)
            qk = conv_silu(p_all, mlstm_conv_w[j], n_lat=n_lat, seq=seq, ctx_len=ctx_len)
            gates_t = p_all[:, P_GATE:P_GATE + 4 * H_B].T
            hdir = mlstm_scan(qk, p_all, gates_t, mlstm_gate_b[j], **kw)
            mix_b = mlstm_finish(hdir, p_all, mlstm_out_norm_g[j])
            xs = matmul(mix_a, ab_w_out, layer=j, a2=mix_b, out_dtype=F32, resid=xs, mods=mods, k_gate=2,
                        dims=dims)
        else:
            qkv = matmul(h, na_w_qkv, layer=j, out_dtype=BF16)
            gq, gk = na_qn_g[j].reshape(1, DH_C), na_kn_g[j].reshape(1, DH_C)
            mixed = na_attention(qkv, na_bias_tables(na_rpb[j], seq // GRID_W), gq, gk,
                                 out_rows=n_lat if last else xs.shape[0], **kw)
            if not last:
                mixed = na_ctx_attention(qkv, gq, gk, mixed, **kw)
            xs = matmul(mixed, na_w_out, layer=j, out_dtype=F32, resid=xs, mods=mods, k_gate=2, dims=dims)

        w_r = jnp.pad(jnp.concatenate([moe_w_re[layer], moe_w_rg[layer]], axis=1),
                      ((0, 0), (0, 128 - N_EXPERTS - N_GROUPS)))
        b_r = jnp.pad(jnp.concatenate([moe_b_re[layer], moe_b_rg[layer]]), (0, 128 - N_EXPERTS - N_GROUPS))
        h2, comb = norm_router(xs, norm2_g[layer], mods, (3, 4), dims, w_r, b_r.reshape(1, 128))
        y = moe_gate_up(h2, comb, w_gate_t, w_up_t, layer)
        xs = matmul(y, w_down_all, layer=layer, out_dtype=F32, resid=xs, mods=mods, k_gate=5, dims=dims, tn=256)
    return xs[:n_lat].reshape(batch, seq, d) if xs.shape[0] != n_lat else xs.reshape(batch, seq, d)
```

```python
import functools

import numpy as np
import jax
import jax.numpy as jnp
from jax import lax
from jax.experimental import pallas as pl
from jax.experimental.pallas import tpu as pltpu

F32 = jnp.float32
BF16 = jnp.bfloat16

D_MODEL = 4096
GRID_W = 64
H_A = 16
Q_LORA = 1024
KV_LORA = 512
D_NOPE = 128
D_ROPE = 64
D_V_A = 128
H_B = 4
DQK_B = 256
DV_B = 512
MLSTM_CHUNK = 128
M_INIT = -1e30
H_C = 32
DH_C = 128
NA_KH = 8
NA_KW = 16
N_GROUPS = 4
EXPERTS_PER_GROUP = 8
N_EXPERTS = 32
D_FF_EXPERT = 192
ROPE_THETA = 10000.0
NORM_EPS = 1e-6
W_A = H_A * D_V_A
W_B = H_B * DV_B
NEG = -1e30

VMEM_LIMIT_BYTES = 56 * 1024 * 1024
ROW_BLOCK = 256
MOD_ROWS = 8
SLAB_ROWS = 32
SLAB_STRIDE = 40
MOE_TILE = 256

P_QK = 0
P_V = 2048
P_O = 4096
P_CQ = 6144
P_CKV = 7168
P_KR = 7680
P_GATE = 7808
P_WIDTH = 8192


def _cparams(sem):
    return pltpu.CompilerParams(dimension_semantics=sem, vmem_limit_bytes=VMEM_LIMIT_BYTES)


def _mod_row(i, rows_per_block, n_lat, seq, batch):
    return jnp.where(i < n_lat // rows_per_block, i // (seq // rows_per_block), batch)


ADA_ROW_CHUNK = 512


def _ada_body(c_ref, w_ref, b_ref, o_ref, *, n_vec):
    d, tn = w_ref.shape[1], w_ref.shape[2]
    reps = tn // 128

    def chunk(ci, accs):
        rows = pl.ds(pl.multiple_of(ci * ADA_ROW_CHUNK, ADA_ROW_CHUNK), ADA_ROW_CHUNK)
        w = w_ref[0, rows, :]
        out = []
        for m in range(n_vec):
            c = c_ref[m, rows, :]
            cm = jnp.concatenate([c * jax.nn.sigmoid(c)] * reps, axis=1)
            part = (w * cm).reshape(ADA_ROW_CHUNK // 8, 8, tn).sum(axis=0)
            out.append(accs[m] + part)
        return tuple(out)

    accs = lax.fori_loop(0, d // ADA_ROW_CHUNK, chunk, tuple(jnp.zeros((8, tn), F32) for _ in range(n_vec)))
    rows = [jnp.sum(a, axis=0, keepdims=True) for a in accs]
    rows += [jnp.zeros((1, tn), F32)] * (MOD_ROWS - n_vec)
    o_ref[0] = jnp.concatenate(rows, axis=0) + b_ref[0]


def ada_mods(cvec, ada_w, ada_b, tn=512):
    depth, d, n6 = ada_w.shape
    n_vec = cvec.shape[0]
    c_lanes = jnp.broadcast_to(cvec[:, :, None], (n_vec, d, 128))
    return pl.pallas_call(
        functools.partial(_ada_body, n_vec=n_vec),
        out_shape=jax.ShapeDtypeStruct((depth, MOD_ROWS, n6), F32),
        grid=(depth, n6 // tn),
        in_specs=[
            pl.BlockSpec((n_vec, d, 128), lambda l, j: (0, 0, 0)),
            pl.BlockSpec((1, d, tn), lambda l, j: (l, 0, j)),
            pl.BlockSpec((1, 1, tn), lambda l, j: (l, 0, j)),
        ],
        out_specs=pl.BlockSpec((1, MOD_ROWS, tn), lambda l, j: (l, 0, j)),
        compiler_params=_cparams(("arbitrary", "arbitrary")),
        name="ada_mods",
    )(c_lanes, ada_w, ada_b.reshape(depth, 1, n6))


def _modulated_norm(x_ref, g_ref, sh_ref, sc_ref):
    x = x_ref[...]
    ms = jnp.mean(x * x, axis=-1, keepdims=True)
    y = x * lax.rsqrt(ms + NORM_EPS) * g_ref[...]
    return y * (1.0 + sc_ref[0]) + sh_ref[0]


def _norm_body(x_ref, g_ref, sh_ref, sc_ref, o_ref):
    o_ref[...] = _modulated_norm(x_ref, g_ref, sh_ref, sc_ref).astype(o_ref.dtype)


def _store_slabs(ref, mat, rows):
    for j in range(SLAB_ROWS):
        ref[pl.ds(j, rows, stride=SLAB_STRIDE), :] = mat[:, j * 128:(j + 1) * 128]
    for j in range(SLAB_ROWS, SLAB_STRIDE):
        ref[pl.ds(j, rows, stride=SLAB_STRIDE), :] = jnp.zeros((rows, 128), mat.dtype)


def _norm_router_body(x_ref, g_ref, sh_ref, sc_ref, wr_ref, br_ref, hs_ref, route_ref, cnt_ref, carry):
    @pl.when(pl.program_id(0) == 0)
    def _():
        carry[...] = jnp.zeros_like(carry)

    h = _modulated_norm(x_ref, g_ref, sh_ref, sc_ref)
    _store_slabs(hs_ref, h, ROW_BLOCK)
    logits = jnp.dot(h, wr_ref[...], precision=lax.Precision.HIGHEST,
                     preferred_element_type=F32) + br_ref[...]
    lane = lax.broadcasted_iota(jnp.int32, logits.shape, 1)
    big = jnp.int32(1 << 20)
    is_g = (lane >= N_EXPERTS) & (lane < N_EXPERTS + N_GROUPS)
    gl = jnp.where(is_g, logits, NEG)
    gmax = jnp.max(gl, axis=-1, keepdims=True)
    g_idx = jnp.min(jnp.where(gl == gmax, lane, big), axis=-1, keepdims=True) - N_EXPERTS
    p_g = 1.0 / jnp.sum(jnp.where(is_g, jnp.exp(gl - gmax), 0.0), axis=-1, keepdims=True)
    lo = g_idx * EXPERTS_PER_GROUP
    in_grp = (lane >= lo) & (lane < lo + EXPERTS_PER_GROUP)
    el = jnp.where(in_grp, logits, NEG)
    e1 = jnp.max(el, axis=-1, keepdims=True)
    i1 = jnp.min(jnp.where(el == e1, lane, big), axis=-1, keepdims=True)
    el2 = jnp.where(lane == i1, NEG, el)
    e2 = jnp.max(el2, axis=-1, keepdims=True)
    i2 = jnp.min(jnp.where(el2 == e2, lane, big), axis=-1, keepdims=True)
    t = jnp.exp(e2 - e1)
    w1 = p_g / (1.0 + t)
    w2 = w1 * t
    onehot = jnp.where((lane == i1) | (lane == i2), 1.0, 0.0)
    t_i = lax.broadcasted_iota(jnp.int32, (ROW_BLOCK, ROW_BLOCK), 0)
    s_i = lax.broadcasted_iota(jnp.int32, (ROW_BLOCK, ROW_BLOCK), 1)
    before = jnp.where(s_i < t_i, 1.0, 0.0).astype(BF16)
    seen = jnp.dot(before, onehot.astype(BF16), preferred_element_type=F32) + carry[...]
    rank1 = jnp.sum(jnp.where(lane == i1, seen, 0.0), axis=-1, keepdims=True)
    rank2 = jnp.sum(jnp.where(lane == i2, seen, 0.0), axis=-1, keepdims=True)
    total = carry[...] + jnp.sum(onehot, axis=0, keepdims=True)
    carry[...] = total
    cnt_ref[...] = jnp.broadcast_to(total, cnt_ref.shape)
    cols = (i1.astype(F32), i2.astype(F32), w1, w2, rank1, rank2)
    route = jnp.zeros(logits.shape, F32)
    for k, v in enumerate(cols):
        route = jnp.where(lane == k, v, route)
    route_ref[...] = route


def _norm_specs(d, mods_k, dims):
    n_lat, seq, batch = dims
    mrow = functools.partial(_mod_row, rows_per_block=ROW_BLOCK, n_lat=n_lat, seq=seq, batch=batch)
    k_shift, k_scale = mods_k
    return [
        pl.BlockSpec((ROW_BLOCK, d), lambda i: (i, 0)),
        pl.BlockSpec((1, d), lambda i: (0, 0)),
        pl.BlockSpec((1, 1, d), lambda i: (mrow(i) * 6 + k_shift, 0, 0)),
        pl.BlockSpec((1, 1, d), lambda i: (mrow(i) * 6 + k_scale, 0, 0)),
    ]


def norm_mod(x, g, mods, mods_k, dims):
    n, d = x.shape
    return pl.pallas_call(
        _norm_body,
        out_shape=jax.ShapeDtypeStruct((n, d), BF16),
        grid=(n // ROW_BLOCK,),
        in_specs=_norm_specs(d, mods_k, dims),
        out_specs=pl.BlockSpec((ROW_BLOCK, d), lambda i: (i, 0)),
        compiler_params=_cparams(("parallel",)),
        name="norm_mod",
    )(x, g.reshape(1, d), mods, mods)


def norm_router(x, g, mods, mods_k, dims, w_r, b_r):
    n, d = x.shape
    assert d == SLAB_ROWS * 128
    return pl.pallas_call(
        _norm_router_body,
        out_shape=(jax.ShapeDtypeStruct((n * SLAB_STRIDE, 128), F32), jax.ShapeDtypeStruct((n, 128), F32),
                   jax.ShapeDtypeStruct((8, 128), F32)),
        grid=(n // ROW_BLOCK,),
        in_specs=_norm_specs(d, mods_k, dims) + [
            pl.BlockSpec((d, 128), lambda i: (0, 0)),
            pl.BlockSpec((1, 128), lambda i: (0, 0)),
        ],
        out_specs=(pl.BlockSpec((ROW_BLOCK * SLAB_STRIDE, 128), lambda i: (i, 0)),
                   pl.BlockSpec((ROW_BLOCK, 128), lambda i: (i, 0)),
                   pl.BlockSpec((8, 128), lambda i: (0, 0))),
        scratch_shapes=[pltpu.VMEM((1, 128), F32)],
        compiler_params=_cparams(("arbitrary",)),
        name="norm_router",
    )(x, g.reshape(1, d), mods, mods, w_r, b_r)


def _mm_body(*refs, rms, gated, k1, two_a):
    it = iter(refs)
    a_ref = next(it)
    a2_ref = next(it) if two_a else None
    g_ref = next(it) if rms else None
    w_ref = next(it)
    res_ref = next(it) if gated else None
    gate_ref = next(it) if gated else None
    o_ref = next(it)
    wbf_ref = next(it)

    @pl.when(pl.program_id(1) == 0)
    def _():
        wbf_ref[...] = w_ref[...].astype(BF16)

    a = a_ref[...]
    if rms:
        af = a.astype(F32)
        ms = jnp.mean(af * af, axis=-1, keepdims=True)
        a = (af * lax.rsqrt(ms + NORM_EPS) * g_ref[...]).astype(BF16)
    if two_a:
        acc = (jnp.dot(a, wbf_ref[:k1, :], preferred_element_type=F32)
               + jnp.dot(a2_ref[...], wbf_ref[k1:, :], preferred_element_type=F32))
    else:
        acc = jnp.dot(a, wbf_ref[...], preferred_element_type=F32)
    if gated:
        acc = res_ref[...] + gate_ref[0] * acc
    o_ref[...] = acc.astype(o_ref.dtype)


def matmul(a, w, *, out_dtype, layer=None, a_col=0, k=None, a2=None, rms_gain=None, resid=None, mods=None,
           k_gate=None, dims=None, tm=512, tn=512):
    m = a.shape[0]
    kw_, nw = w.shape[-2], w.shape[-1]
    two_a = a2 is not None
    k = (a.shape[1] if two_a else kw_) if k is None else k
    assert kw_ == k + (a2.shape[1] if two_a else 0) and m % tm == 0 and nw % tn == 0
    rms = rms_gain is not None
    gated = resid is not None
    in_specs = [pl.BlockSpec((tm, k), lambda j, i: (i, a_col))]
    args = [a]
    if two_a:
        in_specs.append(pl.BlockSpec((tm, a2.shape[1]), lambda j, i: (i, 0)))
        args.append(a2)
    if rms:
        in_specs.append(pl.BlockSpec((1, k), lambda j, i: (0, 0)))
        args.append(rms_gain.reshape(1, k))
    if layer is None:
        in_specs.append(pl.BlockSpec((kw_, tn), lambda j, i: (0, j)))
    else:
        in_specs.append(pl.BlockSpec((None, kw_, tn), lambda j, i: (layer, 0, j)))
    args.append(w)
    if gated:
        n_lat, seq, batch = dims
        mrow = functools.partial(_mod_row, rows_per_block=tm, n_lat=n_lat, seq=seq, batch=batch)
        in_specs.append(pl.BlockSpec((tm, tn), lambda j, i: (i, j)))
        in_specs.append(pl.BlockSpec((1, 1, tn), lambda j, i: (mrow(i) * 6 + k_gate, 0, j)))
        args += [resid, mods]
    return pl.pallas_call(
        functools.partial(_mm_body, rms=rms, gated=gated, k1=k, two_a=two_a),
        out_shape=jax.ShapeDtypeStruct((m, nw), out_dtype),
        grid=(nw // tn, m // tm),
        in_specs=in_specs,
        out_specs=pl.BlockSpec((tm, tn), lambda j, i: (i, j)),
        scratch_shapes=[pltpu.VMEM((kw_, tn), BF16)],
        compiler_params=_cparams(("arbitrary", "arbitrary")),
        name="matmul",
    )(*args)


def _rope_rot(pr, trig):
    if trig is not None:
        pr = pr * trig
    else:
        lane = lax.broadcasted_iota(jnp.int32, pr.shape, 1)
        pr = jnp.where(lane < D_ROPE, pr, 0.0)
    return pr + pltpu.roll(pr, D_ROPE, axis=1)


def _mla_body(*refs, n_lat_keys, n_ctx_keys, rope_q, scale, key_chunk, aliased):
    it = iter(refs)
    q_ref = next(it)
    kvl_ref = next(it) if n_lat_keys else None
    krl_ref = next(it) if n_lat_keys else None
    tkl_ref = next(it) if n_lat_keys else None
    kvc_ref = next(it)
    krc_ref = next(it)
    tq_ref = next(it) if rope_q else None
    gq_ref = next(it)
    gk_ref = next(it)
    if aliased:
        next(it)
    o_ref = next(it)
    kbuf = next(it)
    vbuf = next(it)
    dqk = D_NOPE + D_ROPE

    def prep_keys(kv_ref, kr_ref, trig_ref, off, n):
        gk = gk_ref[...]

        def chunk(c, carry):
            rows = pl.ds(pl.multiple_of(c * key_chunk, key_chunk), key_chunk)
            kv = kv_ref[rows, :]
            kn = kv[:, :D_NOPE].astype(F32)
            kr = kr_ref[rows, :].astype(F32)
            lane = lax.broadcasted_iota(jnp.int32, kr.shape, 1)
            ss = (jnp.sum(kn * kn, axis=-1, keepdims=True)
                  + jnp.sum(jnp.where(lane < D_ROPE, kr * kr, 0.0), axis=-1, keepdims=True))
            r = lax.rsqrt(ss / dqk + NORM_EPS)
            trig = trig_ref[rows, :] if trig_ref is not None else None
            rot = _rope_rot(kr * (r * gk[:, D_NOPE:]), trig)
            orow = pl.ds(pl.multiple_of(off + c * key_chunk, key_chunk), key_chunk)
            kbuf[orow, :D_NOPE] = (kn * (r * gk[:, :D_NOPE])).astype(BF16)
            kbuf[orow, D_NOPE:] = rot.astype(BF16)
            vbuf[orow, :] = kv[:, D_NOPE:].astype(BF16)
            return carry

        lax.fori_loop(0, n // key_chunk, chunk, 0)

    @pl.when(pl.program_id(2) == 0)
    def _():
        if n_lat_keys:
            prep_keys(kvl_ref, krl_ref, tkl_ref, 0, n_lat_keys)
        prep_keys(kvc_ref, krc_ref, None, n_lat_keys, n_ctx_keys)

    q = q_ref[...].astype(F32)
    qn = q[:, :D_NOPE]
    qr = q[:, D_NOPE:]
    lane = lax.broadcasted_iota(jnp.int32, qr.shape, 1)
    ss = (jnp.sum(qn * qn, axis=-1, keepdims=True)
          + jnp.sum(jnp.where(lane < D_ROPE, qr * qr, 0.0), axis=-1, keepdims=True))
    r = lax.rsqrt(ss / dqk + NORM_EPS) * scale
    gq = gq_ref[...]
    rot = _rope_rot(qr * (r * gq[:, D_NOPE:]), tq_ref[...] if rope_q else None)
    rot = jnp.where(lane < D_ROPE, rot, 0.0)
    q2 = jnp.concatenate([qn * (r * gq[:, :D_NOPE]), rot], axis=1).astype(BF16)
    s = lax.dot_general(q2, kbuf[...], (((1,), (1,)), ((), ())), preferred_element_type=F32)
    m = jnp.max(s, axis=-1, keepdims=True)
    p = jnp.exp(s - m)
    l = jnp.sum(p, axis=-1, keepdims=True)
    o = jnp.dot(p.astype(BF16), vbuf[...], preferred_element_type=F32)
    o_ref[...] = (o / l).astype(o_ref.dtype)


def mla_attention(q_all, kv_all, p_all, trig, gq, gk, *, batch, seq, ctx_len, latent_queries, out_buf=None,
                  tq=256):
    n_lat = batch * seq
    kr_blk = P_KR // 128
    scale = (D_NOPE + D_ROPE) ** -0.5
    if latent_queries:
        nq, n_lat_keys, q_row0 = seq, seq, 0
    else:
        nq, n_lat_keys, q_row0 = ctx_len, 0, n_lat // tq
    nqb = nq // tq
    ctx_blk0 = n_lat // ctx_len
    in_specs = [pl.BlockSpec((tq, 256), lambda b, h, i: (q_row0 + b * nqb + i, h))]
    args = [q_all]
    if latent_queries:
        in_specs += [
            pl.BlockSpec((seq, 256), lambda b, h, i: (b, h)),
            pl.BlockSpec((seq, 128), lambda b, h, i: (b, kr_blk)),
            pl.BlockSpec((seq, 128), lambda b, h, i: (0, 0)),
        ]
        args += [kv_all, p_all, trig]
    in_specs += [
        pl.BlockSpec((ctx_len, 256), lambda b, h, i: (ctx_blk0 + b, h)),
        pl.BlockSpec((ctx_len, 128), lambda b, h, i: (ctx_blk0 + b, kr_blk)),
    ]
    args += [kv_all, p_all]
    if latent_queries:
        in_specs.append(pl.BlockSpec((tq, 128), lambda b, h, i: (i, 0)))
        args.append(trig)
    in_specs += [pl.BlockSpec((1, 256), lambda b, h, i: (0, 0))] * 2
    args += [gq, gk]
    aliases = {}
    if out_buf is not None:
        in_specs.append(pl.BlockSpec(memory_space=pl.ANY))
        aliases = {len(args): 0}
        args.append(out_buf)
    nk = n_lat_keys + ctx_len
    body = functools.partial(_mla_body, n_lat_keys=n_lat_keys, n_ctx_keys=ctx_len, rope_q=latent_queries,
                             scale=scale, key_chunk=min(256, ctx_len), aliased=out_buf is not None)
    return pl.pallas_call(
        body,
        out_shape=jax.ShapeDtypeStruct((q_all.shape[0], W_A), BF16),
        grid=(batch, H_A, nqb),
        in_specs=in_specs,
        out_specs=pl.BlockSpec((tq, D_V_A), lambda b, h, i: (q_row0 + b * nqb + i, h)),
        scratch_shapes=[pltpu.VMEM((nk, 256), BF16), pltpu.VMEM((nk, D_V_A), BF16)],
        input_output_aliases=aliases,
        compiler_params=_cparams(("arbitrary", "arbitrary", "arbitrary")),
        name="mla_attention",
    )(*args)


def _conv_silu_body(x_ref, prev_ref, next_ref, w_ref, o_ref, *, lat_blocks, seq_blocks, ctx_blocks):
    i = pl.program_id(0)
    j = jnp.where(i < lat_blocks, i % seq_blocks, (i - lat_blocks) % ctx_blocks)
    nb = jnp.where(i < lat_blocks, seq_blocks, ctx_blocks)
    x = x_ref[...]
    rows = x.shape[0]
    row = lax.broadcasted_iota(jnp.int32, x.shape, 0)
    prev_row = jnp.where(j == 0, 0.0, prev_ref[7:8, :])
    next_row = jnp.where(j == nb - 1, 0.0, next_ref[0:1, :])
    xp = jnp.where(row == 0, prev_row, pltpu.roll(x, 1, axis=0))
    xn = jnp.where(row == rows - 1, next_row, pltpu.roll(x, rows - 1, axis=0))
    w = w_ref[...]
    y = xp * w[0:1, :] + x * w[1:2, :] + xn * w[2:3, :]
    y = y * jax.nn.sigmoid(y)
    half = y.shape[1] // 2
    o_ref[:, :half] = y[:, :half].astype(o_ref.dtype)
    o_ref[:, half:] = (y[:, half:] * (DQK_B ** -0.5)).astype(o_ref.dtype)


def conv_silu(p_all, conv_w, *, n_lat, seq, ctx_len):
    n = p_all.shape[0]
    width = 2 * H_B * DQK_B
    rb = ROW_BLOCK
    nblk = n // rb
    sub = rb // 8
    last8 = n // 8 - 1
    body = functools.partial(_conv_silu_body, lat_blocks=n_lat // rb, seq_blocks=seq // rb,
                             ctx_blocks=ctx_len // rb)
    return pl.pallas_call(
        body,
        out_shape=jax.ShapeDtypeStruct((n, width), BF16),
        grid=(nblk,),
        in_specs=[
            pl.BlockSpec((rb, width), lambda i: (i, P_QK // width)),
            pl.BlockSpec((8, width), lambda i: (jnp.maximum(i * sub - 1, 0), P_QK // width)),
            pl.BlockSpec((8, width), lambda i: (jnp.minimum((i + 1) * sub, last8), P_QK // width)),
            pl.BlockSpec((3, width), lambda i: (0, 0)),
        ],
        out_specs=pl.BlockSpec((rb, width), lambda i: (i, 0)),
        compiler_params=_cparams(("parallel",)),
        name="conv_silu",
    )(p_all, p_all, p_all, conv_w)


def _log_sigmoid(x):
    return jnp.minimum(x, 0.0) - jnp.log(1.0 + jnp.exp(-jnp.abs(x)))


def _mlstm_body(q_ref, k_ref, v_ref, gc_ref, gr_ref, bc_ref, br_ref, o_ref, c_st, n_st, m_st):
    h = pl.program_id(1)
    direction = pl.program_id(2)
    c = pl.program_id(3)
    L = MLSTM_CHUNK
    hi = lax.Precision.HIGHEST

    @pl.when(c == 0)
    def _():
        c_st[...] = jnp.zeros_like(c_st)
        n_st[...] = jnp.zeros_like(n_st)
        m_st[...] = jnp.full_like(m_st, M_INIT)

    gi = direction * (2 * H_B) + h
    gcol = gc_ref[...] + bc_ref[...]
    lane = lax.broadcasted_iota(jnp.int32, gcol.shape, 1)
    li_c = jnp.sum(jnp.where(lane == gi, gcol, 0.0), axis=1, keepdims=True)
    lf_c = _log_sigmoid(jnp.sum(jnp.where(lane == gi + H_B, gcol, 0.0), axis=1, keepdims=True))
    grow = gr_ref[...] + br_ref[...]
    sub = lax.broadcasted_iota(jnp.int32, grow.shape, 0)
    li_r = jnp.sum(jnp.where(sub == gi, grow, 0.0), axis=0, keepdims=True)
    lf_r = _log_sigmoid(jnp.sum(jnp.where(sub == gi + H_B, grow, 0.0), axis=0, keepdims=True))

    t_i = lax.broadcasted_iota(jnp.int32, (L, L), 0)
    s_i = lax.broadcasted_iota(jnp.int32, (L, L), 1)
    sgn = 1 - 2 * direction
    incl = (s_i - t_i) * sgn <= 0
    incl_f = incl.astype(F32)
    incl_t = ((t_i - s_i) * sgn <= 0).astype(F32)
    bcum_c = jnp.dot(incl_f, jnp.broadcast_to(lf_c, (L, L)), precision=hi, preferred_element_type=F32)
    bcum_r = jnp.dot(jnp.broadcast_to(lf_r, (8, L)), incl_t, precision=hi, preferred_element_type=F32)[0:1, :]
    b_last = jnp.sum(lf_r, axis=1, keepdims=True)

    m_prev = m_st[...]
    a_c = bcum_c[:, 0:1] + m_prev
    dlog = jnp.where(incl, bcum_c - bcum_r + li_r, -jnp.inf)
    m_t = jnp.maximum(a_c, jnp.max(dlog, axis=1, keepdims=True))
    w_intra = jnp.exp(dlog - m_t)
    w_inter = jnp.exp(a_c - m_t)

    q = q_ref[...]
    k = k_ref[...]
    v = v_ref[...].astype(BF16)
    c_prev = c_st[...]
    n_prev = n_st[...]
    s = lax.dot_general(q, k, (((1,), (1,)), ((), ())), preferred_element_type=F32) * w_intra
    num = (w_inter * jnp.dot(q, c_prev.astype(BF16), preferred_element_type=F32)
           + jnp.dot(s.astype(BF16), v, preferred_element_type=F32))
    den = (w_inter * jnp.sum(q.astype(F32) * n_prev, axis=1, keepdims=True)
           + jnp.sum(s, axis=1, keepdims=True))
    o_ref[0] = num / jnp.maximum(jnp.abs(den), jnp.exp(-m_t))

    g_r = b_last - bcum_r + li_r
    g_c = b_last - bcum_c[:, 0:1] + li_c
    m_new = jnp.maximum(b_last + m_prev, jnp.max(g_r, axis=1, keepdims=True))
    decay = jnp.exp(b_last + m_prev - m_new)
    kw = k.astype(F32) * jnp.exp(g_c - m_new)
    c_st[...] = decay * c_prev + lax.dot_general(kw.astype(BF16), v, (((0,), (0,)), ((), ())),
                                                 preferred_element_type=F32)
    n_st[...] = decay * n_prev + jnp.sum(kw, axis=0, keepdims=True)
    m_st[...] = m_new


def mlstm_scan(qk, p_all, gates_t, gate_b, *, batch, seq, ctx_len):
    n = qk.shape[0]
    L = MLSTM_CHUNK
    cc, lc = ctx_len // L, seq // L
    n_lat_blk = batch * lc

    def rowblk(b, d, c):
        ctx_blk = n_lat_blk + b * cc + jnp.where(d == 0, c, cc - 1 - c)
        lat_blk = b * lc + jnp.where(d == 0, c - cc, lc - 1 - (c - cc))
        return jnp.where(c < cc, ctx_blk, lat_blk)

    gb_col = jnp.zeros((1, 128), F32).at[0, :4 * H_B].set(gate_b)
    gb_row = gate_b.reshape(4 * H_B, 1)
    return pl.pallas_call(
        _mlstm_body,
        out_shape=jax.ShapeDtypeStruct((2, n, W_B), F32),
        grid=(batch, H_B, 2, cc + lc),
        in_specs=[
            pl.BlockSpec((L, DQK_B), lambda b, h, d, c: (rowblk(b, d, c), h)),
            pl.BlockSpec((L, DQK_B), lambda b, h, d, c: (rowblk(b, d, c), H_B + h)),
            pl.BlockSpec((L, DV_B), lambda b, h, d, c: (rowblk(b, d, c), P_V // DV_B + h)),
            pl.BlockSpec((L, 128), lambda b, h, d, c: (rowblk(b, d, c), P_GATE // 128)),
            pl.BlockSpec((4 * H_B, L), lambda b, h, d, c: (0, rowblk(b, d, c))),
            pl.BlockSpec((1, 128), lambda b, h, d, c: (0, 0)),
            pl.BlockSpec((4 * H_B, 1), lambda b, h, d, c: (0, 0)),
        ],
        out_specs=pl.BlockSpec((1, L, DV_B), lambda b, h, d, c: (d, rowblk(b, d, c), h)),
        scratch_shapes=[pltpu.VMEM((DQK_B, DV_B), F32), pltpu.VMEM((1, DQK_B), F32), pltpu.VMEM((1, 1), F32)],
        compiler_params=_cparams(("arbitrary",) * 4),
        name="mlstm_scan",
    )(qk, qk, p_all, p_all, gates_t, gb_col, gb_row)


def _mlstm_finish_body(h_ref, o_ref, g_ref, out_ref):
    hs = h_ref[0] + h_ref[1]
    g = g_ref[...]
    og = o_ref[...]
    for hd in range(H_B):
        sl = slice(hd * DV_B, (hd + 1) * DV_B)
        x = hs[:, sl]
        ms = jnp.mean(x * x, axis=-1, keepdims=True)
        hn = x * lax.rsqrt(ms + NORM_EPS) * g[:, sl]
        out_ref[:, sl] = (jax.nn.sigmoid(og[:, sl]) * hn).astype(out_ref.dtype)


def mlstm_finish(hdir, p_all, out_norm_g):
    n = hdir.shape[1]
    rb = ROW_BLOCK
    return pl.pallas_call(
        _mlstm_finish_body,
        out_shape=jax.ShapeDtypeStruct((n, W_B), BF16),
        grid=(n // rb,),
        in_specs=[
            pl.BlockSpec((2, rb, W_B), lambda i: (0, i, 0)),
            pl.BlockSpec((rb, W_B), lambda i: (i, P_O // W_B)),
            pl.BlockSpec((1, W_B), lambda i: (0, 0)),
        ],
        out_specs=pl.BlockSpec((rb, W_B), lambda i: (i, 0)),
        compiler_params=_cparams(("parallel",)),
        name="mlstm_finish",
    )(hdir, p_all, out_norm_g.reshape(1, W_B))


NA_ROWS_PER_STEP = 4


def _na_geometry(rows_n):
    r_t = NA_ROWS_PER_STEP
    kh = min(NA_KH, rows_n)
    win = min(r_t + kh - 1, rows_n)
    tiles = rows_n // r_t
    starts, sigs, cls = [], [], []
    for t in range(tiles):
        rq0 = t * r_t
        start = int(np.clip(rq0 - kh // 2, 0, rows_n - win))
        r0 = np.clip(np.arange(rq0, rq0 + r_t) - kh // 2, 0, rows_n - kh)
        sig = (start - rq0, tuple((r0 - rq0).tolist()))
        if sig not in sigs:
            sigs.append(sig)
        starts.append(start)
        cls.append(sigs.index(sig))
    return kh, win, np.array(starts, np.int32), np.array(cls, np.int32), sigs


def na_bias_tables(rpb, rows_n):
    r_t = NA_ROWS_PER_STEP
    kh, win, _, _, sigs = _na_geometry(rows_n)
    hi = lax.Precision.HIGHEST
    cols = np.arange(GRID_W)
    c0 = np.clip(cols - NA_KW // 2, 0, GRID_W - NA_KW)
    col_valid = (cols[None, :] >= c0[:, None]) & (cols[None, :] < c0[:, None] + NA_KW)
    col_off = cols[None, :] - cols[:, None] + (NA_KW - 1)
    c_sel = (col_valid[:, :, None] & (col_off[:, :, None] == np.arange(2 * NA_KW - 1))).astype(np.float32)
    r_sel = np.zeros((len(sigs), r_t, win, 2 * NA_KH - 1), np.float32)
    valid = np.zeros((len(sigs), r_t, GRID_W, win, GRID_W), bool)
    for ci, (dstart, dr0) in enumerate(sigs):
        a = np.arange(r_t)[:, None]
        j = np.arange(win)[None, :]
        krow = dstart + j
        r0 = np.array(dr0)[:, None]
        row_valid = (krow >= r0) & (krow < r0 + kh)
        row_off = krow - a + (NA_KH - 1)
        r_sel[ci] = row_valid[:, :, None] & (row_off[:, :, None] == np.arange(2 * NA_KH - 1))
        valid[ci] = row_valid[:, None, :, None] & col_valid[None, :, None, :]
    t = jnp.einsum('hrs,cds->hrcd', rpb, c_sel, precision=hi)
    b = jnp.einsum('hrcd,kajr->hkacjd', t, r_sel, precision=hi)
    b = b + np.where(valid, 0.0, NEG).astype(np.float32)[None]
    return b.reshape(H_C, len(sigs), r_t * GRID_W, win * GRID_W)


def _head_rmsnorm(x, g):
    xf = x.astype(F32)
    ms = jnp.mean(xf * xf, axis=-1, keepdims=True)
    return xf * lax.rsqrt(ms + NORM_EPS) * g


def _na_body(start_ref, cls_ref, q_ref, k_ref, v_ref, kc_ref, vc_ref, bias_ref, gq_ref, gk_ref, *rest,
             seq, ctx_len, win_rows, scale):
    o_ref, kn, knc = rest[-3:]
    i = pl.program_id(2)
    chunk = min(512, seq)

    @pl.when(i == 0)
    def _():
        gk = gk_ref[...]

        def body(c, carry):
            rows = pl.ds(pl.multiple_of(c * chunk, chunk), chunk)
            kn[rows, :] = _head_rmsnorm(k_ref[rows, :], gk).astype(BF16)
            return carry

        lax.fori_loop(0, seq // chunk, body, 0)
        knc[...] = _head_rmsnorm(kc_ref[...], gk).astype(BF16)

    q = (_head_rmsnorm(q_ref[...], gq_ref[...]) * scale).astype(BF16)
    krow0 = pl.multiple_of(start_ref[i] * GRID_W, GRID_W)
    kw = kn[pl.ds(krow0, win_rows), :]
    vw = v_ref[pl.ds(krow0, win_rows), :]
    nt = (((1,), (1,)), ((), ()))
    s = lax.dot_general(q, kw, nt, preferred_element_type=F32) + bias_ref[0, 0]
    sc = lax.dot_general(q, knc[...], nt, preferred_element_type=F32)
    m = jnp.maximum(jnp.max(s, axis=-1, keepdims=True), jnp.max(sc, axis=-1, keepdims=True))
    p = jnp.exp(s - m)
    pc = jnp.exp(sc - m)
    l = jnp.sum(p, axis=-1, keepdims=True) + jnp.sum(pc, axis=-1, keepdims=True)
    o = (jnp.dot(p.astype(BF16), vw, preferred_element_type=F32)
         + jnp.dot(pc.astype(BF16), vc_ref[...], preferred_element_type=F32))
    o_ref[...] = (o / l).astype(o_ref.dtype)


def na_attention(qkv, bias_tabs, gq, gk, *, batch, seq, ctx_len, out_buf=None):
    rows_n = seq // GRID_W
    r_t = NA_ROWS_PER_STEP
    _, win, starts, cls, _ = _na_geometry(rows_n)
    tq = r_t * GRID_W
    tiles = rows_n // r_t
    n_lat = batch * seq
    ctx_blk0 = n_lat // ctx_len
    body = functools.partial(_na_body, seq=seq, ctx_len=ctx_len, win_rows=win * GRID_W, scale=DH_C ** -0.5)
    in_specs = [
        pl.BlockSpec((tq, DH_C), lambda b, h, i, st, cl: (b * tiles + i, h)),
        pl.BlockSpec((seq, DH_C), lambda b, h, i, st, cl: (b, H_C + h)),
        pl.BlockSpec((seq, DH_C), lambda b, h, i, st, cl: (b, 2 * H_C + h)),
        pl.BlockSpec((ctx_len, DH_C), lambda b, h, i, st, cl: (ctx_blk0 + b, H_C + h)),
        pl.BlockSpec((ctx_len, DH_C), lambda b, h, i, st, cl: (ctx_blk0 + b, 2 * H_C + h)),
        pl.BlockSpec((1, 1, tq, win * GRID_W), lambda b, h, i, st, cl: (h, cl[i], 0, 0)),
        pl.BlockSpec((1, DH_C), lambda b, h, i, st, cl: (0, 0)),
        pl.BlockSpec((1, DH_C), lambda b, h, i, st, cl: (0, 0)),
    ]
    args = [jnp.asarray(starts), jnp.asarray(cls), qkv, qkv, qkv, qkv, qkv, bias_tabs, gq, gk]
    aliases = {}
    out_rows = n_lat
    if out_buf is not None:
        in_specs.append(pl.BlockSpec(memory_space=pl.ANY))
        aliases = {len(args): 0}
        args.append(out_buf)
        out_rows = out_buf.shape[0]
    grid_spec = pltpu.PrefetchScalarGridSpec(
        num_scalar_prefetch=2,
        grid=(batch, H_C, tiles),
        in_specs=in_specs,
        out_specs=pl.BlockSpec((tq, DH_C), lambda b, h, i, st, cl: (b * tiles + i, h)),
        scratch_shapes=[pltpu.VMEM((seq, DH_C), BF16), pltpu.VMEM((ctx_len, DH_C), BF16)],
    )
    return pl.pallas_call(
        body,
        out_shape=jax.ShapeDtypeStruct((out_rows, H_C * DH_C), BF16),
        grid_spec=grid_spec,
        input_output_aliases=aliases,
        compiler_params=_cparams(("arbitrary", "arbitrary", "arbitrary")),
        name="na_attention",
    )(*args)


def _ctx_attn_body(q_ref, k_ref, v_ref, gq_ref, gk_ref, buf_ref, o_ref, *, scale):
    del buf_ref
    q = (_head_rmsnorm(q_ref[...], gq_ref[...]) * scale).astype(BF16)
    k = _head_rmsnorm(k_ref[...], gk_ref[...]).astype(BF16)
    s = lax.dot_general(q, k, (((1,), (1,)), ((), ())), preferred_element_type=F32)
    m = jnp.max(s, axis=-1, keepdims=True)
    p = jnp.exp(s - m)
    l = jnp.sum(p, axis=-1, keepdims=True)
    o = jnp.dot(p.astype(BF16), v_ref[...], preferred_element_type=F32)
    o_ref[...] = (o / l).astype(o_ref.dtype)


def na_ctx_attention(qkv, gq, gk, out_buf, *, batch, seq, ctx_len):
    ctx_blk0 = batch * seq // ctx_len
    return pl.pallas_call(
        functools.partial(_ctx_attn_body, scale=DH_C ** -0.5),
        out_shape=jax.ShapeDtypeStruct(out_buf.shape, out_buf.dtype),
        grid=(batch, H_C),
        in_specs=[
            pl.BlockSpec((ctx_len, DH_C), lambda b, h: (ctx_blk0 + b, h)),
            pl.BlockSpec((ctx_len, DH_C), lambda b, h: (ctx_blk0 + b, H_C + h)),
            pl.BlockSpec((ctx_len, DH_C), lambda b, h: (ctx_blk0 + b, 2 * H_C + h)),
            pl.BlockSpec((1, DH_C), lambda b, h: (0, 0)),
            pl.BlockSpec((1, DH_C), lambda b, h: (0, 0)),
            pl.BlockSpec(memory_space=pl.ANY),
        ],
        out_specs=pl.BlockSpec((ctx_len, DH_C), lambda b, h: (ctx_blk0 + b, h)),
        input_output_aliases={5: 0},
        compiler_params=_cparams(("parallel", "parallel")),
        name="na_ctx_attention",
    )(qkv, qkv, qkv, gq, gk, out_buf)


def _slab_copy(src_ref, src_tok, dst_ref, dst_tok, sem):
    def rows(tok):
        off = tok * SLAB_STRIDE
        return pl.ds(off if isinstance(off, int) else pl.multiple_of(off, 8), SLAB_ROWS)

    return pltpu.make_async_copy(src_ref.at[rows(src_tok), :], dst_ref.at[rows(dst_tok), :], sem)


def _load_slabs(ref, rows, dtype):
    return jnp.concatenate([ref[pl.ds(j, rows, stride=SLAB_STRIDE), :].astype(dtype) for j in range(SLAB_ROWS)],
                           axis=1)


def moe_slot_plan(route, counts, n_tiles):
    cnt = counts[0, :N_EXPERTS].astype(jnp.int32)
    padded = (cnt + MOE_TILE - 1) // MOE_TILE * MOE_TILE
    ends = jnp.cumsum(padded)
    base = ends - padded
    n_used = ends[-1] // MOE_TILE
    tile_start = jnp.arange(n_tiles, dtype=jnp.int32) * MOE_TILE
    last_start = jnp.maximum(ends[-1] - MOE_TILE, 0)
    tile_expert = jnp.searchsorted(ends, jnp.minimum(tile_start, last_start), side='right').astype(jnp.int32)
    tile_expert = jnp.minimum(tile_expert, N_EXPERTS - 1)
    ids = route[:, 0:2].astype(jnp.int32)
    pos = jnp.take(base, ids, axis=0) + route[:, 4:6].astype(jnp.int32)
    return pos[:, 0], pos[:, 1], tile_expert, n_used.reshape(1).astype(jnp.int32)


def _slot_source_body(p1_ref, p2_ref, src_ref, *, n_tok):
    def clear(i, c):
        src_ref[i] = 0
        return c

    lax.fori_loop(0, src_ref.shape[0], clear, 0, unroll=8)

    def fill(n, c):
        src_ref[p1_ref[n]] = n
        src_ref[p2_ref[n]] = n
        return c

    lax.fori_loop(0, n_tok, fill, 0, unroll=4)


def moe_slot_source(pos1, pos2, n_slots):
    smem = pl.BlockSpec(memory_space=pltpu.SMEM)
    return pl.pallas_call(
        functools.partial(_slot_source_body, n_tok=pos1.shape[0]),
        out_shape=jax.ShapeDtypeStruct((n_slots,), jnp.int32),
        in_specs=[smem, smem],
        out_specs=smem,
        name="moe_slot_source",
    )(pos1, pos2)


def _moe_expert_body(te_ref, nu_ref, src_ref, hs_ref, wg_ref, wu_ref, wd_ref, o_ref, xbuf, wg_bf, wu_bf, wd_bf,
                     sem):
    t = pl.program_id(0)
    tm = MOE_TILE

    @pl.when(t < nu_ref[0])
    def _():
        def issue(r, c):
            _slab_copy(hs_ref, src_ref[t * tm + r], xbuf, r, sem).start()
            return c

        lax.fori_loop(0, tm, issue, 0)

        @pl.when((t == 0) | (te_ref[t] != te_ref[jnp.maximum(t - 1, 0)]))
        def _():
            wg_bf[...] = wg_ref[...].astype(BF16)
            wu_bf[...] = wu_ref[...].astype(BF16)
            wd_bf[...] = wd_ref[...].astype(BF16)

        def wait(r, c):
            _slab_copy(hs_ref, 0, xbuf, 0, sem).wait()
            return c

        lax.fori_loop(0, tm, wait, 0)
        x = _load_slabs(xbuf, tm, BF16)
        nt = (((1,), (1,)), ((), ()))
        a = lax.dot_general(x, wg_bf[...], nt, preferred_element_type=F32)
        u = lax.dot_general(x, wu_bf[...], nt, preferred_element_type=F32)
        y = (a * jax.nn.sigmoid(a) * u).astype(BF16)
        _store_slabs(o_ref, jnp.dot(y, wd_bf[...], preferred_element_type=F32), tm)

    @pl.when(t >= nu_ref[0])
    def _():
        o_ref[...] = jnp.zeros_like(o_ref)


def moe_experts(hs, src, tile_expert, n_used, w_gate_t, w_up_t, w_down, layer):
    n_tiles = tile_expert.shape[0]
    f, d = w_gate_t.shape[-2:]
    blk = MOE_TILE * SLAB_STRIDE
    wspec = pl.BlockSpec((None, None, f, d), lambda t, te, nu, src: (layer, te[t], 0, 0))
    grid_spec = pltpu.PrefetchScalarGridSpec(
        num_scalar_prefetch=3,
        grid=(n_tiles,),
        in_specs=[pl.BlockSpec(memory_space=pl.ANY), wspec, wspec, wspec],
        out_specs=pl.BlockSpec((blk, 128), lambda t, te, nu, src: (t, 0)),
        scratch_shapes=[pltpu.VMEM((blk, 128), F32), pltpu.VMEM((f, d), BF16), pltpu.VMEM((f, d), BF16),
                        pltpu.VMEM((f, d), BF16), pltpu.SemaphoreType.DMA],
    )
    return pl.pallas_call(
        _moe_expert_body,
        out_shape=jax.ShapeDtypeStruct((n_tiles * blk, 128), F32),
        grid_spec=grid_spec,
        compiler_params=_cparams(("arbitrary",)),
        name="moe_experts",
    )(tile_expert, n_used, src, hs, w_gate_t, w_up_t, w_down)


def _moe_combine_body(p1_ref, p2_ref, ys_ref, route_ref, res_ref, gate_ref, o_ref, abuf, bbuf, sem):
    i = pl.program_id(0)
    rb = ROW_BLOCK

    def issue(r, c):
        _slab_copy(ys_ref, p1_ref[i * rb + r], abuf, r, sem).start()
        _slab_copy(ys_ref, p2_ref[i * rb + r], bbuf, r, sem).start()
        return c

    lax.fori_loop(0, rb, issue, 0)
    route = route_ref[...]
    lane = lax.broadcasted_iota(jnp.int32, route.shape, 1)
    w1 = jnp.sum(jnp.where(lane == 2, route, 0.0), axis=1, keepdims=True)
    w2 = jnp.sum(jnp.where(lane == 3, route, 0.0), axis=1, keepdims=True)

    def wait(r, c):
        _slab_copy(ys_ref, 0, abuf, 0, sem).wait()
        return c

    lax.fori_loop(0, 2 * rb, wait, 0)
    for j in range(SLAB_ROWS):
        sl = slice(j * 128, (j + 1) * 128)
        rows = pl.ds(j, rb, stride=SLAB_STRIDE)
        mix = w1 * abuf[rows, :] + w2 * bbuf[rows, :]
        o_ref[:, sl] = res_ref[:, sl] + gate_ref[0][:, sl] * mix


def moe_combine(ys, pos1, pos2, route, resid, mods, k_gate, dims):
    n, d = route.shape[0], resid.shape[1]
    n_lat, seq, batch = dims
    rb = ROW_BLOCK
    mrow = functools.partial(_mod_row, rows_per_block=rb, n_lat=n_lat, seq=seq, batch=batch)
    grid_spec = pltpu.PrefetchScalarGridSpec(
        num_scalar_prefetch=2,
        grid=(n // rb,),
        in_specs=[
            pl.BlockSpec(memory_space=pl.ANY),
            pl.BlockSpec((rb, 128), lambda i, p1, p2: (i, 0)),
            pl.BlockSpec((rb, d), lambda i, p1, p2: (i, 0)),
            pl.BlockSpec((1, 1, d), lambda i, p1, p2: (mrow(i) * 6 + k_gate, 0, 0)),
        ],
        out_specs=pl.BlockSpec((rb, d), lambda i, p1, p2: (i, 0)),
        scratch_shapes=[pltpu.VMEM((rb * SLAB_STRIDE, 128), F32), pltpu.VMEM((rb * SLAB_STRIDE, 128), F32),
                        pltpu.SemaphoreType.DMA],
    )
    return pl.pallas_call(
        _moe_combine_body,
        out_shape=jax.ShapeDtypeStruct((n, d), F32),
        grid_spec=grid_spec,
        compiler_params=_cparams(("arbitrary",)),
        name="moe_combine",
    )(pos1, pos2, ys, route, resid, mods)


def _swap16(a):
    lead = a.shape[:-1]
    return jnp.flip(a.reshape(*lead, D_ROPE // 32, 2, 16), axis=-2).reshape(*lead, D_ROPE)


def _ext_w_in(w):
    c_kr = Q_LORA + KV_LORA
    c_qb = c_kr + D_ROPE
    c_g = c_qb + 2 * H_B * DQK_B + 2 * W_B
    kr = w[:, c_kr:c_qb]
    used = (c_g - c_qb) + c_kr + 2 * D_ROPE
    gates = jnp.pad(w[:, c_g:], ((0, 0), (0, P_WIDTH - used - 4 * H_B)))
    return jnp.concatenate([w[:, c_qb:c_g], w[:, :c_kr], kr, _swap16(kr), gates], axis=1)


def _ext_w_q_up(w):
    w3 = w.reshape(Q_LORA, H_A, D_NOPE + D_ROPE)
    return jnp.concatenate([w3, _swap16(w3[:, :, D_NOPE:])], axis=2).reshape(Q_LORA, H_A * 256)


def _ext_gain(g):
    return jnp.concatenate([g, _swap16(g[D_NOPE:])]).reshape(1, 256)


def _rope_tables(seq):
    t = jnp.arange(seq)
    half = D_ROPE // 4
    inv = ROPE_THETA ** (-jnp.arange(half, dtype=F32) / half)
    ar = (t // GRID_W).astype(F32)[:, None] * inv[None, :]
    ac = (t % GRID_W).astype(F32)[:, None] * inv[None, :]
    cr, sr, cc, sc = jnp.cos(ar), jnp.sin(ar), jnp.cos(ac), jnp.sin(ac)
    return jnp.concatenate([cr, cr, cc, cc, -sr, sr, -sc, sc], axis=1)


def kernel(x, c, ctx, c_ctx, ada_w, ada_b, norm1_g, norm2_g, ab_w_in, mla_q_norm_g, mla_kv_norm_g, mla_w_q_up, mla_w_kv_up, mla_qn_g, mla_kn_g, mlstm_conv_w, mlstm_gate_b, mlstm_out_norm_g, ab_w_out, na_w_qkv, na_qn_g, na_kn_g, na_rpb, na_w_out, moe_w_rg, moe_b_rg, moe_w_re, moe_b_re, moe_w_gate, moe_w_up, moe_w_down):
    batch, seq, d = x.shape
    ctx_len = ctx.shape[1]
    depth = ada_w.shape[0]
    n_lat = batch * seq
    dims = (n_lat, seq, batch)
    assert d == D_MODEL and batch + 1 <= MOD_ROWS and seq % 512 == 0 and ctx_len == ROW_BLOCK

    xs = jnp.concatenate([x.reshape(n_lat, d), ctx.reshape(batch * ctx_len, d)], axis=0)
    cvec = jnp.concatenate([c, c_ctx[None, :]], axis=0)
    mods_all = ada_mods(cvec, ada_w, ada_b).reshape(depth, MOD_ROWS * 6, 1, d)
    trig = _rope_tables(seq)

    w_gate_t, w_up_t = jnp.swapaxes(moe_w_gate, 2, 3), jnp.swapaxes(moe_w_up, 2, 3)
    kw = dict(batch=batch, seq=seq, ctx_len=ctx_len)
    for layer in range(depth):
        j = layer // 2
        last = layer == depth - 1
        mods = mods_all[layer]
        h = norm_mod(xs, norm1_g[layer], mods, (0, 1), dims)
        if layer % 2 == 0:
            p_all = matmul(h, _ext_w_in(ab_w_in[j]), out_dtype=F32)
            q_all = matmul(p_all, _ext_w_q_up(mla_w_q_up[j]), out_dtype=BF16, a_col=P_CQ // Q_LORA, k=Q_LORA,
                           rms_gain=mla_q_norm_g[j])
            kv_all = matmul(p_all, mla_w_kv_up, layer=j, out_dtype=BF16, a_col=P_CKV // KV_LORA, k=KV_LORA,
                            rms_gain=mla_kv_norm_g[j])
            gq, gk = _ext_gain(mla_qn_g[j]), _ext_gain(mla_kn_g[j])
            mix_a = jnp.zeros((xs.shape[0], W_A), BF16)
            mix_a = mla_attention(q_all, kv_all, p_all, trig, gq, gk, latent_queries=True, out_buf=mix_a, **kw)
            mix_a = mla_attention(q_all, kv_all, p_all, trig, gq, gk, latent_queries=False, out_buf=mix_a, **kw)
            qk = conv_silu(p_all, mlstm_conv_w[j], n_lat=n_lat, seq=seq, ctx_len=ctx_len)
            gates_t = p_all[:, P_GATE:P_GATE + 4 * H_B].T
            hdir = mlstm_scan(qk, p_all, gates_t, mlstm_gate_b[j], **kw)
            mix_b = mlstm_finish(hdir, p_all, mlstm_out_norm_g[j])
            xs = matmul(mix_a, ab_w_out, layer=j, a2=mix_b, out_dtype=F32, resid=xs, mods=mods, k_gate=2,
                        dims=dims)
        else:
            qkv = matmul(h, na_w_qkv, layer=j, out_dtype=BF16)
            gq, gk = na_qn_g[j].reshape(1, DH_C), na_kn_g[j].reshape(1, DH_C)
            bias = na_bias_tables(na_rpb[j], seq // GRID_W)
            if last:
                mixed = na_attention(qkv, bias, gq, gk, **kw)
            else:
                mixed = jnp.zeros((xs.shape[0], H_C * DH_C), BF16)
                mixed = na_attention(qkv, bias, gq, gk, out_buf=mixed, **kw)
                mixed = na_ctx_attention(qkv, gq, gk, mixed, **kw)
            xs = matmul(mixed, na_w_out, layer=j, out_dtype=F32, resid=xs, mods=mods, k_gate=2, dims=dims)

        w_r = jnp.pad(jnp.concatenate([moe_w_re[layer], moe_w_rg[layer]], axis=1),
                      ((0, 0), (0, 128 - N_EXPERTS - N_GROUPS)))
        b_r = jnp.pad(jnp.concatenate([moe_b_re[layer], moe_b_rg[layer]]), (0, 128 - N_EXPERTS - N_GROUPS))
        hs, route, counts = norm_router(xs, norm2_g[layer], mods, (3, 4), dims, w_r, b_r.reshape(1, 128))
        n_tiles = 2 * xs.shape[0] // MOE_TILE + N_EXPERTS
        pos1, pos2, tile_expert, n_used = moe_slot_plan(route, counts, n_tiles)
        src = moe_slot_source(pos1, pos2, n_tiles * MOE_TILE)
        ys = moe_experts(hs, src, tile_expert, n_used, w_gate_t, w_up_t, moe_w_down, layer)
        xs = moe_combine(ys, pos1, pos2, route, xs, mods, 5, dims)
    return xs[:n_lat].reshape(batch, seq, d) if xs.shape[0] != n_lat else xs.reshape(batch, seq, d)
```

```python
import functools

import numpy as np
import jax
import jax.numpy as jnp
from jax import lax
from jax.experimental import pallas as pl
from jax.experimental.pallas import tpu as pltpu

F32 = jnp.float32
BF16 = jnp.bfloat16

D_MODEL = 4096
GRID_W = 64
H_A = 16
Q_LORA = 1024
KV_LORA = 512
D_NOPE = 128
D_ROPE = 64
D_V_A = 128
H_B = 4
DQK_B = 256
DV_B = 512
MLSTM_CHUNK = 128
M_INIT = -1e30
H_C = 32
DH_C = 128
NA_KH = 8
NA_KW = 16
N_GROUPS = 4
EXPERTS_PER_GROUP = 8
N_EXPERTS = 32
D_FF_EXPERT = 192
ROPE_THETA = 10000.0
NORM_EPS = 1e-6
W_A = H_A * D_V_A
W_B = H_B * DV_B
NEG = -1e30

VMEM_LIMIT_BYTES = 56 * 1024 * 1024
ROW_BLOCK = 256
MOD_ROWS = 8
SLAB_ROWS = 32
SLAB_STRIDE = 40
MOE_TILE = 256

P_QK = 0
P_V = 2048
P_O = 4096
P_CQ = 6144
P_CKV = 7168
P_KR = 7680
P_GATE = 7808
P_WIDTH = 8192


def _cparams(sem):
    return pltpu.CompilerParams(dimension_semantics=sem, vmem_limit_bytes=VMEM_LIMIT_BYTES)


def _dma_cparams(sem):
    return pltpu.CompilerParams(dimension_semantics=sem, vmem_limit_bytes=VMEM_LIMIT_BYTES,
                                disable_bounds_checks=True)


def _mod_row(i, rows_per_block, n_lat, seq, batch):
    return jnp.where(i < n_lat // rows_per_block, i // (seq // rows_per_block), batch)


ADA_ROW_CHUNK = 512


def _ada_body(c_ref, w_ref, b_ref, o_ref, *, n_vec):
    d, tn = w_ref.shape[1], w_ref.shape[2]
    reps = tn // 128

    def chunk(ci, accs):
        rows = pl.ds(pl.multiple_of(ci * ADA_ROW_CHUNK, ADA_ROW_CHUNK), ADA_ROW_CHUNK)
        w = w_ref[0, rows, :]
        out = []
        for m in range(n_vec):
            c = c_ref[m, rows, :]
            cm = jnp.concatenate([c * jax.nn.sigmoid(c)] * reps, axis=1)
            part = (w * cm).reshape(ADA_ROW_CHUNK // 8, 8, tn).sum(axis=0)
            out.append(accs[m] + part)
        return tuple(out)

    accs = lax.fori_loop(0, d // ADA_ROW_CHUNK, chunk, tuple(jnp.zeros((8, tn), F32) for _ in range(n_vec)))
    rows = [jnp.sum(a, axis=0, keepdims=True) for a in accs]
    rows += [jnp.zeros((1, tn), F32)] * (MOD_ROWS - n_vec)
    o_ref[0] = jnp.concatenate(rows, axis=0) + b_ref[0]


def ada_mods(cvec, ada_w, ada_b, tn=512):
    depth, d, n6 = ada_w.shape
    n_vec = cvec.shape[0]
    c_lanes = jnp.broadcast_to(cvec[:, :, None], (n_vec, d, 128))
    return pl.pallas_call(
        functools.partial(_ada_body, n_vec=n_vec),
        out_shape=jax.ShapeDtypeStruct((depth, MOD_ROWS, n6), F32),
        grid=(depth, n6 // tn),
        in_specs=[
            pl.BlockSpec((n_vec, d, 128), lambda l, j: (0, 0, 0)),
            pl.BlockSpec((1, d, tn), lambda l, j: (l, 0, j)),
            pl.BlockSpec((1, 1, tn), lambda l, j: (l, 0, j)),
        ],
        out_specs=pl.BlockSpec((1, MOD_ROWS, tn), lambda l, j: (l, 0, j)),
        compiler_params=_cparams(("arbitrary", "arbitrary")),
        name="ada_mods",
    )(c_lanes, ada_w, ada_b.reshape(depth, 1, n6))


def _modulated_norm(x_ref, g_ref, sh_ref, sc_ref):
    x = x_ref[...]
    ms = jnp.mean(x * x, axis=-1, keepdims=True)
    y = x * lax.rsqrt(ms + NORM_EPS) * g_ref[...]
    return y * (1.0 + sc_ref[0]) + sh_ref[0]


def _norm_body(x_ref, g_ref, sh_ref, sc_ref, o_ref):
    o_ref[...] = _modulated_norm(x_ref, g_ref, sh_ref, sc_ref).astype(o_ref.dtype)


def _store_slabs(ref, mat, rows):
    for j in range(SLAB_ROWS):
        ref[pl.ds(j, rows, stride=SLAB_STRIDE), :] = mat[:, j * 128:(j + 1) * 128]
    for j in range(SLAB_ROWS, SLAB_STRIDE):
        ref[pl.ds(j, rows, stride=SLAB_STRIDE), :] = jnp.zeros((rows, 128), mat.dtype)


def _norm_router_body(x_ref, g_ref, sh_ref, sc_ref, wr_ref, br_ref, hs_ref, route_ref, cnt_ref, carry):
    @pl.when(pl.program_id(0) == 0)
    def _():
        carry[...] = jnp.zeros_like(carry)

    h = _modulated_norm(x_ref, g_ref, sh_ref, sc_ref)
    _store_slabs(hs_ref, h, ROW_BLOCK)
    logits = jnp.dot(h, wr_ref[...], precision=lax.Precision.HIGHEST,
                     preferred_element_type=F32) + br_ref[...]
    lane = lax.broadcasted_iota(jnp.int32, logits.shape, 1)
    big = jnp.int32(1 << 20)
    is_g = (lane >= N_EXPERTS) & (lane < N_EXPERTS + N_GROUPS)
    gl = jnp.where(is_g, logits, NEG)
    gmax = jnp.max(gl, axis=-1, keepdims=True)
    g_idx = jnp.min(jnp.where(gl == gmax, lane, big), axis=-1, keepdims=True) - N_EXPERTS
    p_g = 1.0 / jnp.sum(jnp.where(is_g, jnp.exp(gl - gmax), 0.0), axis=-1, keepdims=True)
    lo = g_idx * EXPERTS_PER_GROUP
    in_grp = (lane >= lo) & (lane < lo + EXPERTS_PER_GROUP)
    el = jnp.where(in_grp, logits, NEG)
    e1 = jnp.max(el, axis=-1, keepdims=True)
    i1 = jnp.min(jnp.where(el == e1, lane, big), axis=-1, keepdims=True)
    el2 = jnp.where(lane == i1, NEG, el)
    e2 = jnp.max(el2, axis=-1, keepdims=True)
    i2 = jnp.min(jnp.where(el2 == e2, lane, big), axis=-1, keepdims=True)
    t = jnp.exp(e2 - e1)
    w1 = p_g / (1.0 + t)
    w2 = w1 * t
    onehot = jnp.where((lane == i1) | (lane == i2), 1.0, 0.0)
    t_i = lax.broadcasted_iota(jnp.int32, (ROW_BLOCK, ROW_BLOCK), 0)
    s_i = lax.broadcasted_iota(jnp.int32, (ROW_BLOCK, ROW_BLOCK), 1)
    before = jnp.where(s_i < t_i, 1.0, 0.0).astype(BF16)
    seen = jnp.dot(before, onehot.astype(BF16), preferred_element_type=F32) + carry[...]
    rank1 = jnp.sum(jnp.where(lane == i1, seen, 0.0), axis=-1, keepdims=True)
    rank2 = jnp.sum(jnp.where(lane == i2, seen, 0.0), axis=-1, keepdims=True)
    total = carry[...] + jnp.sum(onehot, axis=0, keepdims=True)
    carry[...] = total
    cnt_ref[...] = jnp.broadcast_to(total, cnt_ref.shape)
    cols = (i1.astype(F32), i2.astype(F32), w1, w2, rank1, rank2)
    route = jnp.zeros(logits.shape, F32)
    for k, v in enumerate(cols):
        route = jnp.where(lane == k, v, route)
    route_ref[...] = route


def _norm_specs(d, mods_k, dims):
    n_lat, seq, batch = dims
    mrow = functools.partial(_mod_row, rows_per_block=ROW_BLOCK, n_lat=n_lat, seq=seq, batch=batch)
    k_shift, k_scale = mods_k
    return [
        pl.BlockSpec((ROW_BLOCK, d), lambda i: (i, 0)),
        pl.BlockSpec((1, d), lambda i: (0, 0)),
        pl.BlockSpec((1, 1, d), lambda i: (mrow(i) * 6 + k_shift, 0, 0)),
        pl.BlockSpec((1, 1, d), lambda i: (mrow(i) * 6 + k_scale, 0, 0)),
    ]


def norm_mod(x, g, mods, mods_k, dims):
    n, d = x.shape
    return pl.pallas_call(
        _norm_body,
        out_shape=jax.ShapeDtypeStruct((n, d), BF16),
        grid=(n // ROW_BLOCK,),
        in_specs=_norm_specs(d, mods_k, dims),
        out_specs=pl.BlockSpec((ROW_BLOCK, d), lambda i: (i, 0)),
        compiler_params=_cparams(("parallel",)),
        name="norm_mod",
    )(x, g.reshape(1, d), mods, mods)


def norm_router(x, g, mods, mods_k, dims, w_r, b_r):
    n, d = x.shape
    assert d == SLAB_ROWS * 128
    return pl.pallas_call(
        _norm_router_body,
        out_shape=(jax.ShapeDtypeStruct((n * SLAB_STRIDE, 128), F32), jax.ShapeDtypeStruct((n, 128), F32),
                   jax.ShapeDtypeStruct((8, 128), F32)),
        grid=(n // ROW_BLOCK,),
        in_specs=_norm_specs(d, mods_k, dims) + [
            pl.BlockSpec((d, 128), lambda i: (0, 0)),
            pl.BlockSpec((1, 128), lambda i: (0, 0)),
        ],
        out_specs=(pl.BlockSpec((ROW_BLOCK * SLAB_STRIDE, 128), lambda i: (i, 0)),
                   pl.BlockSpec((ROW_BLOCK, 128), lambda i: (i, 0)),
                   pl.BlockSpec((8, 128), lambda i: (0, 0))),
        scratch_shapes=[pltpu.VMEM((1, 128), F32)],
        compiler_params=_cparams(("arbitrary",)),
        name="norm_router",
    )(x, g.reshape(1, d), mods, mods, w_r, b_r)


def _mm_body(*refs, rms, gated, k1, two_a):
    it = iter(refs)
    a_ref = next(it)
    a2_ref = next(it) if two_a else None
    g_ref = next(it) if rms else None
    w_ref = next(it)
    res_ref = next(it) if gated else None
    gate_ref = next(it) if gated else None
    o_ref = next(it)
    wbf_ref = next(it)

    @pl.when(pl.program_id(1) == 0)
    def _():
        wbf_ref[...] = w_ref[...].astype(BF16)

    a = a_ref[...]
    if rms:
        af = a.astype(F32)
        ms = jnp.mean(af * af, axis=-1, keepdims=True)
        a = (af * lax.rsqrt(ms + NORM_EPS) * g_ref[...]).astype(BF16)
    if two_a:
        acc = (jnp.dot(a, wbf_ref[:k1, :], preferred_element_type=F32)
               + jnp.dot(a2_ref[...], wbf_ref[k1:, :], preferred_element_type=F32))
    else:
        acc = jnp.dot(a, wbf_ref[...], preferred_element_type=F32)
    if gated:
        acc = res_ref[...] + gate_ref[0] * acc
    o_ref[...] = acc.astype(o_ref.dtype)


def matmul(a, w, *, out_dtype, layer=None, a_col=0, k=None, a2=None, rms_gain=None, resid=None, mods=None,
           k_gate=None, dims=None, tm=512, tn=512):
    m = a.shape[0]
    kw_, nw = w.shape[-2], w.shape[-1]
    two_a = a2 is not None
    k = (a.shape[1] if two_a else kw_) if k is None else k
    assert kw_ == k + (a2.shape[1] if two_a else 0) and m % tm == 0 and nw % tn == 0
    rms = rms_gain is not None
    gated = resid is not None
    in_specs = [pl.BlockSpec((tm, k), lambda j, i: (i, a_col))]
    args = [a]
    if two_a:
        in_specs.append(pl.BlockSpec((tm, a2.shape[1]), lambda j, i: (i, 0)))
        args.append(a2)
    if rms:
        in_specs.append(pl.BlockSpec((1, k), lambda j, i: (0, 0)))
        args.append(rms_gain.reshape(1, k))
    if layer is None:
        in_specs.append(pl.BlockSpec((kw_, tn), lambda j, i: (0, j)))
    else:
        in_specs.append(pl.BlockSpec((None, kw_, tn), lambda j, i: (layer, 0, j)))
    args.append(w)
    if gated:
        n_lat, seq, batch = dims
        mrow = functools.partial(_mod_row, rows_per_block=tm, n_lat=n_lat, seq=seq, batch=batch)
        in_specs.append(pl.BlockSpec((tm, tn), lambda j, i: (i, j)))
        in_specs.append(pl.BlockSpec((1, 1, tn), lambda j, i: (mrow(i) * 6 + k_gate, 0, j)))
        args += [resid, mods]
    return pl.pallas_call(
        functools.partial(_mm_body, rms=rms, gated=gated, k1=k, two_a=two_a),
        out_shape=jax.ShapeDtypeStruct((m, nw), out_dtype),
        grid=(nw // tn, m // tm),
        in_specs=in_specs,
        out_specs=pl.BlockSpec((tm, tn), lambda j, i: (i, j)),
        scratch_shapes=[pltpu.VMEM((kw_, tn), BF16)],
        compiler_params=_cparams(("arbitrary", "arbitrary")),
        name="matmul",
    )(*args)


def _rope_rot(pr, trig):
    if trig is not None:
        pr = pr * trig
    else:
        lane = lax.broadcasted_iota(jnp.int32, pr.shape, 1)
        pr = jnp.where(lane < D_ROPE, pr, 0.0)
    return pr + pltpu.roll(pr, D_ROPE, axis=1)


def _mla_body(*refs, n_lat_keys, n_ctx_keys, rope_q, scale, key_chunk, aliased):
    it = iter(refs)
    q_ref = next(it)
    kvl_ref = next(it) if n_lat_keys else None
    krl_ref = next(it) if n_lat_keys else None
    tkl_ref = next(it) if n_lat_keys else None
    kvc_ref = next(it)
    krc_ref = next(it)
    tq_ref = next(it) if rope_q else None
    gq_ref = next(it)
    gk_ref = next(it)
    if aliased:
        next(it)
    o_ref = next(it)
    kbuf = next(it)
    vbuf = next(it)
    dqk = D_NOPE + D_ROPE

    def prep_keys(kv_ref, kr_ref, trig_ref, off, n):
        gk = gk_ref[...]

        def chunk(c, carry):
            rows = pl.ds(pl.multiple_of(c * key_chunk, key_chunk), key_chunk)
            kv = kv_ref[rows, :]
            kn = kv[:, :D_NOPE].astype(F32)
            kr = kr_ref[rows, :].astype(F32)
            lane = lax.broadcasted_iota(jnp.int32, kr.shape, 1)
            ss = (jnp.sum(kn * kn, axis=-1, keepdims=True)
                  + jnp.sum(jnp.where(lane < D_ROPE, kr * kr, 0.0), axis=-1, keepdims=True))
            r = lax.rsqrt(ss / dqk + NORM_EPS)
            trig = trig_ref[rows, :] if trig_ref is not None else None
            rot = _rope_rot(kr * (r * gk[:, D_NOPE:]), trig)
            orow = pl.ds(pl.multiple_of(off + c * key_chunk, key_chunk), key_chunk)
            kbuf[orow, :D_NOPE] = (kn * (r * gk[:, :D_NOPE])).astype(BF16)
            kbuf[orow, D_NOPE:] = rot.astype(BF16)
            vbuf[orow, :] = kv[:, D_NOPE:].astype(BF16)
            return carry

        lax.fori_loop(0, n // key_chunk, chunk, 0)

    @pl.when(pl.program_id(2) == 0)
    def _():
        if n_lat_keys:
            prep_keys(kvl_ref, krl_ref, tkl_ref, 0, n_lat_keys)
        prep_keys(kvc_ref, krc_ref, None, n_lat_keys, n_ctx_keys)

    gq = gq_ref[...]
    for sb in range(q_ref.shape[0] // MLA_SUB_Q):
        rows = slice(sb * MLA_SUB_Q, (sb + 1) * MLA_SUB_Q)
        q = q_ref[rows, :].astype(F32)
        qn = q[:, :D_NOPE]
        qr = q[:, D_NOPE:]
        lane = lax.broadcasted_iota(jnp.int32, qr.shape, 1)
        ss = (jnp.sum(qn * qn, axis=-1, keepdims=True)
              + jnp.sum(jnp.where(lane < D_ROPE, qr * qr, 0.0), axis=-1, keepdims=True))
        r = lax.rsqrt(ss / dqk + NORM_EPS) * scale
        rot = _rope_rot(qr * (r * gq[:, D_NOPE:]), tq_ref[rows, :] if rope_q else None)
        rot = jnp.where(lane < D_ROPE, rot, 0.0)
        q2 = jnp.concatenate([qn * (r * gq[:, :D_NOPE]), rot], axis=1).astype(BF16)
        s = lax.dot_general(q2, kbuf[...], (((1,), (1,)), ((), ())), preferred_element_type=F32)
        m = jnp.max(s, axis=-1, keepdims=True)
        p = jnp.exp(s - m)
        l = jnp.sum(p, axis=-1, keepdims=True)
        o = jnp.dot(p.astype(BF16), vbuf[...], preferred_element_type=F32)
        o_ref[rows, :] = (o / l).astype(o_ref.dtype)


MLA_SUB_Q = 256


def mla_attention(q_all, kv_all, p_all, trig, gq, gk, *, batch, seq, ctx_len, latent_queries, out_buf=None):
    n_lat = batch * seq
    kr_blk = P_KR // 128
    scale = (D_NOPE + D_ROPE) ** -0.5
    if latent_queries:
        tq = 2 * MLA_SUB_Q
        nq, n_lat_keys, q_row0 = seq, seq, 0
    else:
        tq = ctx_len
        nq, n_lat_keys, q_row0 = ctx_len, 0, n_lat // tq
    assert tq % MLA_SUB_Q == 0 and nq % tq == 0
    nqb = nq // tq
    ctx_blk0 = n_lat // ctx_len
    in_specs = [pl.BlockSpec((tq, 256), lambda b, h, i: (q_row0 + b * nqb + i, h))]
    args = [q_all]
    if latent_queries:
        in_specs += [
            pl.BlockSpec((seq, 256), lambda b, h, i: (b, h)),
            pl.BlockSpec((seq, 128), lambda b, h, i: (b, kr_blk)),
            pl.BlockSpec((seq, 128), lambda b, h, i: (0, 0)),
        ]
        args += [kv_all, p_all, trig]
    in_specs += [
        pl.BlockSpec((ctx_len, 256), lambda b, h, i: (ctx_blk0 + b, h)),
        pl.BlockSpec((ctx_len, 128), lambda b, h, i: (ctx_blk0 + b, kr_blk)),
    ]
    args += [kv_all, p_all]
    if latent_queries:
        in_specs.append(pl.BlockSpec((tq, 128), lambda b, h, i: (i, 0)))
        args.append(trig)
    in_specs += [pl.BlockSpec((1, 256), lambda b, h, i: (0, 0))] * 2
    args += [gq, gk]
    aliases = {}
    if out_buf is not None:
        in_specs.append(pl.BlockSpec(memory_space=pl.ANY))
        aliases = {len(args): 0}
        args.append(out_buf)
    nk = n_lat_keys + ctx_len
    body = functools.partial(_mla_body, n_lat_keys=n_lat_keys, n_ctx_keys=ctx_len, rope_q=latent_queries,
                             scale=scale, key_chunk=min(256, ctx_len), aliased=out_buf is not None)
    return pl.pallas_call(
        body,
        out_shape=jax.ShapeDtypeStruct((q_all.shape[0], W_A), BF16),
        grid=(batch, H_A, nqb),
        in_specs=in_specs,
        out_specs=pl.BlockSpec((tq, D_V_A), lambda b, h, i: (q_row0 + b * nqb + i, h)),
        scratch_shapes=[pltpu.VMEM((nk, 256), BF16), pltpu.VMEM((nk, D_V_A), BF16)],
        input_output_aliases=aliases,
        compiler_params=_cparams(("arbitrary", "arbitrary", "arbitrary")),
        name="mla_attention",
    )(*args)


def _conv_silu_body(x_ref, prev_ref, next_ref, w_ref, o_ref, *, lat_blocks, seq_blocks, ctx_blocks):
    i = pl.program_id(0)
    j = jnp.where(i < lat_blocks, i % seq_blocks, (i - lat_blocks) % ctx_blocks)
    nb = jnp.where(i < lat_blocks, seq_blocks, ctx_blocks)
    x = x_ref[...]
    rows = x.shape[0]
    row = lax.broadcasted_iota(jnp.int32, x.shape, 0)
    prev_row = jnp.where(j == 0, 0.0, prev_ref[7:8, :])
    next_row = jnp.where(j == nb - 1, 0.0, next_ref[0:1, :])
    xp = jnp.where(row == 0, prev_row, pltpu.roll(x, 1, axis=0))
    xn = jnp.where(row == rows - 1, next_row, pltpu.roll(x, rows - 1, axis=0))
    w = w_ref[...]
    y = xp * w[0:1, :] + x * w[1:2, :] + xn * w[2:3, :]
    y = y * jax.nn.sigmoid(y)
    half = y.shape[1] // 2
    o_ref[:, :half] = y[:, :half].astype(o_ref.dtype)
    o_ref[:, half:] = (y[:, half:] * (DQK_B ** -0.5)).astype(o_ref.dtype)


def conv_silu(p_all, conv_w, *, n_lat, seq, ctx_len):
    n = p_all.shape[0]
    width = 2 * H_B * DQK_B
    rb = ROW_BLOCK
    nblk = n // rb
    sub = rb // 8
    last8 = n // 8 - 1
    body = functools.partial(_conv_silu_body, lat_blocks=n_lat // rb, seq_blocks=seq // rb,
                             ctx_blocks=ctx_len // rb)
    return pl.pallas_call(
        body,
        out_shape=jax.ShapeDtypeStruct((n, width), BF16),
        grid=(nblk,),
        in_specs=[
            pl.BlockSpec((rb, width), lambda i: (i, P_QK // width)),
            pl.BlockSpec((8, width), lambda i: (jnp.maximum(i * sub - 1, 0), P_QK // width)),
            pl.BlockSpec((8, width), lambda i: (jnp.minimum((i + 1) * sub, last8), P_QK // width)),
            pl.BlockSpec((3, width), lambda i: (0, 0)),
        ],
        out_specs=pl.BlockSpec((rb, width), lambda i: (i, 0)),
        compiler_params=_cparams(("parallel",)),
        name="conv_silu",
    )(p_all, p_all, p_all, conv_w)


def _log_sigmoid(x):
    return jnp.minimum(x, 0.0) - jnp.log(1.0 + jnp.exp(-jnp.abs(x)))


def _mlstm_body(q_ref, k_ref, v_ref, gc_ref, gr_ref, bc_ref, br_ref, o_ref, c_st, n_st, m_st):
    h = pl.program_id(1)
    direction = pl.program_id(2)
    c = pl.program_id(3)
    L = MLSTM_CHUNK
    hi = lax.Precision.HIGHEST

    @pl.when(c == 0)
    def _():
        c_st[...] = jnp.zeros_like(c_st)
        n_st[...] = jnp.zeros_like(n_st)
        m_st[...] = jnp.full_like(m_st, M_INIT)

    gi = direction * (2 * H_B) + h
    gcol = gc_ref[...] + bc_ref[...]
    lane = lax.broadcasted_iota(jnp.int32, gcol.shape, 1)
    li_c = jnp.sum(jnp.where(lane == gi, gcol, 0.0), axis=1, keepdims=True)
    lf_c = _log_sigmoid(jnp.sum(jnp.where(lane == gi + H_B, gcol, 0.0), axis=1, keepdims=True))
    grow = gr_ref[...] + br_ref[...]
    sub = lax.broadcasted_iota(jnp.int32, grow.shape, 0)
    li_r = jnp.sum(jnp.where(sub == gi, grow, 0.0), axis=0, keepdims=True)
    lf_r = _log_sigmoid(jnp.sum(jnp.where(sub == gi + H_B, grow, 0.0), axis=0, keepdims=True))

    t_i = lax.broadcasted_iota(jnp.int32, (L, L), 0)
    s_i = lax.broadcasted_iota(jnp.int32, (L, L), 1)
    sgn = 1 - 2 * direction
    incl = (s_i - t_i) * sgn <= 0
    incl_f = incl.astype(F32)
    incl_t = ((t_i - s_i) * sgn <= 0).astype(F32)
    bcum_c = jnp.dot(incl_f, jnp.broadcast_to(lf_c, (L, L)), precision=hi, preferred_element_type=F32)
    bcum_r = jnp.dot(jnp.broadcast_to(lf_r, (8, L)), incl_t, precision=hi, preferred_element_type=F32)[0:1, :]
    b_last = jnp.sum(lf_r, axis=1, keepdims=True)

    m_prev = m_st[...]
    a_c = bcum_c[:, 0:1] + m_prev
    dlog = jnp.where(incl, bcum_c - bcum_r + li_r, -jnp.inf)
    m_t = jnp.maximum(a_c, jnp.max(dlog, axis=1, keepdims=True))
    w_intra = jnp.exp(dlog - m_t)
    w_inter = jnp.exp(a_c - m_t)

    q = q_ref[...]
    k = k_ref[...]
    v = v_ref[...].astype(BF16)
    c_prev = c_st[...]
    n_prev = n_st[...]
    s = lax.dot_general(q, k, (((1,), (1,)), ((), ())), preferred_element_type=F32) * w_intra
    num = (w_inter * jnp.dot(q, c_prev.astype(BF16), preferred_element_type=F32)
           + jnp.dot(s.astype(BF16), v, preferred_element_type=F32))
    den = (w_inter * jnp.sum(q.astype(F32) * n_prev, axis=1, keepdims=True)
           + jnp.sum(s, axis=1, keepdims=True))
    o_ref[0] = num / jnp.maximum(jnp.abs(den), jnp.exp(-m_t))

    g_r = b_last - bcum_r + li_r
    g_c = b_last - bcum_c[:, 0:1] + li_c
    m_new = jnp.maximum(b_last + m_prev, jnp.max(g_r, axis=1, keepdims=True))
    decay = jnp.exp(b_last + m_prev - m_new)
    kw = k.astype(F32) * jnp.exp(g_c - m_new)
    c_st[...] = decay * c_prev + lax.dot_general(kw.astype(BF16), v, (((0,), (0,)), ((), ())),
                                                 preferred_element_type=F32)
    n_st[...] = decay * n_prev + jnp.sum(kw, axis=0, keepdims=True)
    m_st[...] = m_new


def mlstm_scan(qk, p_all, gates_t, gate_b, *, batch, seq, ctx_len):
    n = qk.shape[0]
    L = MLSTM_CHUNK
    cc, lc = ctx_len // L, seq // L
    n_lat_blk = batch * lc

    def rowblk(b, d, c):
        ctx_blk = n_lat_blk + b * cc + jnp.where(d == 0, c, cc - 1 - c)
        lat_blk = b * lc + jnp.where(d == 0, c - cc, lc - 1 - (c - cc))
        return jnp.where(c < cc, ctx_blk, lat_blk)

    gb_col = jnp.zeros((1, 128), F32).at[0, :4 * H_B].set(gate_b)
    gb_row = gate_b.reshape(4 * H_B, 1)
    return pl.pallas_call(
        _mlstm_body,
        out_shape=jax.ShapeDtypeStruct((2, n, W_B), F32),
        grid=(batch, H_B, 2, cc + lc),
        in_specs=[
            pl.BlockSpec((L, DQK_B), lambda b, h, d, c: (rowblk(b, d, c), h)),
            pl.BlockSpec((L, DQK_B), lambda b, h, d, c: (rowblk(b, d, c), H_B + h)),
            pl.BlockSpec((L, DV_B), lambda b, h, d, c: (rowblk(b, d, c), P_V // DV_B + h)),
            pl.BlockSpec((L, 128), lambda b, h, d, c: (rowblk(b, d, c), P_GATE // 128)),
            pl.BlockSpec((4 * H_B, L), lambda b, h, d, c: (0, rowblk(b, d, c))),
            pl.BlockSpec((1, 128), lambda b, h, d, c: (0, 0)),
            pl.BlockSpec((4 * H_B, 1), lambda b, h, d, c: (0, 0)),
        ],
        out_specs=pl.BlockSpec((1, L, DV_B), lambda b, h, d, c: (d, rowblk(b, d, c), h)),
        scratch_shapes=[pltpu.VMEM((DQK_B, DV_B), F32), pltpu.VMEM((1, DQK_B), F32), pltpu.VMEM((1, 1), F32)],
        compiler_params=_cparams(("arbitrary",) * 4),
        name="mlstm_scan",
    )(qk, qk, p_all, p_all, gates_t, gb_col, gb_row)


def _mlstm_finish_body(h_ref, o_ref, g_ref, out_ref):
    hs = h_ref[0] + h_ref[1]
    g = g_ref[...]
    og = o_ref[...]
    for hd in range(H_B):
        sl = slice(hd * DV_B, (hd + 1) * DV_B)
        x = hs[:, sl]
        ms = jnp.mean(x * x, axis=-1, keepdims=True)
        hn = x * lax.rsqrt(ms + NORM_EPS) * g[:, sl]
        out_ref[:, sl] = (jax.nn.sigmoid(og[:, sl]) * hn).astype(out_ref.dtype)


def mlstm_finish(hdir, p_all, out_norm_g):
    n = hdir.shape[1]
    rb = ROW_BLOCK
    return pl.pallas_call(
        _mlstm_finish_body,
        out_shape=jax.ShapeDtypeStruct((n, W_B), BF16),
        grid=(n // rb,),
        in_specs=[
            pl.BlockSpec((2, rb, W_B), lambda i: (0, i, 0)),
            pl.BlockSpec((rb, W_B), lambda i: (i, P_O // W_B)),
            pl.BlockSpec((1, W_B), lambda i: (0, 0)),
        ],
        out_specs=pl.BlockSpec((rb, W_B), lambda i: (i, 0)),
        compiler_params=_cparams(("parallel",)),
        name="mlstm_finish",
    )(hdir, p_all, out_norm_g.reshape(1, W_B))


NA_ROWS_PER_STEP = 4
NA_TILES_PER_STEP = 2


def _na_geometry(rows_n):
    r_t = NA_ROWS_PER_STEP
    kh = min(NA_KH, rows_n)
    win = min(r_t + kh - 1, rows_n)
    tiles = rows_n // r_t
    starts, sigs, cls = [], [], []
    for t in range(tiles):
        rq0 = t * r_t
        start = int(np.clip(rq0 - kh // 2, 0, rows_n - win))
        r0 = np.clip(np.arange(rq0, rq0 + r_t) - kh // 2, 0, rows_n - kh)
        sig = (start - rq0, tuple((r0 - rq0).tolist()))
        if sig not in sigs:
            sigs.append(sig)
        starts.append(start)
        cls.append(sigs.index(sig))
    return kh, win, np.array(starts, np.int32), np.array(cls, np.int32), sigs


def na_bias_tables(rpb, rows_n):
    r_t = NA_ROWS_PER_STEP
    kh, win, _, _, sigs = _na_geometry(rows_n)
    hi = lax.Precision.HIGHEST
    cols = np.arange(GRID_W)
    c0 = np.clip(cols - NA_KW // 2, 0, GRID_W - NA_KW)
    col_valid = (cols[None, :] >= c0[:, None]) & (cols[None, :] < c0[:, None] + NA_KW)
    col_off = cols[None, :] - cols[:, None] + (NA_KW - 1)
    c_sel = (col_valid[:, :, None] & (col_off[:, :, None] == np.arange(2 * NA_KW - 1))).astype(np.float32)
    r_sel = np.zeros((len(sigs), r_t, win, 2 * NA_KH - 1), np.float32)
    valid = np.zeros((len(sigs), r_t, GRID_W, win, GRID_W), bool)
    for ci, (dstart, dr0) in enumerate(sigs):
        a = np.arange(r_t)[:, None]
        j = np.arange(win)[None, :]
        krow = dstart + j
        r0 = np.array(dr0)[:, None]
        row_valid = (krow >= r0) & (krow < r0 + kh)
        row_off = krow - a + (NA_KH - 1)
        r_sel[ci] = row_valid[:, :, None] & (row_off[:, :, None] == np.arange(2 * NA_KH - 1))
        valid[ci] = row_valid[:, None, :, None] & col_valid[None, :, None, :]
    t = jnp.einsum('hrs,cds->hrcd', rpb, c_sel, precision=hi)
    b = jnp.einsum('hrcd,kajr->hkacjd', t, r_sel, precision=hi)
    b = b + np.where(valid, 0.0, NEG).astype(np.float32)[None]
    return b.reshape(H_C, len(sigs), r_t * GRID_W, win * GRID_W)


def _head_rmsnorm(x, g):
    xf = x.astype(F32)
    ms = jnp.mean(xf * xf, axis=-1, keepdims=True)
    return xf * lax.rsqrt(ms + NORM_EPS) * g


def _na_body(start_ref, cls_ref, q_ref, k_ref, v_ref, kc_ref, vc_ref, *rest, seq, ctx_len, win_rows, scale):
    bias_refs = rest[:NA_TILES_PER_STEP]
    gq_ref, gk_ref = rest[NA_TILES_PER_STEP:NA_TILES_PER_STEP + 2]
    o_ref, kn, knc = rest[-3:]
    i = pl.program_id(2)
    chunk = min(512, seq)

    @pl.when(i == 0)
    def _():
        gk = gk_ref[...]

        def body(c, carry):
            rows = pl.ds(pl.multiple_of(c * chunk, chunk), chunk)
            kn[rows, :] = _head_rmsnorm(k_ref[rows, :], gk).astype(BF16)
            return carry

        lax.fori_loop(0, seq // chunk, body, 0)
        knc[...] = _head_rmsnorm(kc_ref[...], gk).astype(BF16)

    nt = (((1,), (1,)), ((), ()))
    tq = NA_ROWS_PER_STEP * GRID_W
    for tl in range(NA_TILES_PER_STEP):
        rows = slice(tl * tq, (tl + 1) * tq)
        q = (_head_rmsnorm(q_ref[rows, :], gq_ref[...]) * scale).astype(BF16)
        krow0 = pl.multiple_of(start_ref[i * NA_TILES_PER_STEP + tl] * GRID_W, GRID_W)
        kw = kn[pl.ds(krow0, win_rows), :]
        vw = v_ref[pl.ds(krow0, win_rows), :]
        s = lax.dot_general(q, kw, nt, preferred_element_type=F32) + bias_refs[tl][0, 0]
        sc = lax.dot_general(q, knc[...], nt, preferred_element_type=F32)
        m = jnp.maximum(jnp.max(s, axis=-1, keepdims=True), jnp.max(sc, axis=-1, keepdims=True))
        p = jnp.exp(s - m)
        pc = jnp.exp(sc - m)
        l = jnp.sum(p, axis=-1, keepdims=True) + jnp.sum(pc, axis=-1, keepdims=True)
        o = (jnp.dot(p.astype(BF16), vw, preferred_element_type=F32)
             + jnp.dot(pc.astype(BF16), vc_ref[...], preferred_element_type=F32))
        o_ref[rows, :] = (o / l).astype(o_ref.dtype)


def na_attention(qkv, bias_tabs, gq, gk, *, batch, seq, ctx_len, out_buf=None):
    rows_n = seq // GRID_W
    r_t = NA_ROWS_PER_STEP
    _, win, starts, cls, _ = _na_geometry(rows_n)
    tps = NA_TILES_PER_STEP
    tq = tps * r_t * GRID_W
    tiles = rows_n // (r_t * tps)
    assert rows_n % (r_t * tps) == 0
    n_lat = batch * seq
    ctx_blk0 = n_lat // ctx_len
    body = functools.partial(_na_body, seq=seq, ctx_len=ctx_len, win_rows=win * GRID_W, scale=DH_C ** -0.5)

    def bias_spec(tl):
        return pl.BlockSpec((1, 1, r_t * GRID_W, win * GRID_W),
                            lambda b, h, i, st, cl: (h, cl[i * tps + tl], 0, 0))

    in_specs = [
        pl.BlockSpec((tq, DH_C), lambda b, h, i, st, cl: (b * tiles + i, h)),
        pl.BlockSpec((seq, DH_C), lambda b, h, i, st, cl: (b, H_C + h)),
        pl.BlockSpec((seq, DH_C), lambda b, h, i, st, cl: (b, 2 * H_C + h)),
        pl.BlockSpec((ctx_len, DH_C), lambda b, h, i, st, cl: (ctx_blk0 + b, H_C + h)),
        pl.BlockSpec((ctx_len, DH_C), lambda b, h, i, st, cl: (ctx_blk0 + b, 2 * H_C + h)),
    ] + [bias_spec(tl) for tl in range(tps)] + [
        pl.BlockSpec((1, DH_C), lambda b, h, i, st, cl: (0, 0)),
        pl.BlockSpec((1, DH_C), lambda b, h, i, st, cl: (0, 0)),
    ]
    args = [jnp.asarray(starts), jnp.asarray(cls), qkv, qkv, qkv, qkv, qkv] + [bias_tabs] * tps + [gq, gk]
    aliases = {}
    out_rows = n_lat
    if out_buf is not None:
        in_specs.append(pl.BlockSpec(memory_space=pl.ANY))
        aliases = {len(args): 0}
        args.append(out_buf)
        out_rows = out_buf.shape[0]
    grid_spec = pltpu.PrefetchScalarGridSpec(
        num_scalar_prefetch=2,
        grid=(batch, H_C, tiles),
        in_specs=in_specs,
        out_specs=pl.BlockSpec((tq, DH_C), lambda b, h, i, st, cl: (b * tiles + i, h)),
        scratch_shapes=[pltpu.VMEM((seq, DH_C), BF16), pltpu.VMEM((ctx_len, DH_C), BF16)],
    )
    return pl.pallas_call(
        body,
        out_shape=jax.ShapeDtypeStruct((out_rows, H_C * DH_C), BF16),
        grid_spec=grid_spec,
        input_output_aliases=aliases,
        compiler_params=_cparams(("arbitrary", "arbitrary", "arbitrary")),
        name="na_attention",
    )(*args)


def _ctx_attn_body(q_ref, k_ref, v_ref, gq_ref, gk_ref, buf_ref, o_ref, *, scale):
    del buf_ref
    q = (_head_rmsnorm(q_ref[...], gq_ref[...]) * scale).astype(BF16)
    k = _head_rmsnorm(k_ref[...], gk_ref[...]).astype(BF16)
    s = lax.dot_general(q, k, (((1,), (1,)), ((), ())), preferred_element_type=F32)
    m = jnp.max(s, axis=-1, keepdims=True)
    p = jnp.exp(s - m)
    l = jnp.sum(p, axis=-1, keepdims=True)
    o = jnp.dot(p.astype(BF16), v_ref[...], preferred_element_type=F32)
    o_ref[...] = (o / l).astype(o_ref.dtype)


def na_ctx_attention(qkv, gq, gk, out_buf, *, batch, seq, ctx_len):
    ctx_blk0 = batch * seq // ctx_len
    return pl.pallas_call(
        functools.partial(_ctx_attn_body, scale=DH_C ** -0.5),
        out_shape=jax.ShapeDtypeStruct(out_buf.shape, out_buf.dtype),
        grid=(batch, H_C),
        in_specs=[
            pl.BlockSpec((ctx_len, DH_C), lambda b, h: (ctx_blk0 + b, h)),
            pl.BlockSpec((ctx_len, DH_C), lambda b, h: (ctx_blk0 + b, H_C + h)),
            pl.BlockSpec((ctx_len, DH_C), lambda b, h: (ctx_blk0 + b, 2 * H_C + h)),
            pl.BlockSpec((1, DH_C), lambda b, h: (0, 0)),
            pl.BlockSpec((1, DH_C), lambda b, h: (0, 0)),
            pl.BlockSpec(memory_space=pl.ANY),
        ],
        out_specs=pl.BlockSpec((ctx_len, DH_C), lambda b, h: (ctx_blk0 + b, h)),
        input_output_aliases={5: 0},
        compiler_params=_cparams(("parallel", "parallel")),
        name="na_ctx_attention",
    )(qkv, qkv, qkv, gq, gk, out_buf)


def _slab_copy(src_ref, src_tok, dst_ref, dst_tok, sem):
    def rows(tok):
        off = tok * SLAB_STRIDE
        return pl.ds(off if isinstance(off, int) else pl.multiple_of(off, 8), SLAB_ROWS)

    return pltpu.make_async_copy(src_ref.at[rows(src_tok), :], dst_ref.at[rows(dst_tok), :], sem)


def _load_slabs(ref, rows, dtype):
    return jnp.concatenate([ref[pl.ds(j, rows, stride=SLAB_STRIDE), :].astype(dtype) for j in range(SLAB_ROWS)],
                           axis=1)


def moe_slot_plan(route, counts, n_tiles):
    cnt = counts[0, :N_EXPERTS].astype(jnp.int32)
    padded = (cnt + MOE_TILE - 1) // MOE_TILE * MOE_TILE
    ends = jnp.cumsum(padded)
    base = ends - padded
    n_used = ends[-1] // MOE_TILE
    tile_start = jnp.arange(n_tiles, dtype=jnp.int32) * MOE_TILE
    last_start = jnp.maximum(ends[-1] - MOE_TILE, 0)
    tile_expert = jnp.searchsorted(ends, jnp.minimum(tile_start, last_start), side='right').astype(jnp.int32)
    tile_expert = jnp.minimum(tile_expert, N_EXPERTS - 1)
    ids = route[:, 0:2].astype(jnp.int32)
    pos = jnp.take(base, ids, axis=0) + route[:, 4:6].astype(jnp.int32)
    return pos[:, 0], pos[:, 1], tile_expert, n_used.reshape(1).astype(jnp.int32)


def _slot_source_body(p1_ref, p2_ref, src_ref, *, n_tok):
    def clear(i, c):
        src_ref[i] = 0
        return c

    lax.fori_loop(0, src_ref.shape[0], clear, 0, unroll=8)

    def fill(n, c):
        src_ref[p1_ref[n]] = n
        src_ref[p2_ref[n]] = n
        return c

    lax.fori_loop(0, n_tok, fill, 0, unroll=4)


def moe_slot_source(pos1, pos2, n_slots):
    smem = pl.BlockSpec(memory_space=pltpu.SMEM)
    return pl.pallas_call(
        functools.partial(_slot_source_body, n_tok=pos1.shape[0]),
        out_shape=jax.ShapeDtypeStruct((n_slots,), jnp.int32),
        in_specs=[smem, smem],
        out_specs=smem,
        name="moe_slot_source",
    )(pos1, pos2)


def _gather_slabs(src_ref, tok_ref, tok0, dst_ref, sem, n):
    def issue(r, c):
        _slab_copy(src_ref, tok_ref[tok0 + r], dst_ref, r, sem).start()
        return c

    lax.fori_loop(0, n, issue, 0, unroll=8)


def _wait_slabs(src_ref, dst_ref, sem, n):
    rows = pl.ds(0, n * SLAB_ROWS)
    pltpu.make_async_copy(src_ref.at[rows, :], dst_ref.at[rows, :], sem).wait()


def _moe_expert_body(te_ref, nu_ref, src_ref, hs_ref, wg_ref, wu_ref, wd_ref, o_ref, xbuf, wg_bf, wu_bf, wd_bf,
                     sems):
    t = pl.program_id(0)
    tm = MOE_TILE
    slot = t % 2

    @pl.when(t < nu_ref[0])
    def _():
        @pl.when(t == 0)
        def _():
            _gather_slabs(hs_ref, src_ref, 0, xbuf.at[0], sems.at[0], tm)

        @pl.when(t + 1 < nu_ref[0])
        def _():
            _gather_slabs(hs_ref, src_ref, (t + 1) * tm, xbuf.at[1 - slot], sems.at[1 - slot], tm)

        @pl.when((t == 0) | (te_ref[t] != te_ref[jnp.maximum(t - 1, 0)]))
        def _():
            wg_bf[...] = wg_ref[...].astype(BF16)
            wu_bf[...] = wu_ref[...].astype(BF16)
            wd_bf[...] = wd_ref[...].astype(BF16)

        _wait_slabs(hs_ref, xbuf.at[slot], sems.at[slot], tm)
        x = _load_slabs(xbuf.at[slot], tm, BF16)
        nt = (((1,), (1,)), ((), ()))
        a = lax.dot_general(x, wg_bf[...], nt, preferred_element_type=F32)
        u = lax.dot_general(x, wu_bf[...], nt, preferred_element_type=F32)
        y = (a * jax.nn.sigmoid(a) * u).astype(BF16)
        _store_slabs(o_ref, jnp.dot(y, wd_bf[...], preferred_element_type=F32), tm)

    @pl.when(t >= nu_ref[0])
    def _():
        o_ref[...] = jnp.zeros_like(o_ref)


def moe_experts(hs, src, tile_expert, n_used, w_gate_t, w_up_t, w_down, layer):
    n_tiles = tile_expert.shape[0]
    f, d = w_gate_t.shape[-2:]
    blk = MOE_TILE * SLAB_STRIDE
    wspec = pl.BlockSpec((None, None, f, d), lambda t, te, nu, src: (layer, te[t], 0, 0))
    grid_spec = pltpu.PrefetchScalarGridSpec(
        num_scalar_prefetch=3,
        grid=(n_tiles,),
        in_specs=[pl.BlockSpec(memory_space=pl.ANY), wspec, wspec, wspec],
        out_specs=pl.BlockSpec((blk, 128), lambda t, te, nu, src: (t, 0)),
        scratch_shapes=[pltpu.VMEM((2, blk, 128), F32), pltpu.VMEM((f, d), BF16), pltpu.VMEM((f, d), BF16),
                        pltpu.VMEM((f, d), BF16), pltpu.SemaphoreType.DMA((2,))],
    )
    return pl.pallas_call(
        _moe_expert_body,
        out_shape=jax.ShapeDtypeStruct((n_tiles * blk, 128), F32),
        grid_spec=grid_spec,
        compiler_params=_dma_cparams(("arbitrary",)),
        name="moe_experts",
    )(tile_expert, n_used, src, hs, w_gate_t, w_up_t, w_down)


def _moe_combine_body(p1_ref, p2_ref, ys_ref, route_ref, res_ref, gate_ref, o_ref, abuf, bbuf, sems):
    i = pl.program_id(0)
    rb = ROW_BLOCK
    slot = i % 2

    def gather(blk, s):
        _gather_slabs(ys_ref, p1_ref, blk * rb, abuf.at[s], sems.at[0, s], rb)
        _gather_slabs(ys_ref, p2_ref, blk * rb, bbuf.at[s], sems.at[1, s], rb)

    @pl.when(i == 0)
    def _():
        gather(0, 0)

    @pl.when(i + 1 < pl.num_programs(0))
    def _():
        gather(i + 1, 1 - slot)

    route = route_ref[...]
    lane = lax.broadcasted_iota(jnp.int32, route.shape, 1)
    w1 = jnp.sum(jnp.where(lane == 2, route, 0.0), axis=1, keepdims=True)
    w2 = jnp.sum(jnp.where(lane == 3, route, 0.0), axis=1, keepdims=True)
    _wait_slabs(ys_ref, abuf.at[slot], sems.at[0, slot], rb)
    _wait_slabs(ys_ref, bbuf.at[slot], sems.at[1, slot], rb)
    for j in range(SLAB_ROWS):
        sl = slice(j * 128, (j + 1) * 128)
        rows = pl.ds(j, rb, stride=SLAB_STRIDE)
        mix = w1 * abuf[slot, rows, :] + w2 * bbuf[slot, rows, :]
        o_ref[:, sl] = res_ref[:, sl] + gate_ref[0][:, sl] * mix


def moe_combine(ys, pos1, pos2, route, resid, mods, k_gate, dims):
    n, d = route.shape[0], resid.shape[1]
    n_lat, seq, batch = dims
    rb = ROW_BLOCK
    mrow = functools.partial(_mod_row, rows_per_block=rb, n_lat=n_lat, seq=seq, batch=batch)
    grid_spec = pltpu.PrefetchScalarGridSpec(
        num_scalar_prefetch=2,
        grid=(n // rb,),
        in_specs=[
            pl.BlockSpec(memory_space=pl.ANY),
            pl.BlockSpec((rb, 128), lambda i, p1, p2: (i, 0)),
            pl.BlockSpec((rb, d), lambda i, p1, p2: (i, 0)),
            pl.BlockSpec((1, 1, d), lambda i, p1, p2: (mrow(i) * 6 + k_gate, 0, 0)),
        ],
        out_specs=pl.BlockSpec((rb, d), lambda i, p1, p2: (i, 0)),
        scratch_shapes=[pltpu.VMEM((2, rb * SLAB_STRIDE, 128), F32), pltpu.VMEM((2, rb * SLAB_STRIDE, 128), F32),
                        pltpu.SemaphoreType.DMA((2, 2))],
    )
    return pl.pallas_call(
        _moe_combine_body,
        out_shape=jax.ShapeDtypeStruct((n, d), F32),
        grid_spec=grid_spec,
        compiler_params=_dma_cparams(("arbitrary",)),
        name="moe_combine",
    )(pos1, pos2, ys, route, resid, mods)


def _swap16(a):
    lead = a.shape[:-1]
    return jnp.flip(a.reshape(*lead, D_ROPE // 32, 2, 16), axis=-2).reshape(*lead, D_ROPE)


def _ext_w_in(w):
    c_kr = Q_LORA + KV_LORA
    c_qb = c_kr + D_ROPE
    c_g = c_qb + 2 * H_B * DQK_B + 2 * W_B
    kr = w[:, c_kr:c_qb]
    used = (c_g - c_qb) + c_kr + 2 * D_ROPE
    gates = jnp.pad(w[:, c_g:], ((0, 0), (0, P_WIDTH - used - 4 * H_B)))
    return jnp.concatenate([w[:, c_qb:c_g], w[:, :c_kr], kr, _swap16(kr), gates], axis=1)


def _ext_w_q_up(w):
    w3 = w.reshape(Q_LORA, H_A, D_NOPE + D_ROPE)
    return jnp.concatenate([w3, _swap16(w3[:, :, D_NOPE:])], axis=2).reshape(Q_LORA, H_A * 256)


def _ext_gain(g):
    return jnp.concatenate([g, _swap16(g[D_NOPE:])]).reshape(1, 256)


def _rope_tables(seq):
    t = jnp.arange(seq)
    half = D_ROPE // 4
    inv = ROPE_THETA ** (-jnp.arange(half, dtype=F32) / half)
    ar = (t // GRID_W).astype(F32)[:, None] * inv[None, :]
    ac = (t % GRID_W).astype(F32)[:, None] * inv[None, :]
    cr, sr, cc, sc = jnp.cos(ar), jnp.sin(ar), jnp.cos(ac), jnp.sin(ac)
    return jnp.concatenate([cr, cr, cc, cc, -sr, sr, -sc, sc], axis=1)


def kernel(x, c, ctx, c_ctx, ada_w, ada_b, norm1_g, norm2_g, ab_w_in, mla_q_norm_g, mla_kv_norm_g, mla_w_q_up, mla_w_kv_up, mla_qn_g, mla_kn_g, mlstm_conv_w, mlstm_gate_b, mlstm_out_norm_g, ab_w_out, na_w_qkv, na_qn_g, na_kn_g, na_rpb, na_w_out, moe_w_rg, moe_b_rg, moe_w_re, moe_b_re, moe_w_gate, moe_w_up, moe_w_down):
    batch, seq, d = x.shape
    ctx_len = ctx.shape[1]
    depth = ada_w.shape[0]
    n_lat = batch * seq
    dims = (n_lat, seq, batch)
    assert d == D_MODEL and batch + 1 <= MOD_ROWS and seq % 512 == 0 and ctx_len == ROW_BLOCK

    xs = jnp.concatenate([x.reshape(n_lat, d), ctx.reshape(batch * ctx_len, d)], axis=0)
    cvec = jnp.concatenate([c, c_ctx[None, :]], axis=0)
    mods_all = ada_mods(cvec, ada_w, ada_b).reshape(depth, MOD_ROWS * 6, 1, d)
    trig = _rope_tables(seq)

    w_gate_t, w_up_t = jnp.swapaxes(moe_w_gate, 2, 3), jnp.swapaxes(moe_w_up, 2, 3)
    kw = dict(batch=batch, seq=seq, ctx_len=ctx_len)
    for layer in range(depth):
        j = layer // 2
        last = layer == depth - 1
        mods = mods_all[layer]
        h = norm_mod(xs, norm1_g[layer], mods, (0, 1), dims)
        if layer % 2 == 0:
            p_all = matmul(h, _ext_w_in(ab_w_in[j]), out_dtype=F32)
            q_all = matmul(p_all, _ext_w_q_up(mla_w_q_up[j]), out_dtype=BF16, a_col=P_CQ // Q_LORA, k=Q_LORA,
                           rms_gain=mla_q_norm_g[j])
            kv_all = matmul(p_all, mla_w_kv_up, layer=j, out_dtype=BF16, a_col=P_CKV // KV_LORA, k=KV_LORA,
                            rms_gain=mla_kv_norm_g[j])
            gq, gk = _ext_gain(mla_qn_g[j]), _ext_gain(mla_kn_g[j])
            mix_a = jnp.zeros((xs.shape[0], W_A), BF16)
            mix_a = mla_attention(q_all, kv_all, p_all, trig, gq, gk, latent_queries=True, out_buf=mix_a, **kw)
            mix_a = mla_attention(q_all, kv_all, p_all, trig, gq, gk, latent_queries=False, out_buf=mix_a, **kw)
            qk = conv_silu(p_all, mlstm_conv_w[j], n_lat=n_lat, seq=seq, ctx_len=ctx_len)
            gates_t = p_all[:, P_GATE:P_GATE + 4 * H_B].T
            hdir = mlstm_scan(qk, p_all, gates_t, mlstm_gate_b[j], **kw)
            mix_b = mlstm_finish(hdir, p_all, mlstm_out_norm_g[j])
            xs = matmul(mix_a, ab_w_out, layer=j, a2=mix_b, out_dtype=F32, resid=xs, mods=mods, k_gate=2,
                        dims=dims)
        else:
            qkv = matmul(h, na_w_qkv, layer=j, out_dtype=BF16)
            gq, gk = na_qn_g[j].reshape(1, DH_C), na_kn_g[j].reshape(1, DH_C)
            bias = na_bias_tables(na_rpb[j], seq // GRID_W)
            if last:
                mixed = na_attention(qkv, bias, gq, gk, **kw)
            else:
                mixed = jnp.zeros((xs.shape[0], H_C * DH_C), BF16)
                mixed = na_attention(qkv, bias, gq, gk, out_buf=mixed, **kw)
                mixed = na_ctx_attention(qkv, gq, gk, mixed, **kw)
            xs = matmul(mixed, na_w_out, layer=j, out_dtype=F32, resid=xs, mods=mods, k_gate=2, dims=dims)

        w_r = jnp.pad(jnp.concatenate([moe_w_re[layer], moe_w_rg[layer]], axis=1),
                      ((0, 0), (0, 128 - N_EXPERTS - N_GROUPS)))
        b_r = jnp.pad(jnp.concatenate([moe_b_re[layer], moe_b_rg[layer]]), (0, 128 - N_EXPERTS - N_GROUPS))
        hs, route, counts = norm_router(xs, norm2_g[layer], mods, (3, 4), dims, w_r, b_r.reshape(1, 128))
        n_tiles = 2 * xs.shape[0] // MOE_TILE + N_EXPERTS
        pos1, pos2, tile_expert, n_used = moe_slot_plan(route, counts, n_tiles)
        src = moe_slot_source(pos1, pos2, n_tiles * MOE_TILE)
        ys = moe_experts(hs, src, tile_expert, n_used, w_gate_t, w_up_t, moe_w_down, layer)
        xs = moe_combine(ys, pos1, pos2, route, xs, mods, 5, dims)
    return xs[:n_lat].reshape(batch, seq, d) if xs.shape[0] != n_lat else xs.reshape(batch, seq, d)
```

```python
import functools

import numpy as np
import jax
import jax.numpy as jnp
from jax import lax
from jax.experimental import pallas as pl
from jax.experimental.pallas import tpu as pltpu

F32 = jnp.float32
BF16 = jnp.bfloat16

D_MODEL = 4096
GRID_W = 64
H_A = 16
Q_LORA = 1024
KV_LORA = 512
D_NOPE = 128
D_ROPE = 64
D_V_A = 128
H_B = 4
DQK_B = 256
DV_B = 512
MLSTM_CHUNK = 128
M_INIT = -1e30
H_C = 32
DH_C = 128
NA_KH = 8
NA_KW = 16
N_GROUPS = 4
EXPERTS_PER_GROUP = 8
N_EXPERTS = 32
D_FF_EXPERT = 192
ROPE_THETA = 10000.0
NORM_EPS = 1e-6
W_A = H_A * D_V_A
W_B = H_B * DV_B
NEG = -1e30

VMEM_LIMIT_BYTES = 56 * 1024 * 1024
ROW_BLOCK = 256
MOD_ROWS = 8
SLAB_ROWS = 32
SLAB_STRIDE = 40
MOE_TILE = 256

P_QK = 0
P_V = 2048
P_O = 4096
P_CQ = 6144
P_CKV = 7168
P_KR = 7680
P_GATE = 7808
P_WIDTH = 8192


def _cparams(sem):
    return pltpu.CompilerParams(dimension_semantics=sem, vmem_limit_bytes=VMEM_LIMIT_BYTES)


def _dma_cparams(sem):
    return pltpu.CompilerParams(dimension_semantics=sem, vmem_limit_bytes=VMEM_LIMIT_BYTES,
                                disable_bounds_checks=True)


def _mod_row(i, rows_per_block, n_lat, seq, batch):
    return jnp.where(i < n_lat // rows_per_block, i // (seq // rows_per_block), batch)


ADA_ROW_CHUNK = 512


def _ada_body(c_ref, w_ref, b_ref, o_ref, sc_ref, *, n_vec):
    d, tn = w_ref.shape[1], w_ref.shape[2]
    reps = tn // 128

    @pl.when((pl.program_id(0) == 0) & (pl.program_id(1) == 0))
    def _():
        c = c_ref[...]
        sc_ref[...] = c * jax.nn.sigmoid(c)

    def chunk(ci, accs):
        rows = pl.ds(pl.multiple_of(ci * ADA_ROW_CHUNK, ADA_ROW_CHUNK), ADA_ROW_CHUNK)
        w = w_ref[0, rows, :]
        out = []
        for m in range(n_vec):
            cm = jnp.concatenate([sc_ref[m, rows, :]] * reps, axis=1)
            part = (w * cm).reshape(ADA_ROW_CHUNK // 8, 8, tn).sum(axis=0)
            out.append(accs[m] + part)
        return tuple(out)

    accs = lax.fori_loop(0, d // ADA_ROW_CHUNK, chunk, tuple(jnp.zeros((8, tn), F32) for _ in range(n_vec)))
    rows = [jnp.sum(a, axis=0, keepdims=True) for a in accs]
    rows += [jnp.zeros((1, tn), F32)] * (MOD_ROWS - n_vec)
    o_ref[0] = jnp.concatenate(rows, axis=0) + b_ref[0]


def ada_mods(cvec, ada_w, ada_b, tn=512):
    depth, d, n6 = ada_w.shape
    n_vec = cvec.shape[0]
    c_lanes = jnp.broadcast_to(cvec[:, :, None], (n_vec, d, 128))
    return pl.pallas_call(
        functools.partial(_ada_body, n_vec=n_vec),
        out_shape=jax.ShapeDtypeStruct((depth, MOD_ROWS, n6), F32),
        grid=(depth, n6 // tn),
        in_specs=[
            pl.BlockSpec((n_vec, d, 128), lambda l, j: (0, 0, 0)),
            pl.BlockSpec((1, d, tn), lambda l, j: (l, 0, j)),
            pl.BlockSpec((1, 1, tn), lambda l, j: (l, 0, j)),
        ],
        out_specs=pl.BlockSpec((1, MOD_ROWS, tn), lambda l, j: (l, 0, j)),
        scratch_shapes=[pltpu.VMEM((n_vec, d, 128), F32)],
        compiler_params=_cparams(("arbitrary", "arbitrary")),
        name="ada_mods",
    )(c_lanes, ada_w, ada_b.reshape(depth, 1, n6))


def _modulated_norm(x_ref, g_ref, sh_ref, sc_ref):
    x = x_ref[...]
    ms = jnp.mean(x * x, axis=-1, keepdims=True)
    y = x * lax.rsqrt(ms + NORM_EPS) * g_ref[...]
    return y * (1.0 + sc_ref[0]) + sh_ref[0]


def _norm_body(x_ref, g_ref, sh_ref, sc_ref, o_ref):
    o_ref[...] = _modulated_norm(x_ref, g_ref, sh_ref, sc_ref).astype(o_ref.dtype)


def _store_slabs(ref, mat, rows):
    for j in range(SLAB_ROWS):
        ref[pl.ds(j, rows, stride=SLAB_STRIDE), :] = mat[:, j * 128:(j + 1) * 128]
    for j in range(SLAB_ROWS, SLAB_STRIDE):
        ref[pl.ds(j, rows, stride=SLAB_STRIDE), :] = jnp.zeros((rows, 128), mat.dtype)


def _norm_router_body(x_ref, g_ref, sh_ref, sc_ref, wr_ref, br_ref, hs_ref, route_ref, cnt_ref, carry):
    @pl.when(pl.program_id(0) == 0)
    def _():
        carry[...] = jnp.zeros_like(carry)

    h = _modulated_norm(x_ref, g_ref, sh_ref, sc_ref)
    _store_slabs(hs_ref, h, ROW_BLOCK)
    logits = jnp.dot(h, wr_ref[...], precision=lax.Precision.HIGHEST,
                     preferred_element_type=F32) + br_ref[...]
    lane = lax.broadcasted_iota(jnp.int32, logits.shape, 1)
    big = jnp.int32(1 << 20)
    is_g = (lane >= N_EXPERTS) & (lane < N_EXPERTS + N_GROUPS)
    gl = jnp.where(is_g, logits, NEG)
    gmax = jnp.max(gl, axis=-1, keepdims=True)
    g_idx = jnp.min(jnp.where(gl == gmax, lane, big), axis=-1, keepdims=True) - N_EXPERTS
    p_g = 1.0 / jnp.sum(jnp.where(is_g, jnp.exp(gl - gmax), 0.0), axis=-1, keepdims=True)
    lo = g_idx * EXPERTS_PER_GROUP
    in_grp = (lane >= lo) & (lane < lo + EXPERTS_PER_GROUP)
    el = jnp.where(in_grp, logits, NEG)
    e1 = jnp.max(el, axis=-1, keepdims=True)
    i1 = jnp.min(jnp.where(el == e1, lane, big), axis=-1, keepdims=True)
    el2 = jnp.where(lane == i1, NEG, el)
    e2 = jnp.max(el2, axis=-1, keepdims=True)
    i2 = jnp.min(jnp.where(el2 == e2, lane, big), axis=-1, keepdims=True)
    t = jnp.exp(e2 - e1)
    w1 = p_g / (1.0 + t)
    w2 = w1 * t
    onehot = jnp.where((lane == i1) | (lane == i2), 1.0, 0.0)
    t_i = lax.broadcasted_iota(jnp.int32, (ROW_BLOCK, ROW_BLOCK), 0)
    s_i = lax.broadcasted_iota(jnp.int32, (ROW_BLOCK, ROW_BLOCK), 1)
    before = jnp.where(s_i < t_i, 1.0, 0.0).astype(BF16)
    seen = jnp.dot(before, onehot.astype(BF16), preferred_element_type=F32) + carry[...]
    rank1 = jnp.sum(jnp.where(lane == i1, seen, 0.0), axis=-1, keepdims=True)
    rank2 = jnp.sum(jnp.where(lane == i2, seen, 0.0), axis=-1, keepdims=True)
    total = carry[...] + jnp.sum(onehot, axis=0, keepdims=True)
    carry[...] = total
    cnt_ref[...] = jnp.broadcast_to(total, cnt_ref.shape)
    cols = (i1.astype(F32), i2.astype(F32), w1, w2, rank1, rank2)
    route = jnp.zeros(logits.shape, F32)
    for k, v in enumerate(cols):
        route = jnp.where(lane == k, v, route)
    route_ref[...] = route


def _norm_specs(d, mods_k, dims):
    n_lat, seq, batch = dims
    mrow = functools.partial(_mod_row, rows_per_block=ROW_BLOCK, n_lat=n_lat, seq=seq, batch=batch)
    k_shift, k_scale = mods_k
    return [
        pl.BlockSpec((ROW_BLOCK, d), lambda i: (i, 0)),
        pl.BlockSpec((1, d), lambda i: (0, 0)),
        pl.BlockSpec((1, 1, d), lambda i: (mrow(i) * 6 + k_shift, 0, 0)),
        pl.BlockSpec((1, 1, d), lambda i: (mrow(i) * 6 + k_scale, 0, 0)),
    ]


def norm_mod(x, g, mods, mods_k, dims):
    n, d = x.shape
    return pl.pallas_call(
        _norm_body,
        out_shape=jax.ShapeDtypeStruct((n, d), BF16),
        grid=(n // ROW_BLOCK,),
        in_specs=_norm_specs(d, mods_k, dims),
        out_specs=pl.BlockSpec((ROW_BLOCK, d), lambda i: (i, 0)),
        compiler_params=_cparams(("parallel",)),
        name="norm_mod",
    )(x, g.reshape(1, d), mods, mods)


def norm_router(x, g, mods, mods_k, dims, w_r, b_r):
    n, d = x.shape
    assert d == SLAB_ROWS * 128
    return pl.pallas_call(
        _norm_router_body,
        out_shape=(jax.ShapeDtypeStruct((n * SLAB_STRIDE, 128), F32), jax.ShapeDtypeStruct((n, 128), F32),
                   jax.ShapeDtypeStruct((8, 128), F32)),
        grid=(n // ROW_BLOCK,),
        in_specs=_norm_specs(d, mods_k, dims) + [
            pl.BlockSpec((d, 128), lambda i: (0, 0)),
            pl.BlockSpec((1, 128), lambda i: (0, 0)),
        ],
        out_specs=(pl.BlockSpec((ROW_BLOCK * SLAB_STRIDE, 128), lambda i: (i, 0)),
                   pl.BlockSpec((ROW_BLOCK, 128), lambda i: (i, 0)),
                   pl.BlockSpec((8, 128), lambda i: (0, 0))),
        scratch_shapes=[pltpu.VMEM((1, 128), F32)],
        compiler_params=_cparams(("arbitrary",)),
        name="norm_router",
    )(x, g.reshape(1, d), mods, mods, w_r, b_r)


def _mm_body(*refs, rms, gated, k1, two_a):
    it = iter(refs)
    a_ref = next(it)
    a2_ref = next(it) if two_a else None
    g_ref = next(it) if rms else None
    w_ref = next(it)
    res_ref = next(it) if gated else None
    gate_ref = next(it) if gated else None
    o_ref = next(it)
    wbf_ref = next(it)

    @pl.when(pl.program_id(1) == 0)
    def _():
        wbf_ref[...] = w_ref[...].astype(BF16)

    a = a_ref[...]
    if rms:
        af = a.astype(F32)
        ms = jnp.mean(af * af, axis=-1, keepdims=True)
        a = (af * lax.rsqrt(ms + NORM_EPS) * g_ref[...]).astype(BF16)
    if two_a:
        acc = (jnp.dot(a, wbf_ref[:k1, :], preferred_element_type=F32)
               + jnp.dot(a2_ref[...], wbf_ref[k1:, :], preferred_element_type=F32))
    else:
        acc = jnp.dot(a, wbf_ref[...], preferred_element_type=F32)
    if gated:
        acc = res_ref[...] + gate_ref[0] * acc
    o_ref[...] = acc.astype(o_ref.dtype)


def matmul(a, w, *, out_dtype, layer=None, a_col=0, k=None, a2=None, rms_gain=None, resid=None, mods=None,
           k_gate=None, dims=None, tm=512, tn=512):
    m = a.shape[0]
    kw_, nw = w.shape[-2], w.shape[-1]
    two_a = a2 is not None
    k = (a.shape[1] if two_a else kw_) if k is None else k
    assert kw_ == k + (a2.shape[1] if two_a else 0) and m % tm == 0 and nw % tn == 0
    rms = rms_gain is not None
    gated = resid is not None
    in_specs = [pl.BlockSpec((tm, k), lambda j, i: (i, a_col))]
    args = [a]
    if two_a:
        in_specs.append(pl.BlockSpec((tm, a2.shape[1]), lambda j, i: (i, 0)))
        args.append(a2)
    if rms:
        in_specs.append(pl.BlockSpec((1, k), lambda j, i: (0, 0)))
        args.append(rms_gain.reshape(1, k))
    if layer is None:
        in_specs.append(pl.BlockSpec((kw_, tn), lambda j, i: (0, j)))
    else:
        in_specs.append(pl.BlockSpec((None, kw_, tn), lambda j, i: (layer, 0, j)))
    args.append(w)
    if gated:
        n_lat, seq, batch = dims
        mrow = functools.partial(_mod_row, rows_per_block=tm, n_lat=n_lat, seq=seq, batch=batch)
        in_specs.append(pl.BlockSpec((tm, tn), lambda j, i: (i, j)))
        in_specs.append(pl.BlockSpec((1, 1, tn), lambda j, i: (mrow(i) * 6 + k_gate, 0, j)))
        args += [resid, mods]
    return pl.pallas_call(
        functools.partial(_mm_body, rms=rms, gated=gated, k1=k, two_a=two_a),
        out_shape=jax.ShapeDtypeStruct((m, nw), out_dtype),
        grid=(nw // tn, m // tm),
        in_specs=in_specs,
        out_specs=pl.BlockSpec((tm, tn), lambda j, i: (i, j)),
        scratch_shapes=[pltpu.VMEM((kw_, tn), BF16)],
        compiler_params=_cparams(("arbitrary", "arbitrary")),
        name="matmul",
    )(*args)


def _rope_rot(pr, trig):
    if trig is not None:
        pr = pr * trig
    else:
        lane = lax.broadcasted_iota(jnp.int32, pr.shape, 1)
        pr = jnp.where(lane < D_ROPE, pr, 0.0)
    return pr + pltpu.roll(pr, D_ROPE, axis=1)


def _mla_body(*refs, n_lat_keys, n_ctx_keys, rope_q, scale, key_chunk, aliased):
    it = iter(refs)
    q_ref = next(it)
    kvl_ref = next(it) if n_lat_keys else None
    krl_ref = next(it) if n_lat_keys else None
    tkl_ref = next(it) if n_lat_keys else None
    kvc_ref = next(it)
    krc_ref = next(it)
    tq_ref = next(it) if rope_q else None
    gq_ref = next(it)
    gk_ref = next(it)
    if aliased:
        next(it)
    o_ref = next(it)
    kbuf = next(it)
    vbuf = next(it)
    dqk = D_NOPE + D_ROPE

    def prep_keys(kv_ref, kr_ref, trig_ref, off, n):
        gk = gk_ref[...]

        def chunk(c, carry):
            rows = pl.ds(pl.multiple_of(c * key_chunk, key_chunk), key_chunk)
            kv = kv_ref[rows, :]
            kn = kv[:, :D_NOPE].astype(F32)
            kr = kr_ref[rows, :].astype(F32)
            lane = lax.broadcasted_iota(jnp.int32, kr.shape, 1)
            ss = (jnp.sum(kn * kn, axis=-1, keepdims=True)
                  + jnp.sum(jnp.where(lane < D_ROPE, kr * kr, 0.0), axis=-1, keepdims=True))
            r = lax.rsqrt(ss / dqk + NORM_EPS)
            trig = trig_ref[rows, :] if trig_ref is not None else None
            rot = _rope_rot(kr * (r * gk[:, D_NOPE:]), trig)
            orow = pl.ds(pl.multiple_of(off + c * key_chunk, key_chunk), key_chunk)
            kbuf[orow, :D_NOPE] = (kn * (r * gk[:, :D_NOPE])).astype(BF16)
            kbuf[orow, D_NOPE:] = rot.astype(BF16)
            vbuf[orow, :] = kv[:, D_NOPE:].astype(BF16)
            return carry

        lax.fori_loop(0, n // key_chunk, chunk, 0)

    @pl.when(pl.program_id(2) == 0)
    def _():
        if n_lat_keys:
            prep_keys(kvl_ref, krl_ref, tkl_ref, 0, n_lat_keys)
        prep_keys(kvc_ref, krc_ref, None, n_lat_keys, n_ctx_keys)

    gq = gq_ref[...]
    for sb in range(q_ref.shape[0] // MLA_SUB_Q):
        rows = slice(sb * MLA_SUB_Q, (sb + 1) * MLA_SUB_Q)
        q = q_ref[rows, :].astype(F32)
        qn = q[:, :D_NOPE]
        qr = q[:, D_NOPE:]
        lane = lax.broadcasted_iota(jnp.int32, qr.shape, 1)
        ss = (jnp.sum(qn * qn, axis=-1, keepdims=True)
              + jnp.sum(jnp.where(lane < D_ROPE, qr * qr, 0.0), axis=-1, keepdims=True))
        r = lax.rsqrt(ss / dqk + NORM_EPS) * scale
        rot = _rope_rot(qr * (r * gq[:, D_NOPE:]), tq_ref[rows, :] if rope_q else None)
        rot = jnp.where(lane < D_ROPE, rot, 0.0)
        q2 = jnp.concatenate([qn * (r * gq[:, :D_NOPE]), rot], axis=1).astype(BF16)
        s = lax.dot_general(q2, kbuf[...], (((1,), (1,)), ((), ())), preferred_element_type=F32)
        m = jnp.max(s, axis=-1, keepdims=True)
        p = jnp.exp2(s - m)
        l = jnp.sum(p, axis=-1, keepdims=True)
        o = jnp.dot(p.astype(BF16), vbuf[...], preferred_element_type=F32)
        o_ref[rows, :] = (o / l).astype(o_ref.dtype)


MLA_SUB_Q = 256


def mla_attention(q_all, kv_all, p_all, trig, gq, gk, *, batch, seq, ctx_len, latent_queries, out_buf=None):
    n_lat = batch * seq
    kr_blk = P_KR // 128
    scale = (D_NOPE + D_ROPE) ** -0.5 * float(np.log2(np.e))
    if latent_queries:
        tq = 2 * MLA_SUB_Q
        nq, n_lat_keys, q_row0 = seq, seq, 0
    else:
        tq = ctx_len
        nq, n_lat_keys, q_row0 = ctx_len, 0, n_lat // tq
    assert tq % MLA_SUB_Q == 0 and nq % tq == 0
    nqb = nq // tq
    ctx_blk0 = n_lat // ctx_len
    in_specs = [pl.BlockSpec((tq, 256), lambda b, h, i: (q_row0 + b * nqb + i, h))]
    args = [q_all]
    if latent_queries:
        in_specs += [
            pl.BlockSpec((seq, 256), lambda b, h, i: (b, h)),
            pl.BlockSpec((seq, 128), lambda b, h, i: (b, kr_blk)),
            pl.BlockSpec((seq, 128), lambda b, h, i: (0, 0)),
        ]
        args += [kv_all, p_all, trig]
    in_specs += [
        pl.BlockSpec((ctx_len, 256), lambda b, h, i: (ctx_blk0 + b, h)),
        pl.BlockSpec((ctx_len, 128), lambda b, h, i: (ctx_blk0 + b, kr_blk)),
    ]
    args += [kv_all, p_all]
    if latent_queries:
        in_specs.append(pl.BlockSpec((tq, 128), lambda b, h, i: (i, 0)))
        args.append(trig)
    in_specs += [pl.BlockSpec((1, 256), lambda b, h, i: (0, 0))] * 2
    args += [gq, gk]
    aliases = {}
    if out_buf is not None:
        in_specs.append(pl.BlockSpec(memory_space=pl.ANY))
        aliases = {len(args): 0}
        args.append(out_buf)
    nk = n_lat_keys + ctx_len
    body = functools.partial(_mla_body, n_lat_keys=n_lat_keys, n_ctx_keys=ctx_len, rope_q=latent_queries,
                             scale=scale, key_chunk=min(256, ctx_len), aliased=out_buf is not None)
    return pl.pallas_call(
        body,
        out_shape=jax.ShapeDtypeStruct((q_all.shape[0], W_A), BF16),
        grid=(batch, H_A, nqb),
        in_specs=in_specs,
        out_specs=pl.BlockSpec((tq, D_V_A), lambda b, h, i: (q_row0 + b * nqb + i, h)),
        scratch_shapes=[pltpu.VMEM((nk, 256), BF16), pltpu.VMEM((nk, D_V_A), BF16)],
        input_output_aliases=aliases,
        compiler_params=_cparams(("arbitrary", "arbitrary", "arbitrary")),
        name="mla_attention",
    )(*args)


def _conv_silu_body(x_ref, prev_ref, next_ref, w_ref, o_ref, *, lat_blocks, seq_blocks, ctx_blocks):
    i = pl.program_id(0)
    j = jnp.where(i < lat_blocks, i % seq_blocks, (i - lat_blocks) % ctx_blocks)
    nb = jnp.where(i < lat_blocks, seq_blocks, ctx_blocks)
    x = x_ref[...]
    rows = x.shape[0]
    row = lax.broadcasted_iota(jnp.int32, x.shape, 0)
    prev_row = jnp.where(j == 0, 0.0, prev_ref[7:8, :])
    next_row = jnp.where(j == nb - 1, 0.0, next_ref[0:1, :])
    xp = jnp.where(row == 0, prev_row, pltpu.roll(x, 1, axis=0))
    xn = jnp.where(row == rows - 1, next_row, pltpu.roll(x, rows - 1, axis=0))
    w = w_ref[...]
    y = xp * w[0:1, :] + x * w[1:2, :] + xn * w[2:3, :]
    y = y * jax.nn.sigmoid(y)
    half = y.shape[1] // 2
    o_ref[:, :half] = y[:, :half].astype(o_ref.dtype)
    o_ref[:, half:] = (y[:, half:] * (DQK_B ** -0.5)).astype(o_ref.dtype)


def conv_silu(p_all, conv_w, *, n_lat, seq, ctx_len):
    n = p_all.shape[0]
    width = 2 * H_B * DQK_B
    rb = ROW_BLOCK
    nblk = n // rb
    sub = rb // 8
    last8 = n // 8 - 1
    body = functools.partial(_conv_silu_body, lat_blocks=n_lat // rb, seq_blocks=seq // rb,
                             ctx_blocks=ctx_len // rb)
    return pl.pallas_call(
        body,
        out_shape=jax.ShapeDtypeStruct((n, width), BF16),
        grid=(nblk,),
        in_specs=[
            pl.BlockSpec((rb, width), lambda i: (i, P_QK // width)),
            pl.BlockSpec((8, width), lambda i: (jnp.maximum(i * sub - 1, 0), P_QK // width)),
            pl.BlockSpec((8, width), lambda i: (jnp.minimum((i + 1) * sub, last8), P_QK // width)),
            pl.BlockSpec((3, width), lambda i: (0, 0)),
        ],
        out_specs=pl.BlockSpec((rb, width), lambda i: (i, 0)),
        compiler_params=_cparams(("parallel",)),
        name="conv_silu",
    )(p_all, p_all, p_all, conv_w)


def _log_sigmoid(x):
    return jnp.minimum(x, 0.0) - jnp.log(1.0 + jnp.exp(-jnp.abs(x)))


def _mlstm_body(qf, kf, vf, gcf, grf, qb, kb, vb, gcb, grb, bc_ref, br_ref, of_ref, ob_ref, c_st, n_st, m_st):
    @pl.when(pl.program_id(2) == 0)
    def _():
        c_st[...] = jnp.zeros_like(c_st)
        n_st[...] = jnp.zeros_like(n_st)
        m_st[...] = jnp.full_like(m_st, M_INIT)

    _mlstm_chunk(0, qf, kf, vf, gcf, grf, bc_ref, br_ref, of_ref, c_st.at[0], n_st.at[0], m_st.at[0])
    _mlstm_chunk(1, qb, kb, vb, gcb, grb, bc_ref, br_ref, ob_ref, c_st.at[1], n_st.at[1], m_st.at[1])


def _mlstm_chunk(direction, q_ref, k_ref, v_ref, gc_ref, gr_ref, bc_ref, br_ref, o_ref, c_st, n_st, m_st):
    h = pl.program_id(1)
    L = MLSTM_CHUNK
    hi = lax.Precision.HIGHEST
    gi = direction * (2 * H_B) + h
    gcol = gc_ref[...] + bc_ref[...]
    lane = lax.broadcasted_iota(jnp.int32, gcol.shape, 1)
    li_c = jnp.sum(jnp.where(lane == gi, gcol, 0.0), axis=1, keepdims=True)
    lf_c = _log_sigmoid(jnp.sum(jnp.where(lane == gi + H_B, gcol, 0.0), axis=1, keepdims=True))
    grow = gr_ref[...] + br_ref[...]
    sub = lax.broadcasted_iota(jnp.int32, grow.shape, 0)
    li_r = jnp.sum(jnp.where(sub == gi, grow, 0.0), axis=0, keepdims=True)
    lf_r = _log_sigmoid(jnp.sum(jnp.where(sub == gi + H_B, grow, 0.0), axis=0, keepdims=True))

    t_i = lax.broadcasted_iota(jnp.int32, (L, L), 0)
    s_i = lax.broadcasted_iota(jnp.int32, (L, L), 1)
    sgn = 1 - 2 * direction
    incl = (s_i - t_i) * sgn <= 0
    incl_f = incl.astype(F32)
    incl_t = ((t_i - s_i) * sgn <= 0).astype(F32)
    bcum_c = jnp.dot(incl_f, jnp.broadcast_to(lf_c, (L, L)), precision=hi, preferred_element_type=F32)
    bcum_r = jnp.dot(jnp.broadcast_to(lf_r, (8, L)), incl_t, precision=hi, preferred_element_type=F32)[0:1, :]
    b_last = jnp.sum(lf_r, axis=1, keepdims=True)

    m_prev = m_st[...]
    a_c = bcum_c[:, 0:1] + m_prev
    dlog = jnp.where(incl, bcum_c - bcum_r + li_r, -jnp.inf)
    m_t = jnp.maximum(a_c, jnp.max(dlog, axis=1, keepdims=True))
    w_intra = jnp.exp(dlog - m_t)
    w_inter = jnp.exp(a_c - m_t)

    q = q_ref[...]
    k = k_ref[...]
    v = v_ref[...].astype(BF16)
    c_prev = c_st[...]
    n_prev = n_st[...]
    s = lax.dot_general(q, k, (((1,), (1,)), ((), ())), preferred_element_type=F32) * w_intra
    num = (w_inter * jnp.dot(q, c_prev.astype(BF16), preferred_element_type=F32)
           + jnp.dot(s.astype(BF16), v, preferred_element_type=F32))
    den = (w_inter * jnp.sum(q.astype(F32) * n_prev, axis=1, keepdims=True)
           + jnp.sum(s, axis=1, keepdims=True))
    o_ref[...] = num / jnp.maximum(jnp.abs(den), jnp.exp(-m_t))

    g_r = b_last - bcum_r + li_r
    g_c = b_last - bcum_c[:, 0:1] + li_c
    m_new = jnp.maximum(b_last + m_prev, jnp.max(g_r, axis=1, keepdims=True))
    decay = jnp.exp(b_last + m_prev - m_new)
    kw = k.astype(F32) * jnp.exp(g_c - m_new)
    c_st[...] = decay * c_prev + lax.dot_general(kw.astype(BF16), v, (((0,), (0,)), ((), ())),
                                                 preferred_element_type=F32)
    n_st[...] = decay * n_prev + jnp.sum(kw, axis=0, keepdims=True)
    m_st[...] = m_new


def mlstm_scan(qk, p_all, gates_t, gate_b, *, batch, seq, ctx_len):
    n = qk.shape[0]
    L = MLSTM_CHUNK
    cc, lc = ctx_len // L, seq // L
    n_lat_blk = batch * lc

    def rowblk(b, d, c):
        ctx_blk = n_lat_blk + b * cc + (c if d == 0 else cc - 1 - c)
        lat_blk = b * lc + (c - cc if d == 0 else lc - 1 - (c - cc))
        return jnp.where(c < cc, ctx_blk, lat_blk)

    def chunk_specs(d):
        return [
            pl.BlockSpec((L, DQK_B), lambda b, h, c: (rowblk(b, d, c), h)),
            pl.BlockSpec((L, DQK_B), lambda b, h, c: (rowblk(b, d, c), H_B + h)),
            pl.BlockSpec((L, DV_B), lambda b, h, c: (rowblk(b, d, c), P_V // DV_B + h)),
            pl.BlockSpec((L, 128), lambda b, h, c: (rowblk(b, d, c), P_GATE // 128)),
            pl.BlockSpec((4 * H_B, L), lambda b, h, c: (0, rowblk(b, d, c))),
        ]

    def out_spec(d):
        return pl.BlockSpec((L, DV_B), lambda b, h, c: (rowblk(b, d, c), h))

    gb_col = jnp.zeros((1, 128), F32).at[0, :4 * H_B].set(gate_b)
    gb_row = gate_b.reshape(4 * H_B, 1)
    chunk_args = [qk, qk, p_all, p_all, gates_t]
    return pl.pallas_call(
        _mlstm_body,
        out_shape=(jax.ShapeDtypeStruct((n, W_B), F32), jax.ShapeDtypeStruct((n, W_B), F32)),
        grid=(batch, H_B, cc + lc),
        in_specs=chunk_specs(0) + chunk_specs(1) + [
            pl.BlockSpec((1, 128), lambda b, h, c: (0, 0)),
            pl.BlockSpec((4 * H_B, 1), lambda b, h, c: (0, 0)),
        ],
        out_specs=(out_spec(0), out_spec(1)),
        scratch_shapes=[pltpu.VMEM((2, DQK_B, DV_B), F32), pltpu.VMEM((2, 1, DQK_B), F32),
                        pltpu.VMEM((2, 1, 1), F32)],
        compiler_params=_cparams(("arbitrary",) * 3),
        name="mlstm_scan",
    )(*chunk_args, *chunk_args, gb_col, gb_row)


def _mlstm_finish_body(hf_ref, hb_ref, o_ref, g_ref, out_ref):
    hs = hf_ref[...] + hb_ref[...]
    g = g_ref[...]
    og = o_ref[...]
    for hd in range(H_B):
        sl = slice(hd * DV_B, (hd + 1) * DV_B)
        x = hs[:, sl]
        ms = jnp.mean(x * x, axis=-1, keepdims=True)
        hn = x * lax.rsqrt(ms + NORM_EPS) * g[:, sl]
        out_ref[:, sl] = (jax.nn.sigmoid(og[:, sl]) * hn).astype(out_ref.dtype)


def mlstm_finish(h_fwd, h_bwd, p_all, out_norm_g):
    n = h_fwd.shape[0]
    rb = ROW_BLOCK
    return pl.pallas_call(
        _mlstm_finish_body,
        out_shape=jax.ShapeDtypeStruct((n, W_B), BF16),
        grid=(n // rb,),
        in_specs=[
            pl.BlockSpec((rb, W_B), lambda i: (i, 0)),
            pl.BlockSpec((rb, W_B), lambda i: (i, 0)),
            pl.BlockSpec((rb, W_B), lambda i: (i, P_O // W_B)),
            pl.BlockSpec((1, W_B), lambda i: (0, 0)),
        ],
        out_specs=pl.BlockSpec((rb, W_B), lambda i: (i, 0)),
        compiler_params=_cparams(("parallel",)),
        name="mlstm_finish",
    )(h_fwd, h_bwd, p_all, out_norm_g.reshape(1, W_B))


NA_ROWS_PER_STEP = 4
NA_TILES_PER_STEP = 4


def _na_geometry(rows_n):
    r_t = NA_ROWS_PER_STEP
    kh = min(NA_KH, rows_n)
    win = min(r_t + kh - 1, rows_n)
    tiles = rows_n // r_t
    starts, sigs, cls = [], [], []
    for t in range(tiles):
        rq0 = t * r_t
        start = int(np.clip(rq0 - kh // 2, 0, rows_n - win))
        r0 = np.clip(np.arange(rq0, rq0 + r_t) - kh // 2, 0, rows_n - kh)
        sig = (start - rq0, tuple((r0 - rq0).tolist()))
        if sig not in sigs:
            sigs.append(sig)
        starts.append(start)
        cls.append(sigs.index(sig))
    return kh, win, np.array(starts, np.int32), np.array(cls, np.int32), sigs


def na_bias_rows(rpb):
    cols = np.arange(GRID_W)
    c0 = np.clip(cols - NA_KW // 2, 0, GRID_W - NA_KW)
    col_valid = (cols[None, :] >= c0[:, None]) & (cols[None, :] < c0[:, None] + NA_KW)
    col_off = cols[None, :] - cols[:, None] + (NA_KW - 1)
    c_sel = (col_valid[:, :, None] & (col_off[:, :, None] == np.arange(2 * NA_KW - 1))).astype(np.float32)
    t = jnp.einsum('hrs,cds->hrcd', rpb, c_sel, precision=lax.Precision.HIGHEST)
    t = t + np.where(col_valid, 0.0, NEG).astype(np.float32)[None, None]
    return jnp.concatenate([t, t], axis=-1)


def _na_bias_plan(rows_n):
    r_t = NA_ROWS_PER_STEP
    kh, win, _, _, sigs = _na_geometry(rows_n)
    plan = []
    for (dstart, dr0) in sigs:
        cls_plan = []
        for a in range(r_t):
            row = []
            for j in range(win):
                krow = dstart + j
                ok = dr0[a] <= krow < dr0[a] + kh
                row.append(krow - a + (NA_KH - 1) if ok else None)
            cls_plan.append(row)
        plan.append(cls_plan)
    return plan


def _na_build_bias(rows_ref, btab, plan):
    neg = jnp.full((GRID_W, 2 * GRID_W), NEG, F32)
    lane = lax.broadcasted_iota(jnp.int32, (GRID_W, 2 * GRID_W), 1)

    def piece(ro):
        return neg if ro is None else rows_ref[0, ro]

    for k, cls_plan in enumerate(plan):
        for a, row in enumerate(cls_plan):
            rs = slice(a * GRID_W, (a + 1) * GRID_W)
            for j in range(0, len(row) - 1, 2):
                both = jnp.where(lane < GRID_W, piece(row[j]), piece(row[j + 1]))
                btab[k, rs, j * GRID_W:(j + 2) * GRID_W] = both
            if len(row) % 2:
                j = len(row) - 1
                btab[k, rs, j * GRID_W:(j + 1) * GRID_W] = piece(row[j])[:, :GRID_W]


def _head_rmsnorm(x, g):
    xf = x.astype(F32)
    ms = jnp.mean(xf * xf, axis=-1, keepdims=True)
    return xf * lax.rsqrt(ms + NORM_EPS) * g


def _na_body(start_ref, cls_ref, q_ref, k_ref, v_ref, kc_ref, vc_ref, brow_ref, gq_ref, gk_ref, *rest,
             seq, ctx_len, win_rows, scale, bias_plan):
    o_ref, kn, knc, btab = rest[-4:]
    i = pl.program_id(2)
    chunk = min(512, seq)

    @pl.when(i == 0)
    def _():
        _na_build_bias(brow_ref, btab, bias_plan)
        gk = gk_ref[...]

        def body(c, carry):
            rows = pl.ds(pl.multiple_of(c * chunk, chunk), chunk)
            kn[rows, :] = _head_rmsnorm(k_ref[rows, :], gk).astype(BF16)
            return carry

        lax.fori_loop(0, seq // chunk, body, 0)
        knc[...] = _head_rmsnorm(kc_ref[...], gk).astype(BF16)

    nt = (((1,), (1,)), ((), ()))
    tq = NA_ROWS_PER_STEP * GRID_W
    for tl in range(NA_TILES_PER_STEP):
        rows = slice(tl * tq, (tl + 1) * tq)
        q = (_head_rmsnorm(q_ref[rows, :], gq_ref[...]) * scale).astype(BF16)
        krow0 = pl.multiple_of(start_ref[i * NA_TILES_PER_STEP + tl] * GRID_W, GRID_W)
        kw = kn[pl.ds(krow0, win_rows), :]
        vw = v_ref[pl.ds(krow0, win_rows), :]
        s = lax.dot_general(q, kw, nt, preferred_element_type=F32) + btab[cls_ref[i * NA_TILES_PER_STEP + tl]]
        sc = lax.dot_general(q, knc[...], nt, preferred_element_type=F32)
        m = jnp.maximum(jnp.max(s, axis=-1, keepdims=True), jnp.max(sc, axis=-1, keepdims=True))
        p = jnp.exp(s - m)
        pc = jnp.exp(sc - m)
        l = jnp.sum(p, axis=-1, keepdims=True) + jnp.sum(pc, axis=-1, keepdims=True)
        o = (jnp.dot(p.astype(BF16), vw, preferred_element_type=F32)
             + jnp.dot(pc.astype(BF16), vc_ref[...], preferred_element_type=F32))
        o_ref[rows, :] = (o / l).astype(o_ref.dtype)


def na_attention(qkv, bias_rows, gq, gk, *, batch, seq, ctx_len, out_buf=None):
    rows_n = seq // GRID_W
    r_t = NA_ROWS_PER_STEP
    _, win, starts, cls, _ = _na_geometry(rows_n)
    tps = NA_TILES_PER_STEP
    tq = tps * r_t * GRID_W
    tiles = rows_n // (r_t * tps)
    assert rows_n % (r_t * tps) == 0
    n_lat = batch * seq
    ctx_blk0 = n_lat // ctx_len
    plan = _na_bias_plan(rows_n)
    body = functools.partial(_na_body, seq=seq, ctx_len=ctx_len, win_rows=win * GRID_W, scale=DH_C ** -0.5,
                             bias_plan=plan)
    n_ro = 2 * NA_KH - 1
    in_specs = [
        pl.BlockSpec((tq, DH_C), lambda b, h, i, st, cl: (b * tiles + i, h)),
        pl.BlockSpec((seq, DH_C), lambda b, h, i, st, cl: (b, H_C + h)),
        pl.BlockSpec((seq, DH_C), lambda b, h, i, st, cl: (b, 2 * H_C + h)),
        pl.BlockSpec((ctx_len, DH_C), lambda b, h, i, st, cl: (ctx_blk0 + b, H_C + h)),
        pl.BlockSpec((ctx_len, DH_C), lambda b, h, i, st, cl: (ctx_blk0 + b, 2 * H_C + h)),
        pl.BlockSpec((1, n_ro, GRID_W, 2 * GRID_W), lambda b, h, i, st, cl: (h, 0, 0, 0)),
        pl.BlockSpec((1, DH_C), lambda b, h, i, st, cl: (0, 0)),
        pl.BlockSpec((1, DH_C), lambda b, h, i, st, cl: (0, 0)),
    ]
    args = [jnp.asarray(starts), jnp.asarray(cls), qkv, qkv, qkv, qkv, qkv, bias_rows, gq, gk]
    aliases = {}
    out_rows = n_lat
    if out_buf is not None:
        in_specs.append(pl.BlockSpec(memory_space=pl.ANY))
        aliases = {len(args): 0}
        args.append(out_buf)
        out_rows = out_buf.shape[0]
    grid_spec = pltpu.PrefetchScalarGridSpec(
        num_scalar_prefetch=2,
        grid=(batch, H_C, tiles),
        in_specs=in_specs,
        out_specs=pl.BlockSpec((tq, DH_C), lambda b, h, i, st, cl: (b * tiles + i, h)),
        scratch_shapes=[pltpu.VMEM((seq, DH_C), BF16), pltpu.VMEM((ctx_len, DH_C), BF16),
                        pltpu.VMEM((len(plan), r_t * GRID_W, win * GRID_W), F32)],
    )
    return pl.pallas_call(
        body,
        out_shape=jax.ShapeDtypeStruct((out_rows, H_C * DH_C), BF16),
        grid_spec=grid_spec,
        input_output_aliases=aliases,
        compiler_params=_cparams(("arbitrary", "arbitrary", "arbitrary")),
        name="na_attention",
    )(*args)


def _ctx_attn_body(q_ref, k_ref, v_ref, gq_ref, gk_ref, buf_ref, o_ref, *, scale):
    del buf_ref
    q = (_head_rmsnorm(q_ref[...], gq_ref[...]) * scale).astype(BF16)
    k = _head_rmsnorm(k_ref[...], gk_ref[...]).astype(BF16)
    s = lax.dot_general(q, k, (((1,), (1,)), ((), ())), preferred_element_type=F32)
    m = jnp.max(s, axis=-1, keepdims=True)
    p = jnp.exp(s - m)
    l = jnp.sum(p, axis=-1, keepdims=True)
    o = jnp.dot(p.astype(BF16), v_ref[...], preferred_element_type=F32)
    o_ref[...] = (o / l).astype(o_ref.dtype)


def na_ctx_attention(qkv, gq, gk, out_buf, *, batch, seq, ctx_len):
    ctx_blk0 = batch * seq // ctx_len
    return pl.pallas_call(
        functools.partial(_ctx_attn_body, scale=DH_C ** -0.5),
        out_shape=jax.ShapeDtypeStruct(out_buf.shape, out_buf.dtype),
        grid=(batch, H_C),
        in_specs=[
            pl.BlockSpec((ctx_len, DH_C), lambda b, h: (ctx_blk0 + b, h)),
            pl.BlockSpec((ctx_len, DH_C), lambda b, h: (ctx_blk0 + b, H_C + h)),
            pl.BlockSpec((ctx_len, DH_C), lambda b, h: (ctx_blk0 + b, 2 * H_C + h)),
            pl.BlockSpec((1, DH_C), lambda b, h: (0, 0)),
            pl.BlockSpec((1, DH_C), lambda b, h: (0, 0)),
            pl.BlockSpec(memory_space=pl.ANY),
        ],
        out_specs=pl.BlockSpec((ctx_len, DH_C), lambda b, h: (ctx_blk0 + b, h)),
        input_output_aliases={5: 0},
        compiler_params=_cparams(("parallel", "parallel")),
        name="na_ctx_attention",
    )(qkv, qkv, qkv, gq, gk, out_buf)


def _slab_copy(src_ref, src_tok, dst_ref, dst_tok, sem):
    def rows(tok):
        off = tok * SLAB_STRIDE
        return pl.ds(off if isinstance(off, int) else pl.multiple_of(off, 8), SLAB_ROWS)

    return pltpu.make_async_copy(src_ref.at[rows(src_tok), :], dst_ref.at[rows(dst_tok), :], sem)


def _load_slabs(ref, rows, dtype):
    return jnp.concatenate([ref[pl.ds(j, rows, stride=SLAB_STRIDE), :].astype(dtype) for j in range(SLAB_ROWS)],
                           axis=1)


def moe_slot_plan(route, counts, n_tiles):
    cnt = counts[0, :N_EXPERTS].astype(jnp.int32)
    padded = (cnt + MOE_TILE - 1) // MOE_TILE * MOE_TILE
    ends = jnp.cumsum(padded)
    base = ends - padded
    n_used = ends[-1] // MOE_TILE
    tile_start = jnp.arange(n_tiles, dtype=jnp.int32) * MOE_TILE
    last_start = jnp.maximum(ends[-1] - MOE_TILE, 0)
    tile_expert = jnp.searchsorted(ends, jnp.minimum(tile_start, last_start), side='right').astype(jnp.int32)
    tile_expert = jnp.minimum(tile_expert, N_EXPERTS - 1)
    ids = route[:, 0:2].astype(jnp.int32)
    pos = jnp.take(base, ids, axis=0) + route[:, 4:6].astype(jnp.int32)
    return pos[:, 0], pos[:, 1], tile_expert, n_used.reshape(1).astype(jnp.int32)


def _slot_source_body(p1_ref, p2_ref, src_ref, *, n_tok):
    def clear(i, c):
        src_ref[i] = 0
        return c

    lax.fori_loop(0, src_ref.shape[0], clear, 0, unroll=8)

    def fill(n, c):
        src_ref[p1_ref[n]] = n
        src_ref[p2_ref[n]] = n
        return c

    lax.fori_loop(0, n_tok, fill, 0, unroll=4)


def moe_slot_source(pos1, pos2, n_slots):
    smem = pl.BlockSpec(memory_space=pltpu.SMEM)
    return pl.pallas_call(
        functools.partial(_slot_source_body, n_tok=pos1.shape[0]),
        out_shape=jax.ShapeDtypeStruct((n_slots,), jnp.int32),
        in_specs=[smem, smem],
        out_specs=smem,
        name="moe_slot_source",
    )(pos1, pos2)


def _gather_slabs(src_ref, tok_ref, tok0, dst_ref, sem, n):
    def issue(r, c):
        _slab_copy(src_ref, tok_ref[tok0 + r], dst_ref, r, sem).start()
        return c

    lax.fori_loop(0, n, issue, 0, unroll=8)


def _wait_slabs(src_ref, dst_ref, sem, n):
    rows = pl.ds(0, n * SLAB_ROWS)
    pltpu.make_async_copy(src_ref.at[rows, :], dst_ref.at[rows, :], sem).wait()


def _moe_expert_body(te_ref, nu_ref, src_ref, hs_ref, wg_ref, wu_ref, wd_ref, o_ref, xbuf, wg_bf, wu_bf, wd_bf,
                     sems):
    t = pl.program_id(0)
    tm = MOE_TILE
    slot = t % 2

    @pl.when(t < nu_ref[0])
    def _():
        @pl.when(t == 0)
        def _():
            _gather_slabs(hs_ref, src_ref, 0, xbuf.at[0], sems.at[0], tm)

        @pl.when(t + 1 < nu_ref[0])
        def _():
            _gather_slabs(hs_ref, src_ref, (t + 1) * tm, xbuf.at[1 - slot], sems.at[1 - slot], tm)

        @pl.when((t == 0) | (te_ref[t] != te_ref[jnp.maximum(t - 1, 0)]))
        def _():
            wg_bf[...] = wg_ref[...].astype(BF16)
            wu_bf[...] = wu_ref[...].astype(BF16)
            wd_bf[...] = wd_ref[...].astype(BF16)

        _wait_slabs(hs_ref, xbuf.at[slot], sems.at[slot], tm)
        x = _load_slabs(xbuf.at[slot], tm, BF16)
        nt = (((1,), (1,)), ((), ()))
        a = lax.dot_general(x, wg_bf[...], nt, preferred_element_type=F32)
        u = lax.dot_general(x, wu_bf[...], nt, preferred_element_type=F32)
        y = (a * jax.nn.sigmoid(a) * u).astype(BF16)
        _store_slabs(o_ref, jnp.dot(y, wd_bf[...], preferred_element_type=F32), tm)

    @pl.when(t >= nu_ref[0])
    def _():
        o_ref[...] = jnp.zeros_like(o_ref)


def moe_experts(hs, src, tile_expert, n_used, w_gate_t, w_up_t, w_down, layer):
    n_tiles = tile_expert.shape[0]
    f, d = w_gate_t.shape[-2:]
    blk = MOE_TILE * SLAB_STRIDE
    wspec = pl.BlockSpec((None, None, f, d), lambda t, te, nu, src: (layer, te[t], 0, 0))
    grid_spec = pltpu.PrefetchScalarGridSpec(
        num_scalar_prefetch=3,
        grid=(n_tiles,),
        in_specs=[pl.BlockSpec(memory_space=pl.ANY), wspec, wspec, wspec],
        out_specs=pl.BlockSpec((blk, 128), lambda t, te, nu, src: (t, 0)),
        scratch_shapes=[pltpu.VMEM((2, blk, 128), F32), pltpu.VMEM((f, d), BF16), pltpu.VMEM((f, d), BF16),
                        pltpu.VMEM((f, d), BF16), pltpu.SemaphoreType.DMA((2,))],
    )
    return pl.pallas_call(
        _moe_expert_body,
        out_shape=jax.ShapeDtypeStruct((n_tiles * blk, 128), F32),
        grid_spec=grid_spec,
        compiler_params=_dma_cparams(("arbitrary",)),
        name="moe_experts",
    )(tile_expert, n_used, src, hs, w_gate_t, w_up_t, w_down)


def _moe_combine_body(p1_ref, p2_ref, ys_ref, route_ref, res_ref, gate_ref, o_ref, abuf, bbuf, sems):
    i = pl.program_id(0)
    rb = ROW_BLOCK
    slot = i % 2

    def gather(blk, s):
        _gather_slabs(ys_ref, p1_ref, blk * rb, abuf.at[s], sems.at[0, s], rb)
        _gather_slabs(ys_ref, p2_ref, blk * rb, bbuf.at[s], sems.at[1, s], rb)

    @pl.when(i == 0)
    def _():
        gather(0, 0)

    @pl.when(i + 1 < pl.num_programs(0))
    def _():
        gather(i + 1, 1 - slot)

    route = route_ref[...]
    lane = lax.broadcasted_iota(jnp.int32, route.shape, 1)
    w1 = jnp.sum(jnp.where(lane == 2, route, 0.0), axis=1, keepdims=True)
    w2 = jnp.sum(jnp.where(lane == 3, route, 0.0), axis=1, keepdims=True)
    _wait_slabs(ys_ref, abuf.at[slot], sems.at[0, slot], rb)
    _wait_slabs(ys_ref, bbuf.at[slot], sems.at[1, slot], rb)
    for j in range(SLAB_ROWS):
        sl = slice(j * 128, (j + 1) * 128)
        rows = pl.ds(j, rb, stride=SLAB_STRIDE)
        mix = w1 * abuf[slot, rows, :] + w2 * bbuf[slot, rows, :]
        o_ref[:, sl] = res_ref[:, sl] + gate_ref[0][:, sl] * mix


def moe_combine(ys, pos1, pos2, route, resid, mods, k_gate, dims):
    n, d = route.shape[0], resid.shape[1]
    n_lat, seq, batch = dims
    rb = ROW_BLOCK
    mrow = functools.partial(_mod_row, rows_per_block=rb, n_lat=n_lat, seq=seq, batch=batch)
    grid_spec = pltpu.PrefetchScalarGridSpec(
        num_scalar_prefetch=2,
        grid=(n // rb,),
        in_specs=[
            pl.BlockSpec(memory_space=pl.ANY),
            pl.BlockSpec((rb, 128), lambda i, p1, p2: (i, 0)),
            pl.BlockSpec((rb, d), lambda i, p1, p2: (i, 0)),
            pl.BlockSpec((1, 1, d), lambda i, p1, p2: (mrow(i) * 6 + k_gate, 0, 0)),
        ],
        out_specs=pl.BlockSpec((rb, d), lambda i, p1, p2: (i, 0)),
        scratch_shapes=[pltpu.VMEM((2, rb * SLAB_STRIDE, 128), F32), pltpu.VMEM((2, rb * SLAB_STRIDE, 128), F32),
                        pltpu.SemaphoreType.DMA((2, 2))],
    )
    return pl.pallas_call(
        _moe_combine_body,
        out_shape=jax.ShapeDtypeStruct((n, d), F32),
        grid_spec=grid_spec,
        compiler_params=_dma_cparams(("arbitrary",)),
        name="moe_combine",
    )(pos1, pos2, ys, route, resid, mods)


def _swap16(a):
    lead = a.shape[:-1]
    return jnp.flip(a.reshape(*lead, D_ROPE // 32, 2, 16), axis=-2).reshape(*lead, D_ROPE)


def _ext_w_in(w):
    c_kr = Q_LORA + KV_LORA
    c_qb = c_kr + D_ROPE
    c_g = c_qb + 2 * H_B * DQK_B + 2 * W_B
    kr = w[:, c_kr:c_qb]
    used = (c_g - c_qb) + c_kr + 2 * D_ROPE
    gates = jnp.pad(w[:, c_g:], ((0, 0), (0, P_WIDTH - used - 4 * H_B)))
    return jnp.concatenate([w[:, c_qb:c_g], w[:, :c_kr], kr, _swap16(kr), gates], axis=1)


def _ext_w_q_up(w):
    w3 = w.reshape(Q_LORA, H_A, D_NOPE + D_ROPE)
    return jnp.concatenate([w3, _swap16(w3[:, :, D_NOPE:])], axis=2).reshape(Q_LORA, H_A * 256)


def _ext_gain(g):
    return jnp.concatenate([g, _swap16(g[D_NOPE:])]).reshape(1, 256)


def _rope_tables(seq):
    t = jnp.arange(seq)
    half = D_ROPE // 4
    inv = ROPE_THETA ** (-jnp.arange(half, dtype=F32) / half)
    ar = (t // GRID_W).astype(F32)[:, None] * inv[None, :]
    ac = (t % GRID_W).astype(F32)[:, None] * inv[None, :]
    cr, sr, cc, sc = jnp.cos(ar), jnp.sin(ar), jnp.cos(ac), jnp.sin(ac)
    return jnp.concatenate([cr, cr, cc, cc, -sr, sr, -sc, sc], axis=1)


def kernel(x, c, ctx, c_ctx, ada_w, ada_b, norm1_g, norm2_g, ab_w_in, mla_q_norm_g, mla_kv_norm_g, mla_w_q_up, mla_w_kv_up, mla_qn_g, mla_kn_g, mlstm_conv_w, mlstm_gate_b, mlstm_out_norm_g, ab_w_out, na_w_qkv, na_qn_g, na_kn_g, na_rpb, na_w_out, moe_w_rg, moe_b_rg, moe_w_re, moe_b_re, moe_w_gate, moe_w_up, moe_w_down):
    batch, seq, d = x.shape
    ctx_len = ctx.shape[1]
    depth = ada_w.shape[0]
    n_lat = batch * seq
    dims = (n_lat, seq, batch)
    assert d == D_MODEL and batch + 1 <= MOD_ROWS and seq % 512 == 0 and ctx_len == ROW_BLOCK

    xs = jnp.concatenate([x.reshape(n_lat, d), ctx.reshape(batch * ctx_len, d)], axis=0)
    cvec = jnp.concatenate([c, c_ctx[None, :]], axis=0)
    mods_all = ada_mods(cvec, ada_w, ada_b).reshape(depth, MOD_ROWS * 6, 1, d)
    trig = _rope_tables(seq)

    w_gate_t, w_up_t = jnp.swapaxes(moe_w_gate, 2, 3), jnp.swapaxes(moe_w_up, 2, 3)
    kw = dict(batch=batch, seq=seq, ctx_len=ctx_len)
    for layer in range(depth):
        j = layer // 2
        last = layer == depth - 1
        mods = mods_all[layer]
        h = norm_mod(xs, norm1_g[layer], mods, (0, 1), dims)
        if layer % 2 == 0:
            p_all = matmul(h, _ext_w_in(ab_w_in[j]), out_dtype=F32)
            q_all = matmul(p_all, _ext_w_q_up(mla_w_q_up[j]), out_dtype=BF16, a_col=P_CQ // Q_LORA, k=Q_LORA,
                           rms_gain=mla_q_norm_g[j])
            kv_all = matmul(p_all, mla_w_kv_up, layer=j, out_dtype=BF16, a_col=P_CKV // KV_LORA, k=KV_LORA,
                            rms_gain=mla_kv_norm_g[j])
            gq, gk = _ext_gain(mla_qn_g[j]), _ext_gain(mla_kn_g[j])
            mix_a = jnp.zeros((xs.shape[0], W_A), BF16)
            mix_a = mla_attention(q_all, kv_all, p_all, trig, gq, gk, latent_queries=True, out_buf=mix_a, **kw)
            mix_a = mla_attention(q_all, kv_all, p_all, trig, gq, gk, latent_queries=False, out_buf=mix_a, **kw)
            qk = conv_silu(p_all, mlstm_conv_w[j], n_lat=n_lat, seq=seq, ctx_len=ctx_len)
            gates_t = p_all[:, P_GATE:P_GATE + 4 * H_B].T
            h_fwd, h_bwd = mlstm_scan(qk, p_all, gates_t, mlstm_gate_b[j], **kw)
            mix_b = mlstm_finish(h_fwd, h_bwd, p_all, mlstm_out_norm_g[j])
            xs = matmul(mix_a, ab_w_out, layer=j, a2=mix_b, out_dtype=F32, resid=xs, mods=mods, k_gate=2,
                        dims=dims)
        else:
            qkv = matmul(h, na_w_qkv, layer=j, out_dtype=BF16)
            gq, gk = na_qn_g[j].reshape(1, DH_C), na_kn_g[j].reshape(1, DH_C)
            bias = na_bias_rows(na_rpb[j])
            if last:
                mixed = na_attention(qkv, bias, gq, gk, **kw)
            else:
                mixed = jnp.zeros((xs.shape[0], H_C * DH_C), BF16)
                mixed = na_attention(qkv, bias, gq, gk, out_buf=mixed, **kw)
                mixed = na_ctx_attention(qkv, gq, gk, mixed, **kw)
            xs = matmul(mixed, na_w_out, layer=j, out_dtype=F32, resid=xs, mods=mods, k_gate=2, dims=dims)

        w_r = jnp.pad(jnp.concatenate([moe_w_re[layer], moe_w_rg[layer]], axis=1),
                      ((0, 0), (0, 128 - N_EXPERTS - N_GROUPS)))
        b_r = jnp.pad(jnp.concatenate([moe_b_re[layer], moe_b_rg[layer]]), (0, 128 - N_EXPERTS - N_GROUPS))
        hs, route, counts = norm_router(xs, norm2_g[layer], mods, (3, 4), dims, w_r, b_r.reshape(1, 128))
        n_tiles = 2 * xs.shape[0] // MOE_TILE + N_EXPERTS
        pos1, pos2, tile_expert, n_used = moe_slot_plan(route, counts, n_tiles)
        src = moe_slot_source(pos1, pos2, n_tiles * MOE_TILE)
        ys = moe_experts(hs, src, tile_expert, n_used, w_gate_t, w_up_t, moe_w_down, layer)
        xs = moe_combine(ys, pos1, pos2, route, xs, mods, 5, dims)
    return xs[:n_lat].reshape(batch, seq, d) if xs.shape[0] != n_lat else xs.reshape(batch, seq, d)
```

```python
import functools

import numpy as np
import jax
import jax.numpy as jnp
from jax import lax
from jax.experimental import pallas as pl
from jax.experimental.pallas import tpu as pltpu

F32 = jnp.float32
BF16 = jnp.bfloat16

D_MODEL = 4096
GRID_W = 64
H_A = 16
Q_LORA = 1024
KV_LORA = 512
D_NOPE = 128
D_ROPE = 64
D_V_A = 128
H_B = 4
DQK_B = 256
DV_B = 512
MLSTM_CHUNK = 128
M_INIT = -1e30
H_C = 32
DH_C = 128
NA_KH = 8
NA_KW = 16
N_GROUPS = 4
EXPERTS_PER_GROUP = 8
N_EXPERTS = 32
D_FF_EXPERT = 192
ROPE_THETA = 10000.0
NORM_EPS = 1e-6
W_A = H_A * D_V_A
W_B = H_B * DV_B
NEG = -1e30

VMEM_LIMIT_BYTES = 56 * 1024 * 1024
ROW_BLOCK = 256
MOD_ROWS = 8
SLAB_ROWS = 32
SLAB_STRIDE = 40
MOE_TILE = 256

P_QK = 0
P_V = 2048
P_O = 4096
P_CQ = 6144
P_CKV = 7168
P_KR = 7680
P_GATE = 7808
P_WIDTH = 8192


def _cparams(sem):
    return pltpu.CompilerParams(dimension_semantics=sem, vmem_limit_bytes=VMEM_LIMIT_BYTES)


def _dma_cparams(sem):
    return pltpu.CompilerParams(dimension_semantics=sem, vmem_limit_bytes=VMEM_LIMIT_BYTES,
                                disable_bounds_checks=True)


def _mod_row(i, rows_per_block, n_lat, seq, batch):
    return jnp.where(i < n_lat // rows_per_block, i // (seq // rows_per_block), batch)


ADA_ROW_CHUNK = 512


def _ada_body(c_ref, w_ref, b_ref, o_ref, sc_ref, *, n_vec):
    d, tn = w_ref.shape[1], w_ref.shape[2]
    reps = tn // 128

    @pl.when((pl.program_id(0) == 0) & (pl.program_id(1) == 0))
    def _():
        c = c_ref[...]
        sc_ref[...] = c * jax.nn.sigmoid(c)

    def chunk(ci, accs):
        rows = pl.ds(pl.multiple_of(ci * ADA_ROW_CHUNK, ADA_ROW_CHUNK), ADA_ROW_CHUNK)
        w = w_ref[0, rows, :]
        out = []
        for m in range(n_vec):
            cm = jnp.concatenate([sc_ref[m, rows, :]] * reps, axis=1)
            part = (w * cm).reshape(ADA_ROW_CHUNK // 8, 8, tn).sum(axis=0)
            out.append(accs[m] + part)
        return tuple(out)

    accs = lax.fori_loop(0, d // ADA_ROW_CHUNK, chunk, tuple(jnp.zeros((8, tn), F32) for _ in range(n_vec)))
    rows = [jnp.sum(a, axis=0, keepdims=True) for a in accs]
    rows += [jnp.zeros((1, tn), F32)] * (MOD_ROWS - n_vec)
    o_ref[0] = jnp.concatenate(rows, axis=0) + b_ref[0]


def ada_mods(cvec, ada_w, ada_b, tn=512):
    depth, d, n6 = ada_w.shape
    n_vec = cvec.shape[0]
    c_lanes = jnp.broadcast_to(cvec[:, :, None], (n_vec, d, 128))
    return pl.pallas_call(
        functools.partial(_ada_body, n_vec=n_vec),
        out_shape=jax.ShapeDtypeStruct((depth, MOD_ROWS, n6), F32),
        grid=(depth, n6 // tn),
        in_specs=[
            pl.BlockSpec((n_vec, d, 128), lambda l, j: (0, 0, 0)),
            pl.BlockSpec((1, d, tn), lambda l, j: (l, 0, j)),
            pl.BlockSpec((1, 1, tn), lambda l, j: (l, 0, j)),
        ],
        out_specs=pl.BlockSpec((1, MOD_ROWS, tn), lambda l, j: (l, 0, j)),
        scratch_shapes=[pltpu.VMEM((n_vec, d, 128), F32)],
        compiler_params=_cparams(("arbitrary", "arbitrary")),
        name="ada_mods",
    )(c_lanes, ada_w, ada_b.reshape(depth, 1, n6))


def _modulated_norm(x_ref, g_ref, sh_ref, sc_ref):
    x = x_ref[...]
    ms = jnp.mean(x * x, axis=-1, keepdims=True)
    y = x * lax.rsqrt(ms + NORM_EPS) * g_ref[...]
    return y * (1.0 + sc_ref[0]) + sh_ref[0]


def _norm_body(x_ref, g_ref, sh_ref, sc_ref, o_ref):
    o_ref[...] = _modulated_norm(x_ref, g_ref, sh_ref, sc_ref).astype(o_ref.dtype)


def _store_slabs(ref, mat, rows):
    for j in range(SLAB_ROWS):
        ref[pl.ds(j, rows, stride=SLAB_STRIDE), :] = mat[:, j * 128:(j + 1) * 128]
    for j in range(SLAB_ROWS, SLAB_STRIDE):
        ref[pl.ds(j, rows, stride=SLAB_STRIDE), :] = jnp.zeros((rows, 128), mat.dtype)


def _norm_router_body(x_ref, g_ref, sh_ref, sc_ref, wr_ref, br_ref, hs_ref, route_ref, cnt_ref, carry):
    @pl.when(pl.program_id(0) == 0)
    def _():
        carry[...] = jnp.zeros_like(carry)

    h = _modulated_norm(x_ref, g_ref, sh_ref, sc_ref)
    _store_slabs(hs_ref, h, ROW_BLOCK)
    logits = jnp.dot(h, wr_ref[...], precision=lax.Precision.HIGHEST,
                     preferred_element_type=F32) + br_ref[...]
    lane = lax.broadcasted_iota(jnp.int32, logits.shape, 1)
    big = jnp.int32(1 << 20)
    is_g = (lane >= N_EXPERTS) & (lane < N_EXPERTS + N_GROUPS)
    gl = jnp.where(is_g, logits, NEG)
    gmax = jnp.max(gl, axis=-1, keepdims=True)
    g_idx = jnp.min(jnp.where(gl == gmax, lane, big), axis=-1, keepdims=True) - N_EXPERTS
    p_g = 1.0 / jnp.sum(jnp.where(is_g, jnp.exp(gl - gmax), 0.0), axis=-1, keepdims=True)
    lo = g_idx * EXPERTS_PER_GROUP
    in_grp = (lane >= lo) & (lane < lo + EXPERTS_PER_GROUP)
    el = jnp.where(in_grp, logits, NEG)
    e1 = jnp.max(el, axis=-1, keepdims=True)
    i1 = jnp.min(jnp.where(el == e1, lane, big), axis=-1, keepdims=True)
    el2 = jnp.where(lane == i1, NEG, el)
    e2 = jnp.max(el2, axis=-1, keepdims=True)
    i2 = jnp.min(jnp.where(el2 == e2, lane, big), axis=-1, keepdims=True)
    t = jnp.exp(e2 - e1)
    w1 = p_g / (1.0 + t)
    w2 = w1 * t
    onehot = jnp.where((lane == i1) | (lane == i2), 1.0, 0.0)
    t_i = lax.broadcasted_iota(jnp.int32, (ROW_BLOCK, ROW_BLOCK), 0)
    s_i = lax.broadcasted_iota(jnp.int32, (ROW_BLOCK, ROW_BLOCK), 1)
    before = jnp.where(s_i < t_i, 1.0, 0.0).astype(BF16)
    seen = jnp.dot(before, onehot.astype(BF16), preferred_element_type=F32) + carry[...]
    rank1 = jnp.sum(jnp.where(lane == i1, seen, 0.0), axis=-1, keepdims=True)
    rank2 = jnp.sum(jnp.where(lane == i2, seen, 0.0), axis=-1, keepdims=True)
    total = carry[...] + jnp.sum(onehot, axis=0, keepdims=True)
    carry[...] = total
    cnt_ref[...] = jnp.broadcast_to(total, cnt_ref.shape)
    cols = (i1.astype(F32), i2.astype(F32), w1, w2, rank1, rank2)
    route = jnp.zeros(logits.shape, F32)
    for k, v in enumerate(cols):
        route = jnp.where(lane == k, v, route)
    route_ref[...] = route


def _norm_specs(d, mods_k, dims):
    n_lat, seq, batch = dims
    mrow = functools.partial(_mod_row, rows_per_block=ROW_BLOCK, n_lat=n_lat, seq=seq, batch=batch)
    k_shift, k_scale = mods_k
    return [
        pl.BlockSpec((ROW_BLOCK, d), lambda i: (i, 0)),
        pl.BlockSpec((1, d), lambda i: (0, 0)),
        pl.BlockSpec((1, 1, d), lambda i: (mrow(i) * 6 + k_shift, 0, 0)),
        pl.BlockSpec((1, 1, d), lambda i: (mrow(i) * 6 + k_scale, 0, 0)),
    ]


def norm_mod(x, g, mods, mods_k, dims):
    n, d = x.shape
    return pl.pallas_call(
        _norm_body,
        out_shape=jax.ShapeDtypeStruct((n, d), BF16),
        grid=(n // ROW_BLOCK,),
        in_specs=_norm_specs(d, mods_k, dims),
        out_specs=pl.BlockSpec((ROW_BLOCK, d), lambda i: (i, 0)),
        compiler_params=_cparams(("parallel",)),
        name="norm_mod",
    )(x, g.reshape(1, d), mods, mods)


def norm_router(x, g, mods, mods_k, dims, w_r, b_r):
    n, d = x.shape
    assert d == SLAB_ROWS * 128
    return pl.pallas_call(
        _norm_router_body,
        out_shape=(jax.ShapeDtypeStruct((n * SLAB_STRIDE, 128), F32), jax.ShapeDtypeStruct((n, 128), F32),
                   jax.ShapeDtypeStruct((8, 128), F32)),
        grid=(n // ROW_BLOCK,),
        in_specs=_norm_specs(d, mods_k, dims) + [
            pl.BlockSpec((d, 128), lambda i: (0, 0)),
            pl.BlockSpec((1, 128), lambda i: (0, 0)),
        ],
        out_specs=(pl.BlockSpec((ROW_BLOCK * SLAB_STRIDE, 128), lambda i: (i, 0)),
                   pl.BlockSpec((ROW_BLOCK, 128), lambda i: (i, 0)),
                   pl.BlockSpec((8, 128), lambda i: (0, 0))),
        scratch_shapes=[pltpu.VMEM((1, 128), F32)],
        compiler_params=_cparams(("arbitrary",)),
        name="norm_router",
    )(x, g.reshape(1, d), mods, mods, w_r, b_r)


def _mm_body(*refs, rms, gated, k1, two_a):
    it = iter(refs)
    a_ref = next(it)
    a2_ref = next(it) if two_a else None
    g_ref = next(it) if rms else None
    w_ref = next(it)
    res_ref = next(it) if gated else None
    gate_ref = next(it) if gated else None
    o_ref = next(it)
    wbf_ref = next(it)

    @pl.when(pl.program_id(1) == 0)
    def _():
        wbf_ref[...] = w_ref[...].astype(BF16)

    a = a_ref[...]
    if rms:
        af = a.astype(F32)
        ms = jnp.mean(af * af, axis=-1, keepdims=True)
        a = (af * lax.rsqrt(ms + NORM_EPS) * g_ref[...]).astype(BF16)
    if two_a:
        acc = (jnp.dot(a, wbf_ref[:k1, :], preferred_element_type=F32)
               + jnp.dot(a2_ref[...], wbf_ref[k1:, :], preferred_element_type=F32))
    else:
        acc = jnp.dot(a, wbf_ref[...], preferred_element_type=F32)
    if gated:
        acc = res_ref[...] + gate_ref[0] * acc
    o_ref[...] = acc.astype(o_ref.dtype)


def matmul(a, w, *, out_dtype, layer=None, a_col=0, k=None, a2=None, rms_gain=None, resid=None, mods=None,
           k_gate=None, dims=None, tm=512, tn=512):
    m = a.shape[0]
    kw_, nw = w.shape[-2], w.shape[-1]
    two_a = a2 is not None
    k = (a.shape[1] if two_a else kw_) if k is None else k
    assert kw_ == k + (a2.shape[1] if two_a else 0) and m % tm == 0 and nw % tn == 0
    rms = rms_gain is not None
    gated = resid is not None
    in_specs = [pl.BlockSpec((tm, k), lambda j, i: (i, a_col))]
    args = [a]
    if two_a:
        in_specs.append(pl.BlockSpec((tm, a2.shape[1]), lambda j, i: (i, 0)))
        args.append(a2)
    if rms:
        in_specs.append(pl.BlockSpec((1, k), lambda j, i: (0, 0)))
        args.append(rms_gain.reshape(1, k))
    if layer is None:
        in_specs.append(pl.BlockSpec((kw_, tn), lambda j, i: (0, j)))
    else:
        in_specs.append(pl.BlockSpec((None, kw_, tn), lambda j, i: (layer, 0, j)))
    args.append(w)
    if gated:
        n_lat, seq, batch = dims
        mrow = functools.partial(_mod_row, rows_per_block=tm, n_lat=n_lat, seq=seq, batch=batch)
        in_specs.append(pl.BlockSpec((tm, tn), lambda j, i: (i, j)))
        in_specs.append(pl.BlockSpec((1, 1, tn), lambda j, i: (mrow(i) * 6 + k_gate, 0, j)))
        args += [resid, mods]
    return pl.pallas_call(
        functools.partial(_mm_body, rms=rms, gated=gated, k1=k, two_a=two_a),
        out_shape=jax.ShapeDtypeStruct((m, nw), out_dtype),
        grid=(nw // tn, m // tm),
        in_specs=in_specs,
        out_specs=pl.BlockSpec((tm, tn), lambda j, i: (i, j)),
        scratch_shapes=[pltpu.VMEM((kw_, tn), BF16)],
        compiler_params=_cparams(("arbitrary", "arbitrary")),
        name="matmul",
    )(*args)


def _rope_rot(pr, trig):
    if trig is not None:
        pr = pr * trig
    else:
        lane = lax.broadcasted_iota(jnp.int32, pr.shape, 1)
        pr = jnp.where(lane < D_ROPE, pr, 0.0)
    return pr + pltpu.roll(pr, D_ROPE, axis=1)


def _mla_body(*refs, n_lat_keys, n_ctx_keys, rope_q, scale, key_chunk, aliased):
    it = iter(refs)
    q_ref = next(it)
    kvl_ref = next(it) if n_lat_keys else None
    krl_ref = next(it) if n_lat_keys else None
    tkl_ref = next(it) if n_lat_keys else None
    kvc_ref = next(it)
    krc_ref = next(it)
    tq_ref = next(it) if rope_q else None
    gq_ref = next(it)
    gk_ref = next(it)
    if aliased:
        next(it)
    o_ref = next(it)
    kbuf = next(it)
    vbuf = next(it)
    dqk = D_NOPE + D_ROPE

    def prep_keys(kv_ref, kr_ref, trig_ref, off, n):
        gk = gk_ref[...]

        def chunk(c, carry):
            rows = pl.ds(pl.multiple_of(c * key_chunk, key_chunk), key_chunk)
            kv = kv_ref[rows, :]
            kn = kv[:, :D_NOPE].astype(F32)
            kr = kr_ref[rows, :].astype(F32)
            lane = lax.broadcasted_iota(jnp.int32, kr.shape, 1)
            ss = (jnp.sum(kn * kn, axis=-1, keepdims=True)
                  + jnp.sum(jnp.where(lane < D_ROPE, kr * kr, 0.0), axis=-1, keepdims=True))
            r = lax.rsqrt(ss / dqk + NORM_EPS)
            trig = trig_ref[rows, :] if trig_ref is not None else None
            rot = _rope_rot(kr * (r * gk[:, D_NOPE:]), trig)
            orow = pl.ds(pl.multiple_of(off + c * key_chunk, key_chunk), key_chunk)
            kbuf[orow, :D_NOPE] = (kn * (r * gk[:, :D_NOPE])).astype(BF16)
            kbuf[orow, D_NOPE:] = rot.astype(BF16)
            vbuf[orow, :] = kv[:, D_NOPE:].astype(BF16)
            return carry

        lax.fori_loop(0, n // key_chunk, chunk, 0)

    @pl.when(pl.program_id(2) == 0)
    def _():
        if n_lat_keys:
            prep_keys(kvl_ref, krl_ref, tkl_ref, 0, n_lat_keys)
        prep_keys(kvc_ref, krc_ref, None, n_lat_keys, n_ctx_keys)

    gq = gq_ref[...]
    for sb in range(q_ref.shape[0] // MLA_SUB_Q):
        rows = slice(sb * MLA_SUB_Q, (sb + 1) * MLA_SUB_Q)
        q = q_ref[rows, :].astype(F32)
        qn = q[:, :D_NOPE]
        qr = q[:, D_NOPE:]
        lane = lax.broadcasted_iota(jnp.int32, qr.shape, 1)
        ss = (jnp.sum(qn * qn, axis=-1, keepdims=True)
              + jnp.sum(jnp.where(lane < D_ROPE, qr * qr, 0.0), axis=-1, keepdims=True))
        r = lax.rsqrt(ss / dqk + NORM_EPS) * scale
        rot = _rope_rot(qr * (r * gq[:, D_NOPE:]), tq_ref[rows, :] if rope_q else None)
        rot = jnp.where(lane < D_ROPE, rot, 0.0)
        q2 = jnp.concatenate([qn * (r * gq[:, :D_NOPE]), rot], axis=1).astype(BF16)
        s = lax.dot_general(q2, kbuf[...], (((1,), (1,)), ((), ())), preferred_element_type=F32)
        m = jnp.max(s, axis=-1, keepdims=True)
        p = jnp.exp2(s - m)
        l = jnp.sum(p, axis=-1, keepdims=True)
        o = jnp.dot(p.astype(BF16), vbuf[...], preferred_element_type=F32)
        o_ref[rows, :] = (o / l).astype(o_ref.dtype)


MLA_SUB_Q = 256


def mla_attention(q_all, kv_all, p_all, trig, gq, gk, *, batch, seq, ctx_len, latent_queries, out_buf=None):
    n_lat = batch * seq
    kr_blk = P_KR // 128
    scale = (D_NOPE + D_ROPE) ** -0.5 * float(np.log2(np.e))
    if latent_queries:
        tq = 2 * MLA_SUB_Q
        nq, n_lat_keys, q_row0 = seq, seq, 0
    else:
        tq = ctx_len
        nq, n_lat_keys, q_row0 = ctx_len, 0, n_lat // tq
    assert tq % MLA_SUB_Q == 0 and nq % tq == 0
    nqb = nq // tq
    ctx_blk0 = n_lat // ctx_len
    in_specs = [pl.BlockSpec((tq, 256), lambda b, h, i: (q_row0 + b * nqb + i, h))]
    args = [q_all]
    if latent_queries:
        in_specs += [
            pl.BlockSpec((seq, 256), lambda b, h, i: (b, h)),
            pl.BlockSpec((seq, 128), lambda b, h, i: (b, kr_blk)),
            pl.BlockSpec((seq, 128), lambda b, h, i: (0, 0)),
        ]
        args += [kv_all, p_all, trig]
    in_specs += [
        pl.BlockSpec((ctx_len, 256), lambda b, h, i: (ctx_blk0 + b, h)),
        pl.BlockSpec((ctx_len, 128), lambda b, h, i: (ctx_blk0 + b, kr_blk)),
    ]
    args += [kv_all, p_all]
    if latent_queries:
        in_specs.append(pl.BlockSpec((tq, 128), lambda b, h, i: (i, 0)))
        args.append(trig)
    in_specs += [pl.BlockSpec((1, 256), lambda b, h, i: (0, 0))] * 2
    args += [gq, gk]
    aliases = {}
    if out_buf is not None:
        in_specs.append(pl.BlockSpec(memory_space=pl.ANY))
        aliases = {len(args): 0}
        args.append(out_buf)
    nk = n_lat_keys + ctx_len
    body = functools.partial(_mla_body, n_lat_keys=n_lat_keys, n_ctx_keys=ctx_len, rope_q=latent_queries,
                             scale=scale, key_chunk=min(256, ctx_len), aliased=out_buf is not None)
    return pl.pallas_call(
        body,
        out_shape=jax.ShapeDtypeStruct((q_all.shape[0], W_A), BF16),
        grid=(batch, H_A, nqb),
        in_specs=in_specs,
        out_specs=pl.BlockSpec((tq, D_V_A), lambda b, h, i: (q_row0 + b * nqb + i, h)),
        scratch_shapes=[pltpu.VMEM((nk, 256), BF16), pltpu.VMEM((nk, D_V_A), BF16)],
        input_output_aliases=aliases,
        compiler_params=_cparams(("arbitrary", "arbitrary", "arbitrary")),
        name="mla_attention",
    )(*args)


def _conv_silu_body(x_ref, prev_ref, next_ref, w_ref, o_ref, *, lat_blocks, seq_blocks, ctx_blocks):
    i = pl.program_id(0)
    j = jnp.where(i < lat_blocks, i % seq_blocks, (i - lat_blocks) % ctx_blocks)
    nb = jnp.where(i < lat_blocks, seq_blocks, ctx_blocks)
    x = x_ref[...]
    rows = x.shape[0]
    row = lax.broadcasted_iota(jnp.int32, x.shape, 0)
    prev_row = jnp.where(j == 0, 0.0, prev_ref[7:8, :])
    next_row = jnp.where(j == nb - 1, 0.0, next_ref[0:1, :])
    xp = jnp.where(row == 0, prev_row, pltpu.roll(x, 1, axis=0))
    xn = jnp.where(row == rows - 1, next_row, pltpu.roll(x, rows - 1, axis=0))
    w = w_ref[...]
    y = xp * w[0:1, :] + x * w[1:2, :] + xn * w[2:3, :]
    y = y * jax.nn.sigmoid(y)
    half = y.shape[1] // 2
    o_ref[:, :half] = y[:, :half].astype(o_ref.dtype)
    o_ref[:, half:] = (y[:, half:] * (DQK_B ** -0.5)).astype(o_ref.dtype)


def conv_silu(p_all, conv_w, *, n_lat, seq, ctx_len):
    n = p_all.shape[0]
    width = 2 * H_B * DQK_B
    rb = ROW_BLOCK
    nblk = n // rb
    sub = rb // 8
    last8 = n // 8 - 1
    body = functools.partial(_conv_silu_body, lat_blocks=n_lat // rb, seq_blocks=seq // rb,
                             ctx_blocks=ctx_len // rb)
    return pl.pallas_call(
        body,
        out_shape=jax.ShapeDtypeStruct((n, width), BF16),
        grid=(nblk,),
        in_specs=[
            pl.BlockSpec((rb, width), lambda i: (i, P_QK // width)),
            pl.BlockSpec((8, width), lambda i: (jnp.maximum(i * sub - 1, 0), P_QK // width)),
            pl.BlockSpec((8, width), lambda i: (jnp.minimum((i + 1) * sub, last8), P_QK // width)),
            pl.BlockSpec((3, width), lambda i: (0, 0)),
        ],
        out_specs=pl.BlockSpec((rb, width), lambda i: (i, 0)),
        compiler_params=_cparams(("parallel",)),
        name="conv_silu",
    )(p_all, p_all, p_all, conv_w)


def _log_sigmoid(x):
    return jnp.minimum(x, 0.0) - jnp.log(1.0 + jnp.exp(-jnp.abs(x)))


def _mlstm_body(qf, kf, vf, gcf, grf, qb, kb, vb, gcb, grb, bc_ref, br_ref, of_ref, ob_ref, c_st, n_st, m_st):
    @pl.when(pl.program_id(2) == 0)
    def _():
        c_st[...] = jnp.zeros_like(c_st)
        n_st[...] = jnp.zeros_like(n_st)
        m_st[...] = jnp.full_like(m_st, M_INIT)

    _mlstm_chunk(0, qf, kf, vf, gcf, grf, bc_ref, br_ref, of_ref, c_st.at[0], n_st.at[0], m_st.at[0])
    _mlstm_chunk(1, qb, kb, vb, gcb, grb, bc_ref, br_ref, ob_ref, c_st.at[1], n_st.at[1], m_st.at[1])


def _mlstm_chunk(direction, q_ref, k_ref, v_ref, gc_ref, gr_ref, bc_ref, br_ref, o_ref, c_st, n_st, m_st):
    h = pl.program_id(1)
    L = MLSTM_CHUNK
    hi = lax.Precision.HIGHEST
    gi = direction * (2 * H_B) + h
    gcol = gc_ref[...] + bc_ref[...]
    lane = lax.broadcasted_iota(jnp.int32, gcol.shape, 1)
    li_c = jnp.sum(jnp.where(lane == gi, gcol, 0.0), axis=1, keepdims=True)
    lf_c = _log_sigmoid(jnp.sum(jnp.where(lane == gi + H_B, gcol, 0.0), axis=1, keepdims=True))
    grow = gr_ref[...] + br_ref[...]
    sub = lax.broadcasted_iota(jnp.int32, grow.shape, 0)
    li_r = jnp.sum(jnp.where(sub == gi, grow, 0.0), axis=0, keepdims=True)
    lf_r = _log_sigmoid(jnp.sum(jnp.where(sub == gi + H_B, grow, 0.0), axis=0, keepdims=True))

    t_i = lax.broadcasted_iota(jnp.int32, (L, L), 0)
    s_i = lax.broadcasted_iota(jnp.int32, (L, L), 1)
    sgn = 1 - 2 * direction
    incl = (s_i - t_i) * sgn <= 0
    incl_f = incl.astype(F32)
    incl_t = ((t_i - s_i) * sgn <= 0).astype(F32)
    bcum_c = jnp.dot(incl_f, jnp.broadcast_to(lf_c, (L, L)), precision=hi, preferred_element_type=F32)
    bcum_r = jnp.dot(jnp.broadcast_to(lf_r, (8, L)), incl_t, precision=hi, preferred_element_type=F32)[0:1, :]
    b_last = jnp.sum(lf_r, axis=1, keepdims=True)

    m_prev = m_st[...]
    a_c = bcum_c[:, 0:1] + m_prev
    dlog = jnp.where(incl, bcum_c - bcum_r + li_r, -jnp.inf)
    m_t = jnp.maximum(a_c, jnp.max(dlog, axis=1, keepdims=True))
    w_intra = jnp.exp(dlog - m_t)
    w_inter = jnp.exp(a_c - m_t)

    q = q_ref[...]
    k = k_ref[...]
    v = v_ref[...].astype(BF16)
    c_prev = c_st[...]
    n_prev = n_st[...]
    s = lax.dot_general(q, k, (((1,), (1,)), ((), ())), preferred_element_type=F32) * w_intra
    num = (w_inter * jnp.dot(q, c_prev.astype(BF16), preferred_element_type=F32)
           + jnp.dot(s.astype(BF16), v, preferred_element_type=F32))
    den = (w_inter * jnp.sum(q.astype(F32) * n_prev, axis=1, keepdims=True)
           + jnp.sum(s, axis=1, keepdims=True))
    o_ref[...] = num / jnp.maximum(jnp.abs(den), jnp.exp(-m_t))

    g_r = b_last - bcum_r + li_r
    g_c = b_last - bcum_c[:, 0:1] + li_c
    m_new = jnp.maximum(b_last + m_prev, jnp.max(g_r, axis=1, keepdims=True))
    decay = jnp.exp(b_last + m_prev - m_new)
    kw = k.astype(F32) * jnp.exp(g_c - m_new)
    c_st[...] = decay * c_prev + lax.dot_general(kw.astype(BF16), v, (((0,), (0,)), ((), ())),
                                                 preferred_element_type=F32)
    n_st[...] = decay * n_prev + jnp.sum(kw, axis=0, keepdims=True)
    m_st[...] = m_new


def mlstm_scan(qk, p_all, gates_t, gate_b, *, batch, seq, ctx_len):
    n = qk.shape[0]
    L = MLSTM_CHUNK
    cc, lc = ctx_len // L, seq // L
    n_lat_blk = batch * lc

    def rowblk(b, d, c):
        ctx_blk = n_lat_blk + b * cc + (c if d == 0 else cc - 1 - c)
        lat_blk = b * lc + (c - cc if d == 0 else lc - 1 - (c - cc))
        return jnp.where(c < cc, ctx_blk, lat_blk)

    def chunk_specs(d):
        return [
            pl.BlockSpec((L, DQK_B), lambda b, h, c: (rowblk(b, d, c), h)),
            pl.BlockSpec((L, DQK_B), lambda b, h, c: (rowblk(b, d, c), H_B + h)),
            pl.BlockSpec((L, DV_B), lambda b, h, c: (rowblk(b, d, c), P_V // DV_B + h)),
            pl.BlockSpec((L, 128), lambda b, h, c: (rowblk(b, d, c), P_GATE // 128)),
            pl.BlockSpec((4 * H_B, L), lambda b, h, c: (0, rowblk(b, d, c))),
        ]

    def out_spec(d):
        return pl.BlockSpec((L, DV_B), lambda b, h, c: (rowblk(b, d, c), h))

    gb_col = jnp.zeros((1, 128), F32).at[0, :4 * H_B].set(gate_b)
    gb_row = gate_b.reshape(4 * H_B, 1)
    chunk_args = [qk, qk, p_all, p_all, gates_t]
    return pl.pallas_call(
        _mlstm_body,
        out_shape=(jax.ShapeDtypeStruct((n, W_B), F32), jax.ShapeDtypeStruct((n, W_B), F32)),
        grid=(batch, H_B, cc + lc),
        in_specs=chunk_specs(0) + chunk_specs(1) + [
            pl.BlockSpec((1, 128), lambda b, h, c: (0, 0)),
            pl.BlockSpec((4 * H_B, 1), lambda b, h, c: (0, 0)),
        ],
        out_specs=(out_spec(0), out_spec(1)),
        scratch_shapes=[pltpu.VMEM((2, DQK_B, DV_B), F32), pltpu.VMEM((2, 1, DQK_B), F32),
                        pltpu.VMEM((2, 1, 1), F32)],
        compiler_params=_cparams(("arbitrary",) * 3),
        name="mlstm_scan",
    )(*chunk_args, *chunk_args, gb_col, gb_row)


def _mlstm_finish_body(hf_ref, hb_ref, o_ref, g_ref, out_ref):
    hs = hf_ref[...] + hb_ref[...]
    g = g_ref[...]
    og = o_ref[...]
    for hd in range(H_B):
        sl = slice(hd * DV_B, (hd + 1) * DV_B)
        x = hs[:, sl]
        ms = jnp.mean(x * x, axis=-1, keepdims=True)
        hn = x * lax.rsqrt(ms + NORM_EPS) * g[:, sl]
        out_ref[:, sl] = (jax.nn.sigmoid(og[:, sl]) * hn).astype(out_ref.dtype)


def mlstm_finish(h_fwd, h_bwd, p_all, out_norm_g):
    n = h_fwd.shape[0]
    rb = ROW_BLOCK
    return pl.pallas_call(
        _mlstm_finish_body,
        out_shape=jax.ShapeDtypeStruct((n, W_B), BF16),
        grid=(n // rb,),
        in_specs=[
            pl.BlockSpec((rb, W_B), lambda i: (i, 0)),
            pl.BlockSpec((rb, W_B), lambda i: (i, 0)),
            pl.BlockSpec((rb, W_B), lambda i: (i, P_O // W_B)),
            pl.BlockSpec((1, W_B), lambda i: (0, 0)),
        ],
        out_specs=pl.BlockSpec((rb, W_B), lambda i: (i, 0)),
        compiler_params=_cparams(("parallel",)),
        name="mlstm_finish",
    )(h_fwd, h_bwd, p_all, out_norm_g.reshape(1, W_B))


NA_ROWS_PER_STEP = 4
NA_TILES_PER_STEP = 4


def _na_geometry(rows_n):
    r_t = NA_ROWS_PER_STEP
    kh = min(NA_KH, rows_n)
    win = min(r_t + kh - 1, rows_n)
    tiles = rows_n // r_t
    starts, sigs, cls = [], [], []
    for t in range(tiles):
        rq0 = t * r_t
        start = int(np.clip(rq0 - kh // 2, 0, rows_n - win))
        r0 = np.clip(np.arange(rq0, rq0 + r_t) - kh // 2, 0, rows_n - kh)
        sig = (start - rq0, tuple((r0 - rq0).tolist()))
        if sig not in sigs:
            sigs.append(sig)
        starts.append(start)
        cls.append(sigs.index(sig))
    return kh, win, np.array(starts, np.int32), np.array(cls, np.int32), sigs


def na_bias_rows(rpb):
    cols = np.arange(GRID_W)
    c0 = np.clip(cols - NA_KW // 2, 0, GRID_W - NA_KW)
    col_valid = (cols[None, :] >= c0[:, None]) & (cols[None, :] < c0[:, None] + NA_KW)
    col_off = cols[None, :] - cols[:, None] + (NA_KW - 1)
    c_sel = (col_valid[:, :, None] & (col_off[:, :, None] == np.arange(2 * NA_KW - 1))).astype(np.float32)
    t = jnp.einsum('hrs,cds->hrcd', rpb, c_sel, precision=lax.Precision.HIGHEST)
    t = t + np.where(col_valid, 0.0, NEG).astype(np.float32)[None, None]
    return jnp.concatenate([t, t], axis=-1)


def _na_bias_plan(rows_n):
    r_t = NA_ROWS_PER_STEP
    kh, win, _, _, sigs = _na_geometry(rows_n)
    plan = []
    for (dstart, dr0) in sigs:
        cls_plan = []
        for a in range(r_t):
            row = []
            for j in range(win):
                krow = dstart + j
                ok = dr0[a] <= krow < dr0[a] + kh
                row.append(krow - a + (NA_KH - 1) if ok else None)
            cls_plan.append(row)
        plan.append(cls_plan)
    return plan


def _na_build_bias(rows_ref, btab, plan):
    neg = jnp.full((GRID_W, 2 * GRID_W), NEG, F32)
    lane = lax.broadcasted_iota(jnp.int32, (GRID_W, 2 * GRID_W), 1)

    def piece(ro):
        return neg if ro is None else rows_ref[0, ro]

    for k, cls_plan in enumerate(plan):
        for a, row in enumerate(cls_plan):
            rs = slice(a * GRID_W, (a + 1) * GRID_W)
            for j in range(0, len(row) - 1, 2):
                both = jnp.where(lane < GRID_W, piece(row[j]), piece(row[j + 1]))
                btab[k, rs, j * GRID_W:(j + 2) * GRID_W] = both
            if len(row) % 2:
                j = len(row) - 1
                btab[k, rs, j * GRID_W:(j + 1) * GRID_W] = piece(row[j])[:, :GRID_W]


def _head_rmsnorm(x, g):
    xf = x.astype(F32)
    ms = jnp.mean(xf * xf, axis=-1, keepdims=True)
    return xf * lax.rsqrt(ms + NORM_EPS) * g


def _na_body(start_ref, cls_ref, q_ref, k_ref, v_ref, kc_ref, vc_ref, brow_ref, gq_ref, gk_ref, *rest,
             seq, ctx_len, win_rows, scale, bias_plan):
    o_ref, kn, knc, btab = rest[-4:]
    i = pl.program_id(2)
    chunk = min(512, seq)

    @pl.when(i == 0)
    def _():
        _na_build_bias(brow_ref, btab, bias_plan)
        gk = gk_ref[...]

        def body(c, carry):
            rows = pl.ds(pl.multiple_of(c * chunk, chunk), chunk)
            kn[rows, :] = _head_rmsnorm(k_ref[rows, :], gk).astype(BF16)
            return carry

        lax.fori_loop(0, seq // chunk, body, 0)
        knc[...] = _head_rmsnorm(kc_ref[...], gk).astype(BF16)

    nt = (((1,), (1,)), ((), ()))
    tq = NA_ROWS_PER_STEP * GRID_W
    for tl in range(NA_TILES_PER_STEP):
        rows = slice(tl * tq, (tl + 1) * tq)
        q = (_head_rmsnorm(q_ref[rows, :], gq_ref[...]) * scale).astype(BF16)
        krow0 = pl.multiple_of(start_ref[i * NA_TILES_PER_STEP + tl] * GRID_W, GRID_W)
        kw = kn[pl.ds(krow0, win_rows), :]
        vw = v_ref[pl.ds(krow0, win_rows), :]
        s = lax.dot_general(q, kw, nt, preferred_element_type=F32) + btab[cls_ref[i * NA_TILES_PER_STEP + tl]]
        sc = lax.dot_general(q, knc[...], nt, preferred_element_type=F32)
        m = jnp.maximum(jnp.max(s, axis=-1, keepdims=True), jnp.max(sc, axis=-1, keepdims=True))
        p = jnp.exp(s - m)
        pc = jnp.exp(sc - m)
        l = jnp.sum(p, axis=-1, keepdims=True) + jnp.sum(pc, axis=-1, keepdims=True)
        o = (jnp.dot(p.astype(BF16), vw, preferred_element_type=F32)
             + jnp.dot(pc.astype(BF16), vc_ref[...], preferred_element_type=F32))
        o_ref[rows, :] = (o / l).astype(o_ref.dtype)


def na_attention(qkv, bias_rows, gq, gk, *, batch, seq, ctx_len, out_buf=None):
    rows_n = seq // GRID_W
    r_t = NA_ROWS_PER_STEP
    _, win, starts, cls, _ = _na_geometry(rows_n)
    tps = NA_TILES_PER_STEP
    tq = tps * r_t * GRID_W
    tiles = rows_n // (r_t * tps)
    assert rows_n % (r_t * tps) == 0
    n_lat = batch * seq
    ctx_blk0 = n_lat // ctx_len
    plan = _na_bias_plan(rows_n)
    body = functools.partial(_na_body, seq=seq, ctx_len=ctx_len, win_rows=win * GRID_W, scale=DH_C ** -0.5,
                             bias_plan=plan)
    n_ro = 2 * NA_KH - 1
    in_specs = [
        pl.BlockSpec((tq, DH_C), lambda b, h, i, st, cl: (b * tiles + i, h)),
        pl.BlockSpec((seq, DH_C), lambda b, h, i, st, cl: (b, H_C + h)),
        pl.BlockSpec((seq, DH_C), lambda b, h, i, st, cl: (b, 2 * H_C + h)),
        pl.BlockSpec((ctx_len, DH_C), lambda b, h, i, st, cl: (ctx_blk0 + b, H_C + h)),
        pl.BlockSpec((ctx_len, DH_C), lambda b, h, i, st, cl: (ctx_blk0 + b, 2 * H_C + h)),
        pl.BlockSpec((1, n_ro, GRID_W, 2 * GRID_W), lambda b, h, i, st, cl: (h, 0, 0, 0)),
        pl.BlockSpec((1, DH_C), lambda b, h, i, st, cl: (0, 0)),
        pl.BlockSpec((1, DH_C), lambda b, h, i, st, cl: (0, 0)),
    ]
    args = [jnp.asarray(starts), jnp.asarray(cls), qkv, qkv, qkv, qkv, qkv, bias_rows, gq, gk]
    aliases = {}
    out_rows = n_lat
    if out_buf is not None:
        in_specs.append(pl.BlockSpec(memory_space=pl.ANY))
        aliases = {len(args): 0}
        args.append(out_buf)
        out_rows = out_buf.shape[0]
    grid_spec = pltpu.PrefetchScalarGridSpec(
        num_scalar_prefetch=2,
        grid=(batch, H_C, tiles),
        in_specs=in_specs,
        out_specs=pl.BlockSpec((tq, DH_C), lambda b, h, i, st, cl: (b * tiles + i, h)),
        scratch_shapes=[pltpu.VMEM((seq, DH_C), BF16), pltpu.VMEM((ctx_len, DH_C), BF16),
                        pltpu.VMEM((len(plan), r_t * GRID_W, win * GRID_W), F32)],
    )
    return pl.pallas_call(
        body,
        out_shape=jax.ShapeDtypeStruct((out_rows, H_C * DH_C), BF16),
        grid_spec=grid_spec,
        input_output_aliases=aliases,
        compiler_params=_cparams(("arbitrary", "arbitrary", "arbitrary")),
        name="na_attention",
    )(*args)


def _ctx_attn_body(q_ref, k_ref, v_ref, gq_ref, gk_ref, buf_ref, o_ref, *, scale):
    del buf_ref
    q = (_head_rmsnorm(q_ref[...], gq_ref[...]) * scale).astype(BF16)
    k = _head_rmsnorm(k_ref[...], gk_ref[...]).astype(BF16)
    s = lax.dot_general(q, k, (((1,), (1,)), ((), ())), preferred_element_type=F32)
    m = jnp.max(s, axis=-1, keepdims=True)
    p = jnp.exp(s - m)
    l = jnp.sum(p, axis=-1, keepdims=True)
    o = jnp.dot(p.astype(BF16), v_ref[...], preferred_element_type=F32)
    o_ref[...] = (o / l).astype(o_ref.dtype)


def na_ctx_attention(qkv, gq, gk, out_buf, *, batch, seq, ctx_len):
    ctx_blk0 = batch * seq // ctx_len
    return pl.pallas_call(
        functools.partial(_ctx_attn_body, scale=DH_C ** -0.5),
        out_shape=jax.ShapeDtypeStruct(out_buf.shape, out_buf.dtype),
        grid=(batch, H_C),
        in_specs=[
            pl.BlockSpec((ctx_len, DH_C), lambda b, h: (ctx_blk0 + b, h)),
            pl.BlockSpec((ctx_len, DH_C), lambda b, h: (ctx_blk0 + b, H_C + h)),
            pl.BlockSpec((ctx_len, DH_C), lambda b, h: (ctx_blk0 + b, 2 * H_C + h)),
            pl.BlockSpec((1, DH_C), lambda b, h: (0, 0)),
            pl.BlockSpec((1, DH_C), lambda b, h: (0, 0)),
            pl.BlockSpec(memory_space=pl.ANY),
        ],
        out_specs=pl.BlockSpec((ctx_len, DH_C), lambda b, h: (ctx_blk0 + b, h)),
        input_output_aliases={5: 0},
        compiler_params=_cparams(("parallel", "parallel")),
        name="na_ctx_attention",
    )(qkv, qkv, qkv, gq, gk, out_buf)


def _slab_copy(src_ref, src_tok, dst_ref, dst_tok, sem):
    def rows(tok):
        off = tok * SLAB_STRIDE
        return pl.ds(off if isinstance(off, int) else pl.multiple_of(off, 8), SLAB_ROWS)

    return pltpu.make_async_copy(src_ref.at[rows(src_tok), :], dst_ref.at[rows(dst_tok), :], sem)


def _load_slabs(ref, rows, dtype):
    return jnp.concatenate([ref[pl.ds(j, rows, stride=SLAB_STRIDE), :].astype(dtype) for j in range(SLAB_ROWS)],
                           axis=1)


def moe_slot_plan(route, counts, n_tiles):
    cnt = counts[0, :N_EXPERTS].astype(jnp.int32)
    padded = (cnt + MOE_TILE - 1) // MOE_TILE * MOE_TILE
    ends = jnp.cumsum(padded)
    base = ends - padded
    n_used = ends[-1] // MOE_TILE
    tile_start = jnp.arange(n_tiles, dtype=jnp.int32) * MOE_TILE
    last_start = jnp.maximum(ends[-1] - MOE_TILE, 0)
    tile_expert = jnp.searchsorted(ends, jnp.minimum(tile_start, last_start), side='right').astype(jnp.int32)
    tile_expert = jnp.minimum(tile_expert, N_EXPERTS - 1)
    ids = route[:, 0:2].astype(jnp.int32)
    pos = jnp.take(base, ids, axis=0) + route[:, 4:6].astype(jnp.int32)
    return pos[:, 0], pos[:, 1], tile_expert, n_used.reshape(1).astype(jnp.int32)


def _slot_source_body(p1_ref, p2_ref, src_ref, *, n_tok):
    def clear(i, c):
        src_ref[i] = 0
        return c

    lax.fori_loop(0, src_ref.shape[0], clear, 0, unroll=8)

    def fill(n, c):
        src_ref[p1_ref[n]] = n
        src_ref[p2_ref[n]] = n
        return c

    lax.fori_loop(0, n_tok, fill, 0, unroll=4)


def moe_slot_source(pos1, pos2, n_slots):
    smem = pl.BlockSpec(memory_space=pltpu.SMEM)
    return pl.pallas_call(
        functools.partial(_slot_source_body, n_tok=pos1.shape[0]),
        out_shape=jax.ShapeDtypeStruct((n_slots,), jnp.int32),
        in_specs=[smem, smem],
        out_specs=smem,
        name="moe_slot_source",
    )(pos1, pos2)


def _gather_slabs(src_ref, tok_ref, tok0, dst_ref, sem, n):
    def issue(r, c):
        _slab_copy(src_ref, tok_ref[tok0 + r], dst_ref, r, sem).start()
        return c

    lax.fori_loop(0, n, issue, 0, unroll=8)


def _wait_slabs(src_ref, dst_ref, sem, n):
    rows = pl.ds(0, n * SLAB_ROWS)
    pltpu.make_async_copy(src_ref.at[rows, :], dst_ref.at[rows, :], sem).wait()


def _moe_expert_body(te_ref, nu_ref, src_ref, hs_ref, wg_ref, wu_ref, wd_ref, o_ref, xbuf, wg_bf, wu_bf, wd_bf,
                     sems):
    t = pl.program_id(0)
    tm = MOE_TILE
    slot = t % 2

    @pl.when(t < nu_ref[0])
    def _():
        @pl.when(t == 0)
        def _():
            _gather_slabs(hs_ref, src_ref, 0, xbuf.at[0], sems.at[0], tm)

        @pl.when(t + 1 < nu_ref[0])
        def _():
            _gather_slabs(hs_ref, src_ref, (t + 1) * tm, xbuf.at[1 - slot], sems.at[1 - slot], tm)

        @pl.when((t == 0) | (te_ref[t] != te_ref[jnp.maximum(t - 1, 0)]))
        def _():
            wg_bf[...] = wg_ref[...].astype(BF16)
            wu_bf[...] = wu_ref[...].astype(BF16)
            wd_bf[...] = wd_ref[...].astype(BF16)

        _wait_slabs(hs_ref, xbuf.at[slot], sems.at[slot], tm)
        x = _load_slabs(xbuf.at[slot], tm, BF16)
        nt = (((1,), (1,)), ((), ()))
        a = lax.dot_general(x, wg_bf[...], nt, preferred_element_type=F32)
        u = lax.dot_general(x, wu_bf[...], nt, preferred_element_type=F32)
        y = (a * jax.nn.sigmoid(a) * u).astype(BF16)
        _store_slabs(o_ref, jnp.dot(y, wd_bf[...], preferred_element_type=F32), tm)

    @pl.when(t >= nu_ref[0])
    def _():
        o_ref[...] = jnp.zeros_like(o_ref)


def moe_experts(hs, src, tile_expert, n_used, w_gate_t, w_up_t, w_down, layer):
    n_tiles = tile_expert.shape[0]
    f, d = w_gate_t.shape[-2:]
    blk = MOE_TILE * SLAB_STRIDE
    wspec = pl.BlockSpec((None, None, f, d), lambda t, te, nu, src: (layer, te[t], 0, 0))
    grid_spec = pltpu.PrefetchScalarGridSpec(
        num_scalar_prefetch=3,
        grid=(n_tiles,),
        in_specs=[pl.BlockSpec(memory_space=pl.ANY), wspec, wspec, wspec],
        out_specs=pl.BlockSpec((blk, 128), lambda t, te, nu, src: (t, 0)),
        scratch_shapes=[pltpu.VMEM((2, blk, 128), F32), pltpu.VMEM((f, d), BF16), pltpu.VMEM((f, d), BF16),
                        pltpu.VMEM((f, d), BF16), pltpu.SemaphoreType.DMA((2,))],
    )
    return pl.pallas_call(
        _moe_expert_body,
        out_shape=jax.ShapeDtypeStruct((n_tiles * blk, 128), F32),
        grid_spec=grid_spec,
        compiler_params=_dma_cparams(("arbitrary",)),
        name="moe_experts",
    )(tile_expert, n_used, src, hs, w_gate_t, w_up_t, w_down)


def _moe_combine_body(p1_ref, p2_ref, ys_ref, route_ref, res_ref, gate_ref, o_ref, abuf, bbuf, sems):
    i = pl.program_id(0)
    rb = ROW_BLOCK
    slot = i % 2

    def gather(blk, s):
        _gather_slabs(ys_ref, p1_ref, blk * rb, abuf.at[s], sems.at[0, s], rb)
        _gather_slabs(ys_ref, p2_ref, blk * rb, bbuf.at[s], sems.at[1, s], rb)

    @pl.when(i == 0)
    def _():
        gather(0, 0)

    @pl.when(i + 1 < pl.num_programs(0))
    def _():
        gather(i + 1, 1 - slot)

    route = route_ref[...]
    lane = lax.broadcasted_iota(jnp.int32, route.shape, 1)
    w1 = jnp.sum(jnp.where(lane == 2, route, 0.0), axis=1, keepdims=True)
    w2 = jnp.sum(jnp.where(lane == 3, route, 0.0), axis=1, keepdims=True)
    _wait_slabs(ys_ref, abuf.at[slot], sems.at[0, slot], rb)
    _wait_slabs(ys_ref, bbuf.at[slot], sems.at[1, slot], rb)
    for j in range(SLAB_ROWS):
        sl = slice(j * 128, (j + 1) * 128)
        rows = pl.ds(j, rb, stride=SLAB_STRIDE)
        mix = w1 * abuf[slot, rows, :] + w2 * bbuf[slot, rows, :]
        o_ref[:, sl] = res_ref[:, sl] + gate_ref[0][:, sl] * mix


def moe_combine(ys, pos1, pos2, route, resid, mods, k_gate, dims):
    n, d = route.shape[0], resid.shape[1]
    n_lat, seq, batch = dims
    rb = ROW_BLOCK
    mrow = functools.partial(_mod_row, rows_per_block=rb, n_lat=n_lat, seq=seq, batch=batch)
    grid_spec = pltpu.PrefetchScalarGridSpec(
        num_scalar_prefetch=2,
        grid=(n // rb,),
        in_specs=[
            pl.BlockSpec(memory_space=pl.ANY),
            pl.BlockSpec((rb, 128), lambda i, p1, p2: (i, 0)),
            pl.BlockSpec((rb, d), lambda i, p1, p2: (i, 0)),
            pl.BlockSpec((1, 1, d), lambda i, p1, p2: (mrow(i) * 6 + k_gate, 0, 0)),
        ],
        out_specs=pl.BlockSpec((rb, d), lambda i, p1, p2: (i, 0)),
        scratch_shapes=[pltpu.VMEM((2, rb * SLAB_STRIDE, 128), F32), pltpu.VMEM((2, rb * SLAB_STRIDE, 128), F32),
                        pltpu.SemaphoreType.DMA((2, 2))],
    )
    return pl.pallas_call(
        _moe_combine_body,
        out_shape=jax.ShapeDtypeStruct((n, d), F32),
        grid_spec=grid_spec,
        compiler_params=_dma_cparams(("arbitrary",)),
        name="moe_combine",
    )(pos1, pos2, ys, route, resid, mods)


def _swap16(a):
    lead = a.shape[:-1]
    return jnp.flip(a.reshape(*lead, D_ROPE // 32, 2, 16), axis=-2).reshape(*lead, D_ROPE)


W_IN_TILE = 512
_C_KR = Q_LORA + KV_LORA
_C_QB = _C_KR + D_ROPE
_C_G = _C_QB + 2 * H_B * DQK_B + 2 * W_B


def _w_in_tail_t(wt_all, j):
    wt = wt_all[j]
    kr = wt[_C_KR:_C_QB]
    kr_sw = jnp.flip(kr.reshape(D_ROPE // 32, 2, 16, -1), axis=1).reshape(D_ROPE, -1)
    gates = wt[_C_G:]
    pad = W_IN_TILE - 2 * D_ROPE - gates.shape[0]
    return jnp.concatenate([kr, kr_sw, jnp.pad(gates, ((0, pad), (0, 0)))], axis=0)


def _win_body(h_ref, wt_ref, tail_ref, o_ref, wbf_ref, *, n_main):
    j = pl.program_id(0)

    @pl.when((pl.program_id(1) == 0) & (j < n_main))
    def _():
        wbf_ref[...] = wt_ref[0].astype(BF16)

    @pl.when((pl.program_id(1) == 0) & (j >= n_main))
    def _():
        wbf_ref[...] = tail_ref[...].astype(BF16)

    o_ref[...] = lax.dot_general(h_ref[...], wbf_ref[...], (((1,), (1,)), ((), ())),
                                 preferred_element_type=F32).astype(o_ref.dtype)


def input_projection(h, wt_all, layer, tm=512):
    m, d = h.shape
    tn = W_IN_TILE
    n_a = (_C_G - _C_QB) // tn
    n_b = _C_KR // tn
    n_main = n_a + n_b
    assert (_C_G - _C_QB) % tn == 0 and _C_KR % tn == 0 and (n_main + 1) * tn == P_WIDTH and _C_QB % 8 == 0

    def w_row(j):
        row = jnp.where(j < n_a, _C_QB + j * tn, jnp.where(j < n_main, (j - n_a) * tn, 0))
        return pl.multiple_of(row, 8)

    return pl.pallas_call(
        functools.partial(_win_body, n_main=n_main),
        out_shape=jax.ShapeDtypeStruct((m, P_WIDTH), F32),
        grid=(n_main + 1, m // tm),
        in_specs=[
            pl.BlockSpec((tm, d), lambda j, i: (i, 0)),
            pl.BlockSpec((pl.Element(1), pl.Element(tn), pl.Element(d)), lambda j, i: (layer, w_row(j), 0)),
            pl.BlockSpec((tn, d), lambda j, i: (0, 0)),
        ],
        out_specs=pl.BlockSpec((tm, tn), lambda j, i: (i, j)),
        scratch_shapes=[pltpu.VMEM((tn, d), BF16)],
        compiler_params=_cparams(("arbitrary", "arbitrary")),
        name="input_projection",
    )(h, wt_all, _w_in_tail_t(wt_all, layer))


def _ext_w_q_up(w):
    w3 = w.reshape(Q_LORA, H_A, D_NOPE + D_ROPE)
    return jnp.concatenate([w3, _swap16(w3[:, :, D_NOPE:])], axis=2).reshape(Q_LORA, H_A * 256)


def _ext_gain(g):
    return jnp.concatenate([g, _swap16(g[D_NOPE:])]).reshape(1, 256)


def _rope_tables(seq):
    t = jnp.arange(seq)
    half = D_ROPE // 4
    inv = ROPE_THETA ** (-jnp.arange(half, dtype=F32) / half)
    ar = (t // GRID_W).astype(F32)[:, None] * inv[None, :]
    ac = (t % GRID_W).astype(F32)[:, None] * inv[None, :]
    cr, sr, cc, sc = jnp.cos(ar), jnp.sin(ar), jnp.cos(ac), jnp.sin(ac)
    return jnp.concatenate([cr, cr, cc, cc, -sr, sr, -sc, sc], axis=1)


def kernel(x, c, ctx, c_ctx, ada_w, ada_b, norm1_g, norm2_g, ab_w_in, mla_q_norm_g, mla_kv_norm_g, mla_w_q_up, mla_w_kv_up, mla_qn_g, mla_kn_g, mlstm_conv_w, mlstm_gate_b, mlstm_out_norm_g, ab_w_out, na_w_qkv, na_qn_g, na_kn_g, na_rpb, na_w_out, moe_w_rg, moe_b_rg, moe_w_re, moe_b_re, moe_w_gate, moe_w_up, moe_w_down):
    batch, seq, d = x.shape
    ctx_len = ctx.shape[1]
    depth = ada_w.shape[0]
    n_lat = batch * seq
    dims = (n_lat, seq, batch)
    assert d == D_MODEL and batch + 1 <= MOD_ROWS and seq % 512 == 0 and ctx_len == ROW_BLOCK

    xs = jnp.concatenate([x.reshape(n_lat, d), ctx.reshape(batch * ctx_len, d)], axis=0)
    cvec = jnp.concatenate([c, c_ctx[None, :]], axis=0)
    mods_all = ada_mods(cvec, ada_w, ada_b).reshape(depth, MOD_ROWS * 6, 1, d)
    trig = _rope_tables(seq)

    w_gate_t, w_up_t = jnp.swapaxes(moe_w_gate, 2, 3), jnp.swapaxes(moe_w_up, 2, 3)
    w_in_t = jnp.swapaxes(ab_w_in, 1, 2)
    kw = dict(batch=batch, seq=seq, ctx_len=ctx_len)
    for layer in range(depth):
        j = layer // 2
        last = layer == depth - 1
        mods = mods_all[layer]
        h = norm_mod(xs, norm1_g[layer], mods, (0, 1), dims)
        if layer % 2 == 0:
            p_all = input_projection(h, w_in_t, j)
            q_all = matmul(p_all, _ext_w_q_up(mla_w_q_up[j]), out_dtype=BF16, a_col=P_CQ // Q_LORA, k=Q_LORA,
                           rms_gain=mla_q_norm_g[j])
            kv_all = matmul(p_all, mla_w_kv_up, layer=j, out_dtype=BF16, a_col=P_CKV // KV_LORA, k=KV_LORA,
                            rms_gain=mla_kv_norm_g[j])
            gq, gk = _ext_gain(mla_qn_g[j]), _ext_gain(mla_kn_g[j])
            mix_a = jnp.zeros((xs.shape[0], W_A), BF16)
            mix_a = mla_attention(q_all, kv_all, p_all, trig, gq, gk, latent_queries=True, out_buf=mix_a, **kw)
            mix_a = mla_attention(q_all, kv_all, p_all, trig, gq, gk, latent_queries=False, out_buf=mix_a, **kw)
            qk = conv_silu(p_all, mlstm_conv_w[j], n_lat=n_lat, seq=seq, ctx_len=ctx_len)
            gates_t = p_all[:, P_GATE:P_GATE + 4 * H_B].T
            h_fwd, h_bwd = mlstm_scan(qk, p_all, gates_t, mlstm_gate_b[j], **kw)
            mix_b = mlstm_finish(h_fwd, h_bwd, p_all, mlstm_out_norm_g[j])
            xs = matmul(mix_a, ab_w_out, layer=j, a2=mix_b, out_dtype=F32, resid=xs, mods=mods, k_gate=2,
                        dims=dims)
        else:
            qkv = matmul(h, na_w_qkv, layer=j, out_dtype=BF16)
            gq, gk = na_qn_g[j].reshape(1, DH_C), na_kn_g[j].reshape(1, DH_C)
            bias = na_bias_rows(na_rpb[j])
            if last:
                mixed = na_attention(qkv, bias, gq, gk, **kw)
            else:
                mixed = jnp.zeros((xs.shape[0], H_C * DH_C), BF16)
                mixed = na_attention(qkv, bias, gq, gk, out_buf=mixed, **kw)
                mixed = na_ctx_attention(qkv, gq, gk, mixed, **kw)
            xs = matmul(mixed, na_w_out, layer=j, out_dtype=F32, resid=xs, mods=mods, k_gate=2, dims=dims)

        w_r = jnp.pad(jnp.concatenate([moe_w_re[layer], moe_w_rg[layer]], axis=1),
                      ((0, 0), (0, 128 - N_EXPERTS - N_GROUPS)))
        b_r = jnp.pad(jnp.concatenate([moe_b_re[layer], moe_b_rg[layer]]), (0, 128 - N_EXPERTS - N_GROUPS))
        hs, route, counts = norm_router(xs, norm2_g[layer], mods, (3, 4), dims, w_r, b_r.reshape(1, 128))
        n_tiles = 2 * xs.shape[0] // MOE_TILE + N_EXPERTS
        pos1, pos2, tile_expert, n_used = moe_slot_plan(route, counts, n_tiles)
        src = moe_slot_source(pos1, pos2, n_tiles * MOE_TILE)
        ys = moe_experts(hs, src, tile_expert, n_used, w_gate_t, w_up_t, moe_w_down, layer)
        xs = moe_combine(ys, pos1, pos2, route, xs, mods, 5, dims)
    return xs[:n_lat].reshape(batch, seq, d) if xs.shape[0] != n_lat else xs.reshape(batch, seq, d)
```

```python
import functools

import numpy as np
import jax
import jax.numpy as jnp
from jax import lax
from jax.experimental import pallas as pl
from jax.experimental.pallas import tpu as pltpu

F32 = jnp.float32
BF16 = jnp.bfloat16

D_MODEL = 4096
GRID_W = 64
H_A = 16
Q_LORA = 1024
KV_LORA = 512
D_NOPE = 128
D_ROPE = 64
D_V_A = 128
H_B = 4
DQK_B = 256
DV_B = 512
MLSTM_CHUNK = 128
M_INIT = -1e30
H_C = 32
DH_C = 128
NA_KH = 8
NA_KW = 16
N_GROUPS = 4
EXPERTS_PER_GROUP = 8
N_EXPERTS = 32
D_FF_EXPERT = 192
ROPE_THETA = 10000.0
NORM_EPS = 1e-6
W_A = H_A * D_V_A
W_B = H_B * DV_B
NEG = -1e30

VMEM_LIMIT_BYTES = 56 * 1024 * 1024
ROW_BLOCK = 256
MOD_ROWS = 8
SLAB_ROWS = 32
SLAB_STRIDE = 40
MOE_TILE = 256

P_QK = 0
P_V = 2048
P_O = 4096
P_CQ = 6144
P_CKV = 7168
P_KR = 7680
P_GATE = 7808
P_WIDTH = 8192


def _cparams(sem):
    return pltpu.CompilerParams(dimension_semantics=sem, vmem_limit_bytes=VMEM_LIMIT_BYTES)


MAX_ROW_TILE = 1100


def _row_tile(m, limit=MAX_ROW_TILE):
    return max(t for t in range(16, limit + 1, 16) if m % t == 0)


def _dma_cparams(sem):
    return pltpu.CompilerParams(dimension_semantics=sem, vmem_limit_bytes=VMEM_LIMIT_BYTES,
                                disable_bounds_checks=True)


def _mod_row(i, rows_per_block, n_lat, seq, batch):
    return jnp.where(i < n_lat // rows_per_block, i // (seq // rows_per_block), batch)


ADA_ROW_CHUNK = 512


def _ada_body(c_ref, w_ref, b_ref, o_ref, sc_ref, *, n_vec):
    d, tn = w_ref.shape[1], w_ref.shape[2]
    reps = tn // 128

    @pl.when((pl.program_id(0) == 0) & (pl.program_id(1) == 0))
    def _():
        c = c_ref[...]
        sc_ref[...] = c * jax.nn.sigmoid(c)

    def chunk(ci, accs):
        rows = pl.ds(pl.multiple_of(ci * ADA_ROW_CHUNK, ADA_ROW_CHUNK), ADA_ROW_CHUNK)
        w = w_ref[0, rows, :]
        out = []
        for m in range(n_vec):
            cm = jnp.concatenate([sc_ref[m, rows, :]] * reps, axis=1)
            part = (w * cm).reshape(ADA_ROW_CHUNK // 8, 8, tn).sum(axis=0)
            out.append(accs[m] + part)
        return tuple(out)

    accs = lax.fori_loop(0, d // ADA_ROW_CHUNK, chunk, tuple(jnp.zeros((8, tn), F32) for _ in range(n_vec)))
    rows = [jnp.sum(a, axis=0, keepdims=True) for a in accs]
    rows += [jnp.zeros((1, tn), F32)] * (MOD_ROWS - n_vec)
    o_ref[0] = jnp.concatenate(rows, axis=0) + b_ref[0]


def ada_mods(cvec, ada_w, ada_b, tn=512):
    depth, d, n6 = ada_w.shape
    n_vec = cvec.shape[0]
    c_lanes = jnp.broadcast_to(cvec[:, :, None], (n_vec, d, 128))
    return pl.pallas_call(
        functools.partial(_ada_body, n_vec=n_vec),
        out_shape=jax.ShapeDtypeStruct((depth, MOD_ROWS, n6), F32),
        grid=(depth, n6 // tn),
        in_specs=[
            pl.BlockSpec((n_vec, d, 128), lambda l, j: (0, 0, 0)),
            pl.BlockSpec((1, d, tn), lambda l, j: (l, 0, j)),
            pl.BlockSpec((1, 1, tn), lambda l, j: (l, 0, j)),
        ],
        out_specs=pl.BlockSpec((1, MOD_ROWS, tn), lambda l, j: (l, 0, j)),
        scratch_shapes=[pltpu.VMEM((n_vec, d, 128), F32)],
        compiler_params=_cparams(("arbitrary", "arbitrary")),
        name="ada_mods",
    )(c_lanes, ada_w, ada_b.reshape(depth, 1, n6))


def _modulated_norm(x_ref, g_ref, sh_ref, sc_ref):
    x = x_ref[...]
    ms = jnp.mean(x * x, axis=-1, keepdims=True)
    y = x * lax.rsqrt(ms + NORM_EPS) * g_ref[...]
    return y * (1.0 + sc_ref[0]) + sh_ref[0]


def _norm_body(x_ref, g_ref, sh_ref, sc_ref, o_ref):
    o_ref[...] = _modulated_norm(x_ref, g_ref, sh_ref, sc_ref).astype(o_ref.dtype)


def _store_slabs(ref, mat, rows):
    for j in range(SLAB_ROWS):
        ref[pl.ds(j, rows, stride=SLAB_STRIDE), :] = mat[:, j * 128:(j + 1) * 128]
    for j in range(SLAB_ROWS, SLAB_STRIDE):
        ref[pl.ds(j, rows, stride=SLAB_STRIDE), :] = jnp.zeros((rows, 128), mat.dtype)


def _norm_router_body(x_ref, g_ref, sh_ref, sc_ref, wr_ref, br_ref, hs_ref, route_ref, cnt_ref, carry):
    @pl.when(pl.program_id(0) == 0)
    def _():
        carry[...] = jnp.zeros_like(carry)

    h = _modulated_norm(x_ref, g_ref, sh_ref, sc_ref)
    _store_slabs(hs_ref, h, ROW_BLOCK)
    logits = jnp.dot(h, wr_ref[...], precision=lax.Precision.HIGHEST,
                     preferred_element_type=F32) + br_ref[...]
    lane = lax.broadcasted_iota(jnp.int32, logits.shape, 1)
    big = jnp.int32(1 << 20)
    is_g = (lane >= N_EXPERTS) & (lane < N_EXPERTS + N_GROUPS)
    gl = jnp.where(is_g, logits, NEG)
    gmax = jnp.max(gl, axis=-1, keepdims=True)
    g_idx = jnp.min(jnp.where(gl == gmax, lane, big), axis=-1, keepdims=True) - N_EXPERTS
    p_g = 1.0 / jnp.sum(jnp.where(is_g, jnp.exp(gl - gmax), 0.0), axis=-1, keepdims=True)
    lo = g_idx * EXPERTS_PER_GROUP
    in_grp = (lane >= lo) & (lane < lo + EXPERTS_PER_GROUP)
    el = jnp.where(in_grp, logits, NEG)
    e1 = jnp.max(el, axis=-1, keepdims=True)
    i1 = jnp.min(jnp.where(el == e1, lane, big), axis=-1, keepdims=True)
    el2 = jnp.where(lane == i1, NEG, el)
    e2 = jnp.max(el2, axis=-1, keepdims=True)
    i2 = jnp.min(jnp.where(el2 == e2, lane, big), axis=-1, keepdims=True)
    t = jnp.exp(e2 - e1)
    w1 = p_g / (1.0 + t)
    w2 = w1 * t
    onehot = jnp.where((lane == i1) | (lane == i2), 1.0, 0.0)
    t_i = lax.broadcasted_iota(jnp.int32, (ROW_BLOCK, ROW_BLOCK), 0)
    s_i = lax.broadcasted_iota(jnp.int32, (ROW_BLOCK, ROW_BLOCK), 1)
    before = jnp.where(s_i < t_i, 1.0, 0.0).astype(BF16)
    seen = jnp.dot(before, onehot.astype(BF16), preferred_element_type=F32) + carry[...]
    rank1 = jnp.sum(jnp.where(lane == i1, seen, 0.0), axis=-1, keepdims=True)
    rank2 = jnp.sum(jnp.where(lane == i2, seen, 0.0), axis=-1, keepdims=True)
    total = carry[...] + jnp.sum(onehot, axis=0, keepdims=True)
    carry[...] = total
    cnt_ref[...] = jnp.broadcast_to(total, cnt_ref.shape)
    cols = (i1.astype(F32), i2.astype(F32), w1, w2, rank1, rank2)
    route = jnp.zeros(logits.shape, F32)
    for k, v in enumerate(cols):
        route = jnp.where(lane == k, v, route)
    route_ref[...] = route


def _norm_specs(d, mods_k, dims):
    n_lat, seq, batch = dims
    mrow = functools.partial(_mod_row, rows_per_block=ROW_BLOCK, n_lat=n_lat, seq=seq, batch=batch)
    k_shift, k_scale = mods_k
    return [
        pl.BlockSpec((ROW_BLOCK, d), lambda i: (i, 0)),
        pl.BlockSpec((1, d), lambda i: (0, 0)),
        pl.BlockSpec((1, 1, d), lambda i: (mrow(i) * 6 + k_shift, 0, 0)),
        pl.BlockSpec((1, 1, d), lambda i: (mrow(i) * 6 + k_scale, 0, 0)),
    ]


def norm_mod(x, g, mods, mods_k, dims):
    n, d = x.shape
    return pl.pallas_call(
        _norm_body,
        out_shape=jax.ShapeDtypeStruct((n, d), BF16),
        grid=(n // ROW_BLOCK,),
        in_specs=_norm_specs(d, mods_k, dims),
        out_specs=pl.BlockSpec((ROW_BLOCK, d), lambda i: (i, 0)),
        compiler_params=_cparams(("parallel",)),
        name="norm_mod",
    )(x, g.reshape(1, d), mods, mods)


def norm_router(x, g, mods, mods_k, dims, w_r, b_r):
    n, d = x.shape
    assert d == SLAB_ROWS * 128
    return pl.pallas_call(
        _norm_router_body,
        out_shape=(jax.ShapeDtypeStruct((n * SLAB_STRIDE, 128), F32), jax.ShapeDtypeStruct((n, 128), F32),
                   jax.ShapeDtypeStruct((8, 128), F32)),
        grid=(n // ROW_BLOCK,),
        in_specs=_norm_specs(d, mods_k, dims) + [
            pl.BlockSpec((d, 128), lambda i: (0, 0)),
            pl.BlockSpec((1, 128), lambda i: (0, 0)),
        ],
        out_specs=(pl.BlockSpec((ROW_BLOCK * SLAB_STRIDE, 128), lambda i: (i, 0)),
                   pl.BlockSpec((ROW_BLOCK, 128), lambda i: (i, 0)),
                   pl.BlockSpec((8, 128), lambda i: (0, 0))),
        scratch_shapes=[pltpu.VMEM((1, 128), F32)],
        compiler_params=_cparams(("arbitrary",)),
        name="norm_router",
    )(x, g.reshape(1, d), mods, mods, w_r, b_r)


def _mm_body(*refs, rms, gated, k1, two_a):
    it = iter(refs)
    a_ref = next(it)
    a2_ref = next(it) if two_a else None
    g_ref = next(it) if rms else None
    w_ref = next(it)
    res_ref = next(it) if gated else None
    gate_ref = next(it) if gated else None
    o_ref = next(it)
    wbf_ref = next(it)

    @pl.when(pl.program_id(1) == 0)
    def _():
        wbf_ref[...] = w_ref[...].astype(BF16)

    a = a_ref[...]
    if rms:
        af = a.astype(F32)
        ms = jnp.mean(af * af, axis=-1, keepdims=True)
        a = (af * lax.rsqrt(ms + NORM_EPS) * g_ref[...]).astype(BF16)
    if two_a:
        acc = (jnp.dot(a, wbf_ref[:k1, :], preferred_element_type=F32)
               + jnp.dot(a2_ref[...], wbf_ref[k1:, :], preferred_element_type=F32))
    else:
        acc = jnp.dot(a, wbf_ref[...], preferred_element_type=F32)
    if gated:
        acc = res_ref[...] + gate_ref[0] * acc
    o_ref[...] = acc.astype(o_ref.dtype)


def matmul(a, w, *, out_dtype, layer=None, a_col=0, k=None, a2=None, rms_gain=None, resid=None, mods=None,
           k_gate=None, dims=None, tm=512, tn=512):
    m = a.shape[0]
    kw_, nw = w.shape[-2], w.shape[-1]
    two_a = a2 is not None
    k = (a.shape[1] if two_a else kw_) if k is None else k
    assert kw_ == k + (a2.shape[1] if two_a else 0) and m % tm == 0 and nw % tn == 0
    rms = rms_gain is not None
    gated = resid is not None
    in_specs = [pl.BlockSpec((tm, k), lambda j, i: (i, a_col))]
    args = [a]
    if two_a:
        in_specs.append(pl.BlockSpec((tm, a2.shape[1]), lambda j, i: (i, 0)))
        args.append(a2)
    if rms:
        in_specs.append(pl.BlockSpec((1, k), lambda j, i: (0, 0)))
        args.append(rms_gain.reshape(1, k))
    if layer is None:
        in_specs.append(pl.BlockSpec((kw_, tn), lambda j, i: (0, j)))
    else:
        in_specs.append(pl.BlockSpec((None, kw_, tn), lambda j, i: (layer, 0, j)))
    args.append(w)
    if gated:
        n_lat, seq, batch = dims
        mrow = functools.partial(_mod_row, rows_per_block=tm, n_lat=n_lat, seq=seq, batch=batch)
        in_specs.append(pl.BlockSpec((tm, tn), lambda j, i: (i, j)))
        in_specs.append(pl.BlockSpec((1, 1, tn), lambda j, i: (mrow(i) * 6 + k_gate, 0, j)))
        args += [resid, mods]
    return pl.pallas_call(
        functools.partial(_mm_body, rms=rms, gated=gated, k1=k, two_a=two_a),
        out_shape=jax.ShapeDtypeStruct((m, nw), out_dtype),
        grid=(nw // tn, m // tm),
        in_specs=in_specs,
        out_specs=pl.BlockSpec((tm, tn), lambda j, i: (i, j)),
        scratch_shapes=[pltpu.VMEM((kw_, tn), BF16)],
        compiler_params=_cparams(("arbitrary", "arbitrary")),
        name="matmul",
    )(*args)


def _rope_rot(pr, trig):
    if trig is not None:
        pr = pr * trig
    else:
        lane = lax.broadcasted_iota(jnp.int32, pr.shape, 1)
        pr = jnp.where(lane < D_ROPE, pr, 0.0)
    return pr + pltpu.roll(pr, D_ROPE, axis=1)


def _mla_body(*refs, n_lat_keys, n_ctx_keys, rope_q, scale, key_chunk, aliased):
    it = iter(refs)
    q_ref = next(it)
    kvl_ref = next(it) if n_lat_keys else None
    krl_ref = next(it) if n_lat_keys else None
    tkl_ref = next(it) if n_lat_keys else None
    kvc_ref = next(it)
    krc_ref = next(it)
    tq_ref = next(it) if rope_q else None
    gq_ref = next(it)
    gk_ref = next(it)
    if aliased:
        next(it)
    o_ref = next(it)
    kbuf = next(it)
    vbuf = next(it)
    dqk = D_NOPE + D_ROPE

    def prep_keys(kv_ref, kr_ref, trig_ref, off, n):
        gk = gk_ref[...]

        def chunk(c, carry):
            rows = pl.ds(pl.multiple_of(c * key_chunk, key_chunk), key_chunk)
            kv = kv_ref[rows, :]
            kn = kv[:, :D_NOPE].astype(F32)
            kr = kr_ref[rows, :].astype(F32)
            lane = lax.broadcasted_iota(jnp.int32, kr.shape, 1)
            ss = (jnp.sum(kn * kn, axis=-1, keepdims=True)
                  + jnp.sum(jnp.where(lane < D_ROPE, kr * kr, 0.0), axis=-1, keepdims=True))
            r = lax.rsqrt(ss / dqk + NORM_EPS)
            trig = trig_ref[rows, :] if trig_ref is not None else None
            rot = _rope_rot(kr * (r * gk[:, D_NOPE:]), trig)
            orow = pl.ds(pl.multiple_of(off + c * key_chunk, key_chunk), key_chunk)
            kbuf[orow, :D_NOPE] = (kn * (r * gk[:, :D_NOPE])).astype(BF16)
            kbuf[orow, D_NOPE:] = rot.astype(BF16)
            vbuf[orow, :] = kv[:, D_NOPE:].astype(BF16)
            return carry

        lax.fori_loop(0, n // key_chunk, chunk, 0)

    @pl.when(pl.program_id(2) == 0)
    def _():
        if n_lat_keys:
            prep_keys(kvl_ref, krl_ref, tkl_ref, 0, n_lat_keys)
        prep_keys(kvc_ref, krc_ref, None, n_lat_keys, n_ctx_keys)

    gq = gq_ref[...]
    for sb in range(q_ref.shape[0] // MLA_SUB_Q):
        rows = slice(sb * MLA_SUB_Q, (sb + 1) * MLA_SUB_Q)
        q = q_ref[rows, :].astype(F32)
        qn = q[:, :D_NOPE]
        qr = q[:, D_NOPE:]
        lane = lax.broadcasted_iota(jnp.int32, qr.shape, 1)
        ss = (jnp.sum(qn * qn, axis=-1, keepdims=True)
              + jnp.sum(jnp.where(lane < D_ROPE, qr * qr, 0.0), axis=-1, keepdims=True))
        r = lax.rsqrt(ss / dqk + NORM_EPS) * scale
        rot = _rope_rot(qr * (r * gq[:, D_NOPE:]), tq_ref[rows, :] if rope_q else None)
        rot = jnp.where(lane < D_ROPE, rot, 0.0)
        q2 = jnp.concatenate([qn * (r * gq[:, :D_NOPE]), rot], axis=1).astype(BF16)
        s = lax.dot_general(q2, kbuf[...], (((1,), (1,)), ((), ())), preferred_element_type=F32)
        m = jnp.max(s, axis=-1, keepdims=True)
        p = jnp.exp2(s - m)
        l = jnp.sum(p, axis=-1, keepdims=True)
        o = jnp.dot(p.astype(BF16), vbuf[...], preferred_element_type=F32)
        o_ref[rows, :] = (o / l).astype(o_ref.dtype)


MLA_SUB_Q = 256


def mla_attention(q_all, kv_all, p_all, trig, gq, gk, *, batch, seq, ctx_len, latent_queries, out_buf=None):
    n_lat = batch * seq
    kr_blk = P_KR // 128
    scale = (D_NOPE + D_ROPE) ** -0.5 * float(np.log2(np.e))
    if latent_queries:
        tq = 2 * MLA_SUB_Q
        nq, n_lat_keys, q_row0 = seq, seq, 0
    else:
        tq = ctx_len
        nq, n_lat_keys, q_row0 = ctx_len, 0, n_lat // tq
    assert tq % MLA_SUB_Q == 0 and nq % tq == 0
    nqb = nq // tq
    ctx_blk0 = n_lat // ctx_len
    in_specs = [pl.BlockSpec((tq, 256), lambda b, h, i: (q_row0 + b * nqb + i, h))]
    args = [q_all]
    if latent_queries:
        in_specs += [
            pl.BlockSpec((seq, 256), lambda b, h, i: (b, h)),
            pl.BlockSpec((seq, 128), lambda b, h, i: (b, kr_blk)),
            pl.BlockSpec((seq, 128), lambda b, h, i: (0, 0)),
        ]
        args += [kv_all, p_all, trig]
    in_specs += [
        pl.BlockSpec((ctx_len, 256), lambda b, h, i: (ctx_blk0 + b, h)),
        pl.BlockSpec((ctx_len, 128), lambda b, h, i: (ctx_blk0 + b, kr_blk)),
    ]
    args += [kv_all, p_all]
    if latent_queries:
        in_specs.append(pl.BlockSpec((tq, 128), lambda b, h, i: (i, 0)))
        args.append(trig)
    in_specs += [pl.BlockSpec((1, 256), lambda b, h, i: (0, 0))] * 2
    args += [gq, gk]
    aliases = {}
    if out_buf is not None:
        in_specs.append(pl.BlockSpec(memory_space=pl.ANY))
        aliases = {len(args): 0}
        args.append(out_buf)
    nk = n_lat_keys + ctx_len
    body = functools.partial(_mla_body, n_lat_keys=n_lat_keys, n_ctx_keys=ctx_len, rope_q=latent_queries,
                             scale=scale, key_chunk=min(256, ctx_len), aliased=out_buf is not None)
    return pl.pallas_call(
        body,
        out_shape=jax.ShapeDtypeStruct((q_all.shape[0], W_A), BF16),
        grid=(batch, H_A, nqb),
        in_specs=in_specs,
        out_specs=pl.BlockSpec((tq, D_V_A), lambda b, h, i: (q_row0 + b * nqb + i, h)),
        scratch_shapes=[pltpu.VMEM((nk, 256), BF16), pltpu.VMEM((nk, D_V_A), BF16)],
        input_output_aliases=aliases,
        compiler_params=_cparams(("arbitrary", "arbitrary", "arbitrary")),
        name="mla_attention",
    )(*args)


def _conv_silu_body(x_ref, prev_ref, next_ref, w_ref, o_ref, *, lat_blocks, seq_blocks, ctx_blocks):
    i = pl.program_id(0)
    j = jnp.where(i < lat_blocks, i % seq_blocks, (i - lat_blocks) % ctx_blocks)
    nb = jnp.where(i < lat_blocks, seq_blocks, ctx_blocks)
    x = x_ref[...]
    rows = x.shape[0]
    row = lax.broadcasted_iota(jnp.int32, x.shape, 0)
    prev_row = jnp.where(j == 0, 0.0, prev_ref[7:8, :])
    next_row = jnp.where(j == nb - 1, 0.0, next_ref[0:1, :])
    xp = jnp.where(row == 0, prev_row, pltpu.roll(x, 1, axis=0))
    xn = jnp.where(row == rows - 1, next_row, pltpu.roll(x, rows - 1, axis=0))
    w = w_ref[...]
    y = xp * w[0:1, :] + x * w[1:2, :] + xn * w[2:3, :]
    y = y * jax.nn.sigmoid(y)
    half = y.shape[1] // 2
    o_ref[:, :half] = y[:, :half].astype(o_ref.dtype)
    o_ref[:, half:] = (y[:, half:] * (DQK_B ** -0.5)).astype(o_ref.dtype)


def conv_silu(p_all, conv_w, *, n_lat, seq, ctx_len):
    n = p_all.shape[0]
    width = 2 * H_B * DQK_B
    rb = ROW_BLOCK
    nblk = n // rb
    sub = rb // 8
    last8 = n // 8 - 1
    body = functools.partial(_conv_silu_body, lat_blocks=n_lat // rb, seq_blocks=seq // rb,
                             ctx_blocks=ctx_len // rb)
    return pl.pallas_call(
        body,
        out_shape=jax.ShapeDtypeStruct((n, width), BF16),
        grid=(nblk,),
        in_specs=[
            pl.BlockSpec((rb, width), lambda i: (i, P_QK // width)),
            pl.BlockSpec((8, width), lambda i: (jnp.maximum(i * sub - 1, 0), P_QK // width)),
            pl.BlockSpec((8, width), lambda i: (jnp.minimum((i + 1) * sub, last8), P_QK // width)),
            pl.BlockSpec((3, width), lambda i: (0, 0)),
        ],
        out_specs=pl.BlockSpec((rb, width), lambda i: (i, 0)),
        compiler_params=_cparams(("parallel",)),
        name="conv_silu",
    )(p_all, p_all, p_all, conv_w)


def _log_sigmoid(x):
    return jnp.minimum(x, 0.0) - jnp.log(1.0 + jnp.exp(-jnp.abs(x)))


def _mlstm_body(qf, kf, vf, gcf, grf, qb, kb, vb, gcb, grb, bc_ref, br_ref, of_ref, ob_ref, c_st, n_st, m_st):
    @pl.when(pl.program_id(2) == 0)
    def _():
        c_st[...] = jnp.zeros_like(c_st)
        n_st[...] = jnp.zeros_like(n_st)
        m_st[...] = jnp.full_like(m_st, M_INIT)

    _mlstm_chunk(0, qf, kf, vf, gcf, grf, bc_ref, br_ref, of_ref, c_st.at[0], n_st.at[0], m_st.at[0])
    _mlstm_chunk(1, qb, kb, vb, gcb, grb, bc_ref, br_ref, ob_ref, c_st.at[1], n_st.at[1], m_st.at[1])


def _mlstm_chunk(direction, q_ref, k_ref, v_ref, gc_ref, gr_ref, bc_ref, br_ref, o_ref, c_st, n_st, m_st):
    h = pl.program_id(1)
    L = MLSTM_CHUNK
    hi = lax.Precision.HIGHEST
    gi = direction * (2 * H_B) + h
    gcol = gc_ref[...] + bc_ref[...]
    lane = lax.broadcasted_iota(jnp.int32, gcol.shape, 1)
    li_c = jnp.sum(jnp.where(lane == gi, gcol, 0.0), axis=1, keepdims=True)
    lf_c = _log_sigmoid(jnp.sum(jnp.where(lane == gi + H_B, gcol, 0.0), axis=1, keepdims=True))
    grow = gr_ref[...] + br_ref[...]
    sub = lax.broadcasted_iota(jnp.int32, grow.shape, 0)
    li_r = jnp.sum(jnp.where(sub == gi, grow, 0.0), axis=0, keepdims=True)
    lf_r = _log_sigmoid(jnp.sum(jnp.where(sub == gi + H_B, grow, 0.0), axis=0, keepdims=True))

    t_i = lax.broadcasted_iota(jnp.int32, (L, L), 0)
    s_i = lax.broadcasted_iota(jnp.int32, (L, L), 1)
    sgn = 1 - 2 * direction
    incl = (s_i - t_i) * sgn <= 0
    incl_f = incl.astype(F32)
    incl_t = ((t_i - s_i) * sgn <= 0).astype(F32)
    bcum_c = jnp.dot(incl_f, jnp.broadcast_to(lf_c, (L, L)), precision=hi, preferred_element_type=F32)
    bcum_r = jnp.dot(jnp.broadcast_to(lf_r, (8, L)), incl_t, precision=hi, preferred_element_type=F32)[0:1, :]
    b_last = jnp.sum(lf_r, axis=1, keepdims=True)

    m_prev = m_st[...]
    a_c = bcum_c[:, 0:1] + m_prev
    dlog = jnp.where(incl, bcum_c - bcum_r + li_r, -jnp.inf)
    m_t = jnp.maximum(a_c, jnp.max(dlog, axis=1, keepdims=True))
    w_intra = jnp.exp(dlog - m_t)
    w_inter = jnp.exp(a_c - m_t)

    q = q_ref[...]
    k = k_ref[...]
    v = v_ref[...].astype(BF16)
    c_prev = c_st[...]
    n_prev = n_st[...]
    s = lax.dot_general(q, k, (((1,), (1,)), ((), ())), preferred_element_type=F32) * w_intra
    num = (w_inter * jnp.dot(q, c_prev.astype(BF16), preferred_element_type=F32)
           + jnp.dot(s.astype(BF16), v, preferred_element_type=F32))
    den = (w_inter * jnp.sum(q.astype(F32) * n_prev, axis=1, keepdims=True)
           + jnp.sum(s, axis=1, keepdims=True))
    o_ref[...] = num / jnp.maximum(jnp.abs(den), jnp.exp(-m_t))

    g_r = b_last - bcum_r + li_r
    g_c = b_last - bcum_c[:, 0:1] + li_c
    m_new = jnp.maximum(b_last + m_prev, jnp.max(g_r, axis=1, keepdims=True))
    decay = jnp.exp(b_last + m_prev - m_new)
    kw = k.astype(F32) * jnp.exp(g_c - m_new)
    c_st[...] = decay * c_prev + lax.dot_general(kw.astype(BF16), v, (((0,), (0,)), ((), ())),
                                                 preferred_element_type=F32)
    n_st[...] = decay * n_prev + jnp.sum(kw, axis=0, keepdims=True)
    m_st[...] = m_new


def mlstm_scan(qk, p_all, gates_t, gate_b, *, batch, seq, ctx_len):
    n = qk.shape[0]
    L = MLSTM_CHUNK
    cc, lc = ctx_len // L, seq // L
    n_lat_blk = batch * lc

    def rowblk(b, d, c):
        ctx_blk = n_lat_blk + b * cc + (c if d == 0 else cc - 1 - c)
        lat_blk = b * lc + (c - cc if d == 0 else lc - 1 - (c - cc))
        return jnp.where(c < cc, ctx_blk, lat_blk)

    def chunk_specs(d):
        return [
            pl.BlockSpec((L, DQK_B), lambda b, h, c: (rowblk(b, d, c), h)),
            pl.BlockSpec((L, DQK_B), lambda b, h, c: (rowblk(b, d, c), H_B + h)),
            pl.BlockSpec((L, DV_B), lambda b, h, c: (rowblk(b, d, c), P_V // DV_B + h)),
            pl.BlockSpec((L, 128), lambda b, h, c: (rowblk(b, d, c), P_GATE // 128)),
            pl.BlockSpec((4 * H_B, L), lambda b, h, c: (0, rowblk(b, d, c))),
        ]

    def out_spec(d):
        return pl.BlockSpec((L, DV_B), lambda b, h, c: (rowblk(b, d, c), h))

    gb_col = jnp.zeros((1, 128), F32).at[0, :4 * H_B].set(gate_b)
    gb_row = gate_b.reshape(4 * H_B, 1)
    chunk_args = [qk, qk, p_all, p_all, gates_t]
    return pl.pallas_call(
        _mlstm_body,
        out_shape=(jax.ShapeDtypeStruct((n, W_B), F32), jax.ShapeDtypeStruct((n, W_B), F32)),
        grid=(batch, H_B, cc + lc),
        in_specs=chunk_specs(0) + chunk_specs(1) + [
            pl.BlockSpec((1, 128), lambda b, h, c: (0, 0)),
            pl.BlockSpec((4 * H_B, 1), lambda b, h, c: (0, 0)),
        ],
        out_specs=(out_spec(0), out_spec(1)),
        scratch_shapes=[pltpu.VMEM((2, DQK_B, DV_B), F32), pltpu.VMEM((2, 1, DQK_B), F32),
                        pltpu.VMEM((2, 1, 1), F32)],
        compiler_params=_cparams(("arbitrary",) * 3),
        name="mlstm_scan",
    )(*chunk_args, *chunk_args, gb_col, gb_row)


def _mlstm_finish_body(hf_ref, hb_ref, o_ref, g_ref, out_ref):
    hs = hf_ref[...] + hb_ref[...]
    g = g_ref[...]
    og = o_ref[...]
    for hd in range(H_B):
        sl = slice(hd * DV_B, (hd + 1) * DV_B)
        x = hs[:, sl]
        ms = jnp.mean(x * x, axis=-1, keepdims=True)
        hn = x * lax.rsqrt(ms + NORM_EPS) * g[:, sl]
        out_ref[:, sl] = (jax.nn.sigmoid(og[:, sl]) * hn).astype(out_ref.dtype)


def mlstm_finish(h_fwd, h_bwd, p_all, out_norm_g):
    n = h_fwd.shape[0]
    rb = ROW_BLOCK
    return pl.pallas_call(
        _mlstm_finish_body,
        out_shape=jax.ShapeDtypeStruct((n, W_B), BF16),
        grid=(n // rb,),
        in_specs=[
            pl.BlockSpec((rb, W_B), lambda i: (i, 0)),
            pl.BlockSpec((rb, W_B), lambda i: (i, 0)),
            pl.BlockSpec((rb, W_B), lambda i: (i, P_O // W_B)),
            pl.BlockSpec((1, W_B), lambda i: (0, 0)),
        ],
        out_specs=pl.BlockSpec((rb, W_B), lambda i: (i, 0)),
        compiler_params=_cparams(("parallel",)),
        name="mlstm_finish",
    )(h_fwd, h_bwd, p_all, out_norm_g.reshape(1, W_B))


NA_ROWS_PER_STEP = 4
NA_TILES_PER_STEP = 4


def _na_geometry(rows_n):
    r_t = NA_ROWS_PER_STEP
    kh = min(NA_KH, rows_n)
    win = min(r_t + kh - 1, rows_n)
    tiles = rows_n // r_t
    starts, sigs, cls = [], [], []
    for t in range(tiles):
        rq0 = t * r_t
        start = int(np.clip(rq0 - kh // 2, 0, rows_n - win))
        r0 = np.clip(np.arange(rq0, rq0 + r_t) - kh // 2, 0, rows_n - kh)
        sig = (start - rq0, tuple((r0 - rq0).tolist()))
        if sig not in sigs:
            sigs.append(sig)
        starts.append(start)
        cls.append(sigs.index(sig))
    return kh, win, np.array(starts, np.int32), np.array(cls, np.int32), sigs


def na_bias_rows(rpb):
    cols = np.arange(GRID_W)
    c0 = np.clip(cols - NA_KW // 2, 0, GRID_W - NA_KW)
    col_valid = (cols[None, :] >= c0[:, None]) & (cols[None, :] < c0[:, None] + NA_KW)
    col_off = cols[None, :] - cols[:, None] + (NA_KW - 1)
    c_sel = (col_valid[:, :, None] & (col_off[:, :, None] == np.arange(2 * NA_KW - 1))).astype(np.float32)
    t = jnp.einsum('hrs,cds->hrcd', rpb, c_sel, precision=lax.Precision.HIGHEST)
    t = t + np.where(col_valid, 0.0, NEG).astype(np.float32)[None, None]
    return jnp.concatenate([t, t], axis=-1)


def _na_bias_plan(rows_n):
    r_t = NA_ROWS_PER_STEP
    kh, win, _, _, sigs = _na_geometry(rows_n)
    plan = []
    for (dstart, dr0) in sigs:
        cls_plan = []
        for a in range(r_t):
            row = []
            for j in range(win):
                krow = dstart + j
                ok = dr0[a] <= krow < dr0[a] + kh
                row.append(krow - a + (NA_KH - 1) if ok else None)
            cls_plan.append(row)
        plan.append(cls_plan)
    return plan


def _na_build_bias(rows_ref, btab, plan):
    neg = jnp.full((GRID_W, 2 * GRID_W), NEG, F32)
    lane = lax.broadcasted_iota(jnp.int32, (GRID_W, 2 * GRID_W), 1)

    def piece(ro):
        return neg if ro is None else rows_ref[0, ro]

    for k, cls_plan in enumerate(plan):
        for a, row in enumerate(cls_plan):
            rs = slice(a * GRID_W, (a + 1) * GRID_W)
            for j in range(0, len(row) - 1, 2):
                both = jnp.where(lane < GRID_W, piece(row[j]), piece(row[j + 1]))
                btab[k, rs, j * GRID_W:(j + 2) * GRID_W] = both
            if len(row) % 2:
                j = len(row) - 1
                btab[k, rs, j * GRID_W:(j + 1) * GRID_W] = piece(row[j])[:, :GRID_W]


def _head_rmsnorm(x, g):
    xf = x.astype(F32)
    ms = jnp.mean(xf * xf, axis=-1, keepdims=True)
    return xf * lax.rsqrt(ms + NORM_EPS) * g


def _na_body(start_ref, cls_ref, q_ref, k_ref, v_ref, kc_ref, vc_ref, brow_ref, gq_ref, gk_ref, *rest,
             seq, ctx_len, win_rows, scale, bias_plan):
    o_ref, kn, knc, btab = rest[-4:]
    i = pl.program_id(2)
    chunk = min(512, seq)

    @pl.when(i == 0)
    def _():
        _na_build_bias(brow_ref, btab, bias_plan)
        gk = gk_ref[...]

        def body(c, carry):
            rows = pl.ds(pl.multiple_of(c * chunk, chunk), chunk)
            kn[rows, :] = _head_rmsnorm(k_ref[rows, :], gk).astype(BF16)
            return carry

        lax.fori_loop(0, seq // chunk, body, 0)
        knc[...] = _head_rmsnorm(kc_ref[...], gk).astype(BF16)

    nt = (((1,), (1,)), ((), ()))
    tq = NA_ROWS_PER_STEP * GRID_W
    for tl in range(NA_TILES_PER_STEP):
        rows = slice(tl * tq, (tl + 1) * tq)
        q = (_head_rmsnorm(q_ref[rows, :], gq_ref[...]) * scale).astype(BF16)
        krow0 = pl.multiple_of(start_ref[i * NA_TILES_PER_STEP + tl] * GRID_W, GRID_W)
        kw = kn[pl.ds(krow0, win_rows), :]
        vw = v_ref[pl.ds(krow0, win_rows), :]
        s = lax.dot_general(q, kw, nt, preferred_element_type=F32) + btab[cls_ref[i * NA_TILES_PER_STEP + tl]]
        sc = lax.dot_general(q, knc[...], nt, preferred_element_type=F32)
        m = jnp.maximum(jnp.max(s, axis=-1, keepdims=True), jnp.max(sc, axis=-1, keepdims=True))
        p = jnp.exp(s - m)
        pc = jnp.exp(sc - m)
        l = jnp.sum(p, axis=-1, keepdims=True) + jnp.sum(pc, axis=-1, keepdims=True)
        o = (jnp.dot(p.astype(BF16), vw, preferred_element_type=F32)
             + jnp.dot(pc.astype(BF16), vc_ref[...], preferred_element_type=F32))
        o_ref[rows, :] = (o / l).astype(o_ref.dtype)


def na_attention(qkv, bias_rows, gq, gk, *, batch, seq, ctx_len, out_buf=None):
    rows_n = seq // GRID_W
    r_t = NA_ROWS_PER_STEP
    _, win, starts, cls, _ = _na_geometry(rows_n)
    tps = NA_TILES_PER_STEP
    tq = tps * r_t * GRID_W
    tiles = rows_n // (r_t * tps)
    assert rows_n % (r_t * tps) == 0
    n_lat = batch * seq
    ctx_blk0 = n_lat // ctx_len
    plan = _na_bias_plan(rows_n)
    body = functools.partial(_na_body, seq=seq, ctx_len=ctx_len, win_rows=win * GRID_W, scale=DH_C ** -0.5,
                             bias_plan=plan)
    n_ro = 2 * NA_KH - 1
    in_specs = [
        pl.BlockSpec((tq, DH_C), lambda b, h, i, st, cl: (b * tiles + i, h)),
        pl.BlockSpec((seq, DH_C), lambda b, h, i, st, cl: (b, H_C + h)),
        pl.BlockSpec((seq, DH_C), lambda b, h, i, st, cl: (b, 2 * H_C + h)),
        pl.BlockSpec((ctx_len, DH_C), lambda b, h, i, st, cl: (ctx_blk0 + b, H_C + h)),
        pl.BlockSpec((ctx_len, DH_C), lambda b, h, i, st, cl: (ctx_blk0 + b, 2 * H_C + h)),
        pl.BlockSpec((1, n_ro, GRID_W, 2 * GRID_W), lambda b, h, i, st, cl: (h, 0, 0, 0)),
        pl.BlockSpec((1, DH_C), lambda b, h, i, st, cl: (0, 0)),
        pl.BlockSpec((1, DH_C), lambda b, h, i, st, cl: (0, 0)),
    ]
    args = [jnp.asarray(starts), jnp.asarray(cls), qkv, qkv, qkv, qkv, qkv, bias_rows, gq, gk]
    aliases = {}
    out_rows = n_lat
    if out_buf is not None:
        in_specs.append(pl.BlockSpec(memory_space=pl.ANY))
        aliases = {len(args): 0}
        args.append(out_buf)
        out_rows = out_buf.shape[0]
    grid_spec = pltpu.PrefetchScalarGridSpec(
        num_scalar_prefetch=2,
        grid=(batch, H_C, tiles),
        in_specs=in_specs,
        out_specs=pl.BlockSpec((tq, DH_C), lambda b, h, i, st, cl: (b * tiles + i, h)),
        scratch_shapes=[pltpu.VMEM((seq, DH_C), BF16), pltpu.VMEM((ctx_len, DH_C), BF16),
                        pltpu.VMEM((len(plan), r_t * GRID_W, win * GRID_W), F32)],
    )
    return pl.pallas_call(
        body,
        out_shape=jax.ShapeDtypeStruct((out_rows, H_C * DH_C), BF16),
        grid_spec=grid_spec,
        input_output_aliases=aliases,
        compiler_params=_cparams(("arbitrary", "arbitrary", "arbitrary")),
        name="na_attention",
    )(*args)


def _ctx_attn_body(q_ref, k_ref, v_ref, gq_ref, gk_ref, buf_ref, o_ref, *, scale):
    del buf_ref
    q = (_head_rmsnorm(q_ref[...], gq_ref[...]) * scale).astype(BF16)
    k = _head_rmsnorm(k_ref[...], gk_ref[...]).astype(BF16)
    s = lax.dot_general(q, k, (((1,), (1,)), ((), ())), preferred_element_type=F32)
    m = jnp.max(s, axis=-1, keepdims=True)
    p = jnp.exp(s - m)
    l = jnp.sum(p, axis=-1, keepdims=True)
    o = jnp.dot(p.astype(BF16), v_ref[...], preferred_element_type=F32)
    o_ref[...] = (o / l).astype(o_ref.dtype)


def na_ctx_attention(qkv, gq, gk, out_buf, *, batch, seq, ctx_len):
    ctx_blk0 = batch * seq // ctx_len
    return pl.pallas_call(
        functools.partial(_ctx_attn_body, scale=DH_C ** -0.5),
        out_shape=jax.ShapeDtypeStruct(out_buf.shape, out_buf.dtype),
        grid=(batch, H_C),
        in_specs=[
            pl.BlockSpec((ctx_len, DH_C), lambda b, h: (ctx_blk0 + b, h)),
            pl.BlockSpec((ctx_len, DH_C), lambda b, h: (ctx_blk0 + b, H_C + h)),
            pl.BlockSpec((ctx_len, DH_C), lambda b, h: (ctx_blk0 + b, 2 * H_C + h)),
            pl.BlockSpec((1, DH_C), lambda b, h: (0, 0)),
            pl.BlockSpec((1, DH_C), lambda b, h: (0, 0)),
            pl.BlockSpec(memory_space=pl.ANY),
        ],
        out_specs=pl.BlockSpec((ctx_len, DH_C), lambda b, h: (ctx_blk0 + b, h)),
        input_output_aliases={5: 0},
        compiler_params=_cparams(("parallel", "parallel")),
        name="na_ctx_attention",
    )(qkv, qkv, qkv, gq, gk, out_buf)


def _slab_copy(src_ref, src_tok, dst_ref, dst_tok, sem):
    def rows(tok):
        off = tok * SLAB_STRIDE
        return pl.ds(off if isinstance(off, int) else pl.multiple_of(off, 8), SLAB_ROWS)

    return pltpu.make_async_copy(src_ref.at[rows(src_tok), :], dst_ref.at[rows(dst_tok), :], sem)


def _load_slabs(ref, rows, dtype):
    return jnp.concatenate([ref[pl.ds(j, rows, stride=SLAB_STRIDE), :].astype(dtype) for j in range(SLAB_ROWS)],
                           axis=1)


def moe_slot_plan(route, counts, n_tiles):
    cnt = counts[0, :N_EXPERTS].astype(jnp.int32)
    padded = (cnt + MOE_TILE - 1) // MOE_TILE * MOE_TILE
    ends = jnp.cumsum(padded)
    base = ends - padded
    n_used = ends[-1] // MOE_TILE
    tile_start = jnp.arange(n_tiles, dtype=jnp.int32) * MOE_TILE
    last_start = jnp.maximum(ends[-1] - MOE_TILE, 0)
    tile_expert = jnp.searchsorted(ends, jnp.minimum(tile_start, last_start), side='right').astype(jnp.int32)
    tile_expert = jnp.minimum(tile_expert, N_EXPERTS - 1)
    ids = route[:, 0:2].astype(jnp.int32)
    pos = jnp.take(base, ids, axis=0) + route[:, 4:6].astype(jnp.int32)
    return pos[:, 0], pos[:, 1], tile_expert, n_used.reshape(1).astype(jnp.int32)


def _slot_source_body(p1_ref, p2_ref, src_ref, *, n_tok):
    def clear(i, c):
        src_ref[i] = 0
        return c

    lax.fori_loop(0, src_ref.shape[0], clear, 0, unroll=8)

    def fill(n, c):
        src_ref[p1_ref[n]] = n
        src_ref[p2_ref[n]] = n
        return c

    lax.fori_loop(0, n_tok, fill, 0, unroll=4)


def moe_slot_source(pos1, pos2, n_slots):
    smem = pl.BlockSpec(memory_space=pltpu.SMEM)
    return pl.pallas_call(
        functools.partial(_slot_source_body, n_tok=pos1.shape[0]),
        out_shape=jax.ShapeDtypeStruct((n_slots,), jnp.int32),
        in_specs=[smem, smem],
        out_specs=smem,
        name="moe_slot_source",
    )(pos1, pos2)


def _gather_slabs(src_ref, tok_ref, tok0, dst_ref, sem, n):
    def issue(r, c):
        _slab_copy(src_ref, tok_ref[tok0 + r], dst_ref, r, sem).start()
        return c

    lax.fori_loop(0, n, issue, 0, unroll=8)


def _wait_slabs(src_ref, dst_ref, sem, n):
    rows = pl.ds(0, n * SLAB_ROWS)
    pltpu.make_async_copy(src_ref.at[rows, :], dst_ref.at[rows, :], sem).wait()


def _moe_expert_body(te_ref, nu_ref, src_ref, hs_ref, wg_ref, wu_ref, wd_ref, o_ref, xbuf, wg_bf, wu_bf, wd_bf,
                     sems):
    t = pl.program_id(0)
    tm = MOE_TILE
    slot = t % 2

    @pl.when(t < nu_ref[0])
    def _():
        @pl.when(t == 0)
        def _():
            _gather_slabs(hs_ref, src_ref, 0, xbuf.at[0], sems.at[0], tm)

        @pl.when(t + 1 < nu_ref[0])
        def _():
            _gather_slabs(hs_ref, src_ref, (t + 1) * tm, xbuf.at[1 - slot], sems.at[1 - slot], tm)

        @pl.when((t == 0) | (te_ref[t] != te_ref[jnp.maximum(t - 1, 0)]))
        def _():
            wg_bf[...] = wg_ref[...].astype(BF16)
            wu_bf[...] = wu_ref[...].astype(BF16)
            wd_bf[...] = wd_ref[...].astype(BF16)

        _wait_slabs(hs_ref, xbuf.at[slot], sems.at[slot], tm)
        x = _load_slabs(xbuf.at[slot], tm, BF16)
        nt = (((1,), (1,)), ((), ()))
        a = lax.dot_general(x, wg_bf[...], nt, preferred_element_type=F32)
        u = lax.dot_general(x, wu_bf[...], nt, preferred_element_type=F32)
        y = (a * jax.nn.sigmoid(a) * u).astype(BF16)
        _store_slabs(o_ref, jnp.dot(y, wd_bf[...], preferred_element_type=F32), tm)

    @pl.when(t >= nu_ref[0])
    def _():
        o_ref[...] = jnp.zeros_like(o_ref)


def moe_experts(hs, src, tile_expert, n_used, w_gate_t, w_up_t, w_down, layer):
    n_tiles = tile_expert.shape[0]
    f, d = w_gate_t.shape[-2:]
    blk = MOE_TILE * SLAB_STRIDE
    wspec = pl.BlockSpec((None, None, f, d), lambda t, te, nu, src: (layer, te[t], 0, 0))
    grid_spec = pltpu.PrefetchScalarGridSpec(
        num_scalar_prefetch=3,
        grid=(n_tiles,),
        in_specs=[pl.BlockSpec(memory_space=pl.ANY), wspec, wspec, wspec],
        out_specs=pl.BlockSpec((blk, 128), lambda t, te, nu, src: (t, 0)),
        scratch_shapes=[pltpu.VMEM((2, blk, 128), F32), pltpu.VMEM((f, d), BF16), pltpu.VMEM((f, d), BF16),
                        pltpu.VMEM((f, d), BF16), pltpu.SemaphoreType.DMA((2,))],
    )
    return pl.pallas_call(
        _moe_expert_body,
        out_shape=jax.ShapeDtypeStruct((n_tiles * blk, 128), F32),
        grid_spec=grid_spec,
        compiler_params=_dma_cparams(("arbitrary",)),
        name="moe_experts",
    )(tile_expert, n_used, src, hs, w_gate_t, w_up_t, w_down)


def _moe_combine_body(p1_ref, p2_ref, ys_ref, route_ref, res_ref, gate_ref, o_ref, abuf, bbuf, sems):
    i = pl.program_id(0)
    rb = ROW_BLOCK
    slot = i % 2

    def gather(blk, s):
        _gather_slabs(ys_ref, p1_ref, blk * rb, abuf.at[s], sems.at[0, s], rb)
        _gather_slabs(ys_ref, p2_ref, blk * rb, bbuf.at[s], sems.at[1, s], rb)

    @pl.when(i == 0)
    def _():
        gather(0, 0)

    @pl.when(i + 1 < pl.num_programs(0))
    def _():
        gather(i + 1, 1 - slot)

    route = route_ref[...]
    lane = lax.broadcasted_iota(jnp.int32, route.shape, 1)
    w1 = jnp.sum(jnp.where(lane == 2, route, 0.0), axis=1, keepdims=True)
    w2 = jnp.sum(jnp.where(lane == 3, route, 0.0), axis=1, keepdims=True)
    _wait_slabs(ys_ref, abuf.at[slot], sems.at[0, slot], rb)
    _wait_slabs(ys_ref, bbuf.at[slot], sems.at[1, slot], rb)
    for j in range(SLAB_ROWS):
        sl = slice(j * 128, (j + 1) * 128)
        rows = pl.ds(j, rb, stride=SLAB_STRIDE)
        mix = w1 * abuf[slot, rows, :] + w2 * bbuf[slot, rows, :]
        o_ref[:, sl] = res_ref[:, sl] + gate_ref[0][:, sl] * mix


def moe_combine(ys, pos1, pos2, route, resid, mods, k_gate, dims):
    n, d = route.shape[0], resid.shape[1]
    n_lat, seq, batch = dims
    rb = ROW_BLOCK
    mrow = functools.partial(_mod_row, rows_per_block=rb, n_lat=n_lat, seq=seq, batch=batch)
    grid_spec = pltpu.PrefetchScalarGridSpec(
        num_scalar_prefetch=2,
        grid=(n // rb,),
        in_specs=[
            pl.BlockSpec(memory_space=pl.ANY),
            pl.BlockSpec((rb, 128), lambda i, p1, p2: (i, 0)),
            pl.BlockSpec((rb, d), lambda i, p1, p2: (i, 0)),
            pl.BlockSpec((1, 1, d), lambda i, p1, p2: (mrow(i) * 6 + k_gate, 0, 0)),
        ],
        out_specs=pl.BlockSpec((rb, d), lambda i, p1, p2: (i, 0)),
        scratch_shapes=[pltpu.VMEM((2, rb * SLAB_STRIDE, 128), F32), pltpu.VMEM((2, rb * SLAB_STRIDE, 128), F32),
                        pltpu.SemaphoreType.DMA((2, 2))],
    )
    return pl.pallas_call(
        _moe_combine_body,
        out_shape=jax.ShapeDtypeStruct((n, d), F32),
        grid_spec=grid_spec,
        compiler_params=_dma_cparams(("arbitrary",)),
        name="moe_combine",
    )(pos1, pos2, ys, route, resid, mods)


def _swap16(a):
    lead = a.shape[:-1]
    return jnp.flip(a.reshape(*lead, D_ROPE // 32, 2, 16), axis=-2).reshape(*lead, D_ROPE)


W_IN_TILE = 512
_C_KR = Q_LORA + KV_LORA
_C_QB = _C_KR + D_ROPE
_C_G = _C_QB + 2 * H_B * DQK_B + 2 * W_B


def _w_in_tail_t(wt_all, j):
    wt = wt_all[j]
    kr = wt[_C_KR:_C_QB]
    kr_sw = jnp.flip(kr.reshape(D_ROPE // 32, 2, 16, -1), axis=1).reshape(D_ROPE, -1)
    gates = wt[_C_G:]
    pad = W_IN_TILE - 2 * D_ROPE - gates.shape[0]
    return jnp.concatenate([kr, kr_sw, jnp.pad(gates, ((0, pad), (0, 0)))], axis=0)


def _win_body(h_ref, wt_ref, tail_ref, o_ref, wbf_ref, *, n_main):
    j = pl.program_id(0)

    @pl.when((pl.program_id(1) == 0) & (j < n_main))
    def _():
        wbf_ref[...] = wt_ref[0].astype(BF16)

    @pl.when((pl.program_id(1) == 0) & (j >= n_main))
    def _():
        wbf_ref[...] = tail_ref[...].astype(BF16)

    o_ref[...] = lax.dot_general(h_ref[...], wbf_ref[...], (((1,), (1,)), ((), ())),
                                 preferred_element_type=F32).astype(o_ref.dtype)


def input_projection(h, wt_all, layer):
    m, d = h.shape
    tm = _row_tile(m)
    tn = W_IN_TILE
    n_a = (_C_G - _C_QB) // tn
    n_b = _C_KR // tn
    n_main = n_a + n_b
    assert (_C_G - _C_QB) % tn == 0 and _C_KR % tn == 0 and (n_main + 1) * tn == P_WIDTH and _C_QB % 8 == 0

    def w_row(j):
        row = jnp.where(j < n_a, _C_QB + j * tn, jnp.where(j < n_main, (j - n_a) * tn, 0))
        return pl.multiple_of(row, 8)

    return pl.pallas_call(
        functools.partial(_win_body, n_main=n_main),
        out_shape=jax.ShapeDtypeStruct((m, P_WIDTH), F32),
        grid=(n_main + 1, m // tm),
        in_specs=[
            pl.BlockSpec((tm, d), lambda j, i: (i, 0)),
            pl.BlockSpec((pl.Element(1), pl.Element(tn), pl.Element(d)), lambda j, i: (layer, w_row(j), 0)),
            pl.BlockSpec((tn, d), lambda j, i: (0, 0), pipeline_mode=pl.Buffered(1)),
        ],
        out_specs=pl.BlockSpec((tm, tn), lambda j, i: (i, j)),
        scratch_shapes=[pltpu.VMEM((tn, d), BF16)],
        compiler_params=_cparams(("arbitrary", "arbitrary")),
        name="input_projection",
    )(h, wt_all, _w_in_tail_t(wt_all, layer))


def _ext_w_q_up(w):
    w3 = w.reshape(Q_LORA, H_A, D_NOPE + D_ROPE)
    return jnp.concatenate([w3, _swap16(w3[:, :, D_NOPE:])], axis=2).reshape(Q_LORA, H_A * 256)


def _ext_gain(g):
    return jnp.concatenate([g, _swap16(g[D_NOPE:])]).reshape(1, 256)


def _rope_tables(seq):
    t = jnp.arange(seq)
    half = D_ROPE // 4
    inv = ROPE_THETA ** (-jnp.arange(half, dtype=F32) / half)
    ar = (t // GRID_W).astype(F32)[:, None] * inv[None, :]
    ac = (t % GRID_W).astype(F32)[:, None] * inv[None, :]
    cr, sr, cc, sc = jnp.cos(ar), jnp.sin(ar), jnp.cos(ac), jnp.sin(ac)
    return jnp.concatenate([cr, cr, cc, cc, -sr, sr, -sc, sc], axis=1)


def kernel(x, c, ctx, c_ctx, ada_w, ada_b, norm1_g, norm2_g, ab_w_in, mla_q_norm_g, mla_kv_norm_g, mla_w_q_up, mla_w_kv_up, mla_qn_g, mla_kn_g, mlstm_conv_w, mlstm_gate_b, mlstm_out_norm_g, ab_w_out, na_w_qkv, na_qn_g, na_kn_g, na_rpb, na_w_out, moe_w_rg, moe_b_rg, moe_w_re, moe_b_re, moe_w_gate, moe_w_up, moe_w_down):
    batch, seq, d = x.shape
    ctx_len = ctx.shape[1]
    depth = ada_w.shape[0]
    n_lat = batch * seq
    dims = (n_lat, seq, batch)
    assert d == D_MODEL and batch + 1 <= MOD_ROWS and seq % 512 == 0 and ctx_len == ROW_BLOCK

    xs = jnp.concatenate([x.reshape(n_lat, d), ctx.reshape(batch * ctx_len, d)], axis=0)
    cvec = jnp.concatenate([c, c_ctx[None, :]], axis=0)
    mods_all = ada_mods(cvec, ada_w, ada_b).reshape(depth, MOD_ROWS * 6, 1, d)
    trig = _rope_tables(seq)

    w_gate_t, w_up_t = jnp.swapaxes(moe_w_gate, 2, 3), jnp.swapaxes(moe_w_up, 2, 3)
    w_in_t = jnp.swapaxes(ab_w_in, 1, 2)
    kw = dict(batch=batch, seq=seq, ctx_len=ctx_len)
    for layer in range(depth):
        j = layer // 2
        last = layer == depth - 1
        mods = mods_all[layer]
        h = norm_mod(xs, norm1_g[layer], mods, (0, 1), dims)
        if layer % 2 == 0:
            p_all = input_projection(h, w_in_t, j)
            tm_big = _row_tile(p_all.shape[0])
            q_all = matmul(p_all, _ext_w_q_up(mla_w_q_up[j]), out_dtype=BF16, a_col=P_CQ // Q_LORA, k=Q_LORA,
                           rms_gain=mla_q_norm_g[j], tm=tm_big)
            kv_all = matmul(p_all, mla_w_kv_up, layer=j, out_dtype=BF16, a_col=P_CKV // KV_LORA, k=KV_LORA,
                            rms_gain=mla_kv_norm_g[j], tm=tm_big)
            gq, gk = _ext_gain(mla_qn_g[j]), _ext_gain(mla_kn_g[j])
            mix_a = jnp.zeros((xs.shape[0], W_A), BF16)
            mix_a = mla_attention(q_all, kv_all, p_all, trig, gq, gk, latent_queries=True, out_buf=mix_a, **kw)
            mix_a = mla_attention(q_all, kv_all, p_all, trig, gq, gk, latent_queries=False, out_buf=mix_a, **kw)
            qk = conv_silu(p_all, mlstm_conv_w[j], n_lat=n_lat, seq=seq, ctx_len=ctx_len)
            gates_t = p_all[:, P_GATE:P_GATE + 4 * H_B].T
            h_fwd, h_bwd = mlstm_scan(qk, p_all, gates_t, mlstm_gate_b[j], **kw)
            mix_b = mlstm_finish(h_fwd, h_bwd, p_all, mlstm_out_norm_g[j])
            xs = matmul(mix_a, ab_w_out, layer=j, a2=mix_b, out_dtype=F32, resid=xs, mods=mods, k_gate=2,
                        dims=dims)
        else:
            qkv = matmul(h, na_w_qkv, layer=j, out_dtype=BF16, tm=_row_tile(h.shape[0]))
            gq, gk = na_qn_g[j].reshape(1, DH_C), na_kn_g[j].reshape(1, DH_C)
            bias = na_bias_rows(na_rpb[j])
            if last:
                mixed = na_attention(qkv, bias, gq, gk, **kw)
            else:
                mixed = jnp.zeros((xs.shape[0], H_C * DH_C), BF16)
                mixed = na_attention(qkv, bias, gq, gk, out_buf=mixed, **kw)
                mixed = na_ctx_attention(qkv, gq, gk, mixed, **kw)
            xs = matmul(mixed, na_w_out, layer=j, out_dtype=F32, resid=xs, mods=mods, k_gate=2, dims=dims)

        w_r = jnp.pad(jnp.concatenate([moe_w_re[layer], moe_w_rg[layer]], axis=1),
                      ((0, 0), (0, 128 - N_EXPERTS - N_GROUPS)))
        b_r = jnp.pad(jnp.concatenate([moe_b_re[layer], moe_b_rg[layer]]), (0, 128 - N_EXPERTS - N_GROUPS))
        hs, route, counts = norm_router(xs, norm2_g[layer], mods, (3, 4), dims, w_r, b_r.reshape(1, 128))
        n_tiles = 2 * xs.shape[0] // MOE_TILE + N_EXPERTS
        pos1, pos2, tile_expert, n_used = moe_slot_plan(route, counts, n_tiles)
        src = moe_slot_source(pos1, pos2, n_tiles * MOE_TILE)
        ys = moe_experts(hs, src, tile_expert, n_used, w_gate_t, w_up_t, moe_w_down, layer)
        xs = moe_combine(ys, pos1, pos2, route, xs, mods, 5, dims)
    return xs[:n_lat].reshape(batch, seq, d) if xs.shape[0] != n_lat else xs.reshape(batch, seq, d)
```

```python
import functools

import numpy as np
import jax
import jax.numpy as jnp
from jax import lax
from jax.experimental import pallas as pl
from jax.experimental.pallas import tpu as pltpu

F32 = jnp.float32
BF16 = jnp.bfloat16

D_MODEL = 4096
GRID_W = 64
H_A = 16
Q_LORA = 1024
KV_LORA = 512
D_NOPE = 128
D_ROPE = 64
D_V_A = 128
H_B = 4
DQK_B = 256
DV_B = 512
MLSTM_CHUNK = 128
M_INIT = -1e30
H_C = 32
DH_C = 128
NA_KH = 8
NA_KW = 16
N_GROUPS = 4
EXPERTS_PER_GROUP = 8
N_EXPERTS = 32
D_FF_EXPERT = 192
ROPE_THETA = 10000.0
NORM_EPS = 1e-6
W_A = H_A * D_V_A
W_B = H_B * DV_B
NEG = -1e30

VMEM_LIMIT_BYTES = 56 * 1024 * 1024
ROW_BLOCK = 256
MOD_ROWS = 8
SLAB_ROWS = 32
SLAB_STRIDE = 40
MOE_TILE = 256

P_QK = 0
P_V = 2048
P_O = 4096
P_CQ = 6144
P_CKV = 7168
P_KR = 7680
P_GATE = 7808
P_WIDTH = 8192


def _cparams(sem):
    return pltpu.CompilerParams(dimension_semantics=sem, vmem_limit_bytes=VMEM_LIMIT_BYTES)


MAX_ROW_TILE = 1100


def _row_tile(m, limit=MAX_ROW_TILE):
    return max(t for t in range(16, limit + 1, 16) if m % t == 0)


def _dma_cparams(sem):
    return pltpu.CompilerParams(dimension_semantics=sem, vmem_limit_bytes=VMEM_LIMIT_BYTES,
                                disable_bounds_checks=True)


def _mod_row(i, rows_per_block, n_lat, seq, batch):
    return jnp.where(i < n_lat // rows_per_block, i // (seq // rows_per_block), batch)


ADA_ROW_CHUNK = 512


def _ada_body(c_ref, w_ref, b_ref, o_ref, sc_ref, *, n_vec):
    d, tn = w_ref.shape[1], w_ref.shape[2]
    reps = tn // 128

    @pl.when((pl.program_id(0) == 0) & (pl.program_id(1) == 0))
    def _():
        c = c_ref[...]
        sc_ref[...] = c * jax.nn.sigmoid(c)

    def chunk(ci, accs):
        rows = pl.ds(pl.multiple_of(ci * ADA_ROW_CHUNK, ADA_ROW_CHUNK), ADA_ROW_CHUNK)
        w = w_ref[0, rows, :]
        out = []
        for m in range(n_vec):
            cm = jnp.concatenate([sc_ref[m, rows, :]] * reps, axis=1)
            part = (w * cm).reshape(ADA_ROW_CHUNK // 8, 8, tn).sum(axis=0)
            out.append(accs[m] + part)
        return tuple(out)

    accs = lax.fori_loop(0, d // ADA_ROW_CHUNK, chunk, tuple(jnp.zeros((8, tn), F32) for _ in range(n_vec)))
    rows = [jnp.sum(a, axis=0, keepdims=True) for a in accs]
    rows += [jnp.zeros((1, tn), F32)] * (MOD_ROWS - n_vec)
    o_ref[0] = jnp.concatenate(rows, axis=0) + b_ref[0]


def ada_mods(cvec, ada_w, ada_b, tn=512):
    depth, d, n6 = ada_w.shape
    n_vec = cvec.shape[0]
    c_lanes = jnp.broadcast_to(cvec[:, :, None], (n_vec, d, 128))
    return pl.pallas_call(
        functools.partial(_ada_body, n_vec=n_vec),
        out_shape=jax.ShapeDtypeStruct((depth, MOD_ROWS, n6), F32),
        grid=(depth, n6 // tn),
        in_specs=[
            pl.BlockSpec((n_vec, d, 128), lambda l, j: (0, 0, 0)),
            pl.BlockSpec((1, d, tn), lambda l, j: (l, 0, j)),
            pl.BlockSpec((1, 1, tn), lambda l, j: (l, 0, j)),
        ],
        out_specs=pl.BlockSpec((1, MOD_ROWS, tn), lambda l, j: (l, 0, j)),
        scratch_shapes=[pltpu.VMEM((n_vec, d, 128), F32)],
        compiler_params=_cparams(("arbitrary", "arbitrary")),
        name="ada_mods",
    )(c_lanes, ada_w, ada_b.reshape(depth, 1, n6))


def _modulated_norm(x_ref, g_ref, sh_ref, sc_ref):
    x = x_ref[...]
    ms = jnp.mean(x * x, axis=-1, keepdims=True)
    y = x * lax.rsqrt(ms + NORM_EPS) * g_ref[...]
    return y * (1.0 + sc_ref[0]) + sh_ref[0]


def _norm_body(x_ref, g_ref, sh_ref, sc_ref, o_ref):
    o_ref[...] = _modulated_norm(x_ref, g_ref, sh_ref, sc_ref).astype(o_ref.dtype)


def _store_slabs(ref, mat, rows):
    for j in range(SLAB_ROWS):
        ref[pl.ds(j, rows, stride=SLAB_STRIDE), :] = mat[:, j * 128:(j + 1) * 128]
    for j in range(SLAB_ROWS, SLAB_STRIDE):
        ref[pl.ds(j, rows, stride=SLAB_STRIDE), :] = jnp.zeros((rows, 128), mat.dtype)


def _norm_router_body(x_ref, g_ref, sh_ref, sc_ref, wr_ref, br_ref, hs_ref, route_ref, cnt_ref, carry):
    @pl.when(pl.program_id(0) == 0)
    def _():
        carry[...] = jnp.zeros_like(carry)

    h = _modulated_norm(x_ref, g_ref, sh_ref, sc_ref)
    _store_slabs(hs_ref, h, ROW_BLOCK)
    logits = jnp.dot(h, wr_ref[...], precision=lax.Precision.HIGHEST,
                     preferred_element_type=F32) + br_ref[...]
    lane = lax.broadcasted_iota(jnp.int32, logits.shape, 1)
    big = jnp.int32(1 << 20)
    is_g = (lane >= N_EXPERTS) & (lane < N_EXPERTS + N_GROUPS)
    gl = jnp.where(is_g, logits, NEG)
    gmax = jnp.max(gl, axis=-1, keepdims=True)
    g_idx = jnp.min(jnp.where(gl == gmax, lane, big), axis=-1, keepdims=True) - N_EXPERTS
    p_g = 1.0 / jnp.sum(jnp.where(is_g, jnp.exp(gl - gmax), 0.0), axis=-1, keepdims=True)
    lo = g_idx * EXPERTS_PER_GROUP
    in_grp = (lane >= lo) & (lane < lo + EXPERTS_PER_GROUP)
    el = jnp.where(in_grp, logits, NEG)
    e1 = jnp.max(el, axis=-1, keepdims=True)
    i1 = jnp.min(jnp.where(el == e1, lane, big), axis=-1, keepdims=True)
    el2 = jnp.where(lane == i1, NEG, el)
    e2 = jnp.max(el2, axis=-1, keepdims=True)
    i2 = jnp.min(jnp.where(el2 == e2, lane, big), axis=-1, keepdims=True)
    t = jnp.exp(e2 - e1)
    w1 = p_g / (1.0 + t)
    w2 = w1 * t
    onehot = jnp.where((lane == i1) | (lane == i2), 1.0, 0.0)
    t_i = lax.broadcasted_iota(jnp.int32, (ROW_BLOCK, ROW_BLOCK), 0)
    s_i = lax.broadcasted_iota(jnp.int32, (ROW_BLOCK, ROW_BLOCK), 1)
    before = jnp.where(s_i < t_i, 1.0, 0.0).astype(BF16)
    seen = jnp.dot(before, onehot.astype(BF16), preferred_element_type=F32) + carry[...]
    rank1 = jnp.sum(jnp.where(lane == i1, seen, 0.0), axis=-1, keepdims=True)
    rank2 = jnp.sum(jnp.where(lane == i2, seen, 0.0), axis=-1, keepdims=True)
    total = carry[...] + jnp.sum(onehot, axis=0, keepdims=True)
    carry[...] = total
    cnt_ref[...] = jnp.broadcast_to(total, cnt_ref.shape)
    cols = (i1.astype(F32), i2.astype(F32), w1, w2, rank1, rank2)
    route = jnp.zeros(logits.shape, F32)
    for k, v in enumerate(cols):
        route = jnp.where(lane == k, v, route)
    route_ref[...] = route


def _norm_specs(d, mods_k, dims):
    n_lat, seq, batch = dims
    mrow = functools.partial(_mod_row, rows_per_block=ROW_BLOCK, n_lat=n_lat, seq=seq, batch=batch)
    k_shift, k_scale = mods_k
    return [
        pl.BlockSpec((ROW_BLOCK, d), lambda i: (i, 0)),
        pl.BlockSpec((1, d), lambda i: (0, 0)),
        pl.BlockSpec((1, 1, d), lambda i: (mrow(i) * 6 + k_shift, 0, 0)),
        pl.BlockSpec((1, 1, d), lambda i: (mrow(i) * 6 + k_scale, 0, 0)),
    ]


def norm_mod(x, g, mods, mods_k, dims):
    n, d = x.shape
    return pl.pallas_call(
        _norm_body,
        out_shape=jax.ShapeDtypeStruct((n, d), BF16),
        grid=(n // ROW_BLOCK,),
        in_specs=_norm_specs(d, mods_k, dims),
        out_specs=pl.BlockSpec((ROW_BLOCK, d), lambda i: (i, 0)),
        compiler_params=_cparams(("parallel",)),
        name="norm_mod",
    )(x, g.reshape(1, d), mods, mods)


def norm_router(x, g, mods, mods_k, dims, w_r, b_r):
    n, d = x.shape
    assert d == SLAB_ROWS * 128
    return pl.pallas_call(
        _norm_router_body,
        out_shape=(jax.ShapeDtypeStruct((n * SLAB_STRIDE, 128), F32), jax.ShapeDtypeStruct((n, 128), F32),
                   jax.ShapeDtypeStruct((8, 128), F32)),
        grid=(n // ROW_BLOCK,),
        in_specs=_norm_specs(d, mods_k, dims) + [
            pl.BlockSpec((d, 128), lambda i: (0, 0)),
            pl.BlockSpec((1, 128), lambda i: (0, 0)),
        ],
        out_specs=(pl.BlockSpec((ROW_BLOCK * SLAB_STRIDE, 128), lambda i: (i, 0)),
                   pl.BlockSpec((ROW_BLOCK, 128), lambda i: (i, 0)),
                   pl.BlockSpec((8, 128), lambda i: (0, 0))),
        scratch_shapes=[pltpu.VMEM((1, 128), F32)],
        compiler_params=_cparams(("arbitrary",)),
        name="norm_router",
    )(x, g.reshape(1, d), mods, mods, w_r, b_r)


def _mm_body(*refs, rms, gated, k1, two_a):
    it = iter(refs)
    a_ref = next(it)
    a2_ref = next(it) if two_a else None
    g_ref = next(it) if rms else None
    w_ref = next(it)
    res_ref = next(it) if gated else None
    gate_ref = next(it) if gated else None
    o_ref = next(it)
    wbf_ref = next(it)

    @pl.when(pl.program_id(1) == 0)
    def _():
        wbf_ref[...] = w_ref[...].astype(BF16)

    a = a_ref[...]
    if rms:
        af = a.astype(F32)
        ms = jnp.mean(af * af, axis=-1, keepdims=True)
        a = (af * lax.rsqrt(ms + NORM_EPS) * g_ref[...]).astype(BF16)
    if two_a:
        acc = (jnp.dot(a, wbf_ref[:k1, :], preferred_element_type=F32)
               + jnp.dot(a2_ref[...], wbf_ref[k1:, :], preferred_element_type=F32))
    else:
        acc = jnp.dot(a, wbf_ref[...], preferred_element_type=F32)
    if gated:
        acc = res_ref[...] + gate_ref[0] * acc
    o_ref[...] = acc.astype(o_ref.dtype)


def matmul(a, w, *, out_dtype, layer=None, a_col=0, k=None, a2=None, rms_gain=None, resid=None, mods=None,
           k_gate=None, dims=None, tm=512, tn=512):
    m = a.shape[0]
    kw_, nw = w.shape[-2], w.shape[-1]
    two_a = a2 is not None
    k = (a.shape[1] if two_a else kw_) if k is None else k
    assert kw_ == k + (a2.shape[1] if two_a else 0) and m % tm == 0 and nw % tn == 0
    rms = rms_gain is not None
    gated = resid is not None
    in_specs = [pl.BlockSpec((tm, k), lambda j, i: (i, a_col))]
    args = [a]
    if two_a:
        in_specs.append(pl.BlockSpec((tm, a2.shape[1]), lambda j, i: (i, 0)))
        args.append(a2)
    if rms:
        in_specs.append(pl.BlockSpec((1, k), lambda j, i: (0, 0)))
        args.append(rms_gain.reshape(1, k))
    if layer is None:
        in_specs.append(pl.BlockSpec((kw_, tn), lambda j, i: (0, j)))
    else:
        in_specs.append(pl.BlockSpec((None, kw_, tn), lambda j, i: (layer, 0, j)))
    args.append(w)
    if gated:
        n_lat, seq, batch = dims
        mrow = functools.partial(_mod_row, rows_per_block=tm, n_lat=n_lat, seq=seq, batch=batch)
        in_specs.append(pl.BlockSpec((tm, tn), lambda j, i: (i, j)))
        in_specs.append(pl.BlockSpec((1, 1, tn), lambda j, i: (mrow(i) * 6 + k_gate, 0, j)))
        args += [resid, mods]
    return pl.pallas_call(
        functools.partial(_mm_body, rms=rms, gated=gated, k1=k, two_a=two_a),
        out_shape=jax.ShapeDtypeStruct((m, nw), out_dtype),
        grid=(nw // tn, m // tm),
        in_specs=in_specs,
        out_specs=pl.BlockSpec((tm, tn), lambda j, i: (i, j)),
        scratch_shapes=[pltpu.VMEM((kw_, tn), BF16)],
        compiler_params=_cparams(("arbitrary", "arbitrary")),
        name="matmul",
    )(*args)


def _rope_rot(pr, trig):
    if trig is not None:
        pr = pr * trig
    else:
        lane = lax.broadcasted_iota(jnp.int32, pr.shape, 1)
        pr = jnp.where(lane < D_ROPE, pr, 0.0)
    return pr + pltpu.roll(pr, D_ROPE, axis=1)


def _mla_body(*refs, n_lat_keys, n_ctx_keys, rope_q, scale, key_chunk, aliased):
    it = iter(refs)
    q_ref = next(it)
    kvl_ref = next(it) if n_lat_keys else None
    krl_ref = next(it) if n_lat_keys else None
    tkl_ref = next(it) if n_lat_keys else None
    kvc_ref = next(it)
    krc_ref = next(it)
    tq_ref = next(it) if rope_q else None
    gq_ref = next(it)
    gk_ref = next(it)
    if aliased:
        next(it)
    o_ref = next(it)
    kbuf = next(it)
    vbuf = next(it)
    dqk = D_NOPE + D_ROPE

    def prep_keys(kv_ref, kr_ref, trig_ref, off, n):
        gk = gk_ref[...]

        def chunk(c, carry):
            rows = pl.ds(pl.multiple_of(c * key_chunk, key_chunk), key_chunk)
            kv = kv_ref[rows, :]
            kn = kv[:, :D_NOPE].astype(F32)
            kr = kr_ref[rows, :].astype(F32)
            lane = lax.broadcasted_iota(jnp.int32, kr.shape, 1)
            ss = (jnp.sum(kn * kn, axis=-1, keepdims=True)
                  + jnp.sum(jnp.where(lane < D_ROPE, kr * kr, 0.0), axis=-1, keepdims=True))
            r = lax.rsqrt(ss / dqk + NORM_EPS)
            trig = trig_ref[rows, :] if trig_ref is not None else None
            rot = _rope_rot(kr * (r * gk[:, D_NOPE:]), trig)
            orow = pl.ds(pl.multiple_of(off + c * key_chunk, key_chunk), key_chunk)
            kbuf[orow, :D_NOPE] = (kn * (r * gk[:, :D_NOPE])).astype(BF16)
            kbuf[orow, D_NOPE:] = rot.astype(BF16)
            vbuf[orow, :] = kv[:, D_NOPE:].astype(BF16)
            return carry

        lax.fori_loop(0, n // key_chunk, chunk, 0)

    @pl.when(pl.program_id(2) == 0)
    def _():
        if n_lat_keys:
            prep_keys(kvl_ref, krl_ref, tkl_ref, 0, n_lat_keys)
        prep_keys(kvc_ref, krc_ref, None, n_lat_keys, n_ctx_keys)

    gq = gq_ref[...]
    for sb in range(q_ref.shape[0] // MLA_SUB_Q):
        rows = slice(sb * MLA_SUB_Q, (sb + 1) * MLA_SUB_Q)
        q = q_ref[rows, :].astype(F32)
        qn = q[:, :D_NOPE]
        qr = q[:, D_NOPE:]
        lane = lax.broadcasted_iota(jnp.int32, qr.shape, 1)
        ss = (jnp.sum(qn * qn, axis=-1, keepdims=True)
              + jnp.sum(jnp.where(lane < D_ROPE, qr * qr, 0.0), axis=-1, keepdims=True))
        r = lax.rsqrt(ss / dqk + NORM_EPS) * scale
        rot = _rope_rot(qr * (r * gq[:, D_NOPE:]), tq_ref[rows, :] if rope_q else None)
        rot = jnp.where(lane < D_ROPE, rot, 0.0)
        q2 = jnp.concatenate([qn * (r * gq[:, :D_NOPE]), rot], axis=1).astype(BF16)
        s = lax.dot_general(q2, kbuf[...], (((1,), (1,)), ((), ())), preferred_element_type=F32)
        m = jnp.max(s, axis=-1, keepdims=True)
        p = jnp.exp2(s - m)
        l = jnp.sum(p, axis=-1, keepdims=True)
        o = jnp.dot(p.astype(BF16), vbuf[...], preferred_element_type=F32)
        o_ref[rows, :] = (o / l).astype(o_ref.dtype)


MLA_SUB_Q = 256


def mla_attention(q_all, kv_all, p_all, trig, gq, gk, *, batch, seq, ctx_len, latent_queries, out_buf=None):
    n_lat = batch * seq
    kr_blk = P_KR // 128
    scale = (D_NOPE + D_ROPE) ** -0.5 * float(np.log2(np.e))
    if latent_queries:
        tq = 4 * MLA_SUB_Q
        nq, n_lat_keys, q_row0 = seq, seq, 0
    else:
        tq = ctx_len
        nq, n_lat_keys, q_row0 = ctx_len, 0, n_lat // tq
    assert tq % MLA_SUB_Q == 0 and nq % tq == 0
    nqb = nq // tq
    ctx_blk0 = n_lat // ctx_len
    in_specs = [pl.BlockSpec((tq, 256), lambda b, h, i: (q_row0 + b * nqb + i, h))]
    args = [q_all]
    if latent_queries:
        in_specs += [
            pl.BlockSpec((seq, 256), lambda b, h, i: (b, h)),
            pl.BlockSpec((seq, 128), lambda b, h, i: (b, kr_blk)),
            pl.BlockSpec((seq, 128), lambda b, h, i: (0, 0)),
        ]
        args += [kv_all, p_all, trig]
    in_specs += [
        pl.BlockSpec((ctx_len, 256), lambda b, h, i: (ctx_blk0 + b, h)),
        pl.BlockSpec((ctx_len, 128), lambda b, h, i: (ctx_blk0 + b, kr_blk)),
    ]
    args += [kv_all, p_all]
    if latent_queries:
        in_specs.append(pl.BlockSpec((tq, 128), lambda b, h, i: (i, 0)))
        args.append(trig)
    in_specs += [pl.BlockSpec((1, 256), lambda b, h, i: (0, 0))] * 2
    args += [gq, gk]
    aliases = {}
    if out_buf is not None:
        in_specs.append(pl.BlockSpec(memory_space=pl.ANY))
        aliases = {len(args): 0}
        args.append(out_buf)
    nk = n_lat_keys + ctx_len
    body = functools.partial(_mla_body, n_lat_keys=n_lat_keys, n_ctx_keys=ctx_len, rope_q=latent_queries,
                             scale=scale, key_chunk=min(256, ctx_len), aliased=out_buf is not None)
    return pl.pallas_call(
        body,
        out_shape=jax.ShapeDtypeStruct((q_all.shape[0], W_A), BF16),
        grid=(batch, H_A, nqb),
        in_specs=in_specs,
        out_specs=pl.BlockSpec((tq, D_V_A), lambda b, h, i: (q_row0 + b * nqb + i, h)),
        scratch_shapes=[pltpu.VMEM((nk, 256), BF16), pltpu.VMEM((nk, D_V_A), BF16)],
        input_output_aliases=aliases,
        compiler_params=_cparams(("arbitrary", "arbitrary", "arbitrary")),
        name="mla_attention",
    )(*args)


def _conv_silu_body(x_ref, prev_ref, next_ref, w_ref, o_ref, *, lat_blocks, seq_blocks, ctx_blocks):
    i = pl.program_id(0)
    j = jnp.where(i < lat_blocks, i % seq_blocks, (i - lat_blocks) % ctx_blocks)
    nb = jnp.where(i < lat_blocks, seq_blocks, ctx_blocks)
    x = x_ref[...]
    rows = x.shape[0]
    row = lax.broadcasted_iota(jnp.int32, x.shape, 0)
    prev_row = jnp.where(j == 0, 0.0, prev_ref[7:8, :])
    next_row = jnp.where(j == nb - 1, 0.0, next_ref[0:1, :])
    xp = jnp.where(row == 0, prev_row, pltpu.roll(x, 1, axis=0))
    xn = jnp.where(row == rows - 1, next_row, pltpu.roll(x, rows - 1, axis=0))
    w = w_ref[...]
    y = xp * w[0:1, :] + x * w[1:2, :] + xn * w[2:3, :]
    y = y * jax.nn.sigmoid(y)
    half = y.shape[1] // 2
    o_ref[:, :half] = y[:, :half].astype(o_ref.dtype)
    o_ref[:, half:] = (y[:, half:] * (DQK_B ** -0.5)).astype(o_ref.dtype)


def conv_silu(p_all, conv_w, *, n_lat, seq, ctx_len):
    n = p_all.shape[0]
    width = 2 * H_B * DQK_B
    rb = ROW_BLOCK
    nblk = n // rb
    sub = rb // 8
    last8 = n // 8 - 1
    body = functools.partial(_conv_silu_body, lat_blocks=n_lat // rb, seq_blocks=seq // rb,
                             ctx_blocks=ctx_len // rb)
    return pl.pallas_call(
        body,
        out_shape=jax.ShapeDtypeStruct((n, width), BF16),
        grid=(nblk,),
        in_specs=[
            pl.BlockSpec((rb, width), lambda i: (i, P_QK // width)),
            pl.BlockSpec((8, width), lambda i: (jnp.maximum(i * sub - 1, 0), P_QK // width)),
            pl.BlockSpec((8, width), lambda i: (jnp.minimum((i + 1) * sub, last8), P_QK // width)),
            pl.BlockSpec((3, width), lambda i: (0, 0)),
        ],
        out_specs=pl.BlockSpec((rb, width), lambda i: (i, 0)),
        compiler_params=_cparams(("parallel",)),
        name="conv_silu",
    )(p_all, p_all, p_all, conv_w)


MLSTM_HEADS_PER_STEP = 2


def _log_sigmoid(x):
    return jnp.minimum(x, 0.0) - jnp.log(1.0 + jnp.exp(-jnp.abs(x)))


def _mlstm_body(qf, kf, vf, gcf, grf, qb, kb, vb, gcb, grb, bc_ref, br_ref, of_ref, ob_ref, c_st, n_st, m_st):
    @pl.when(pl.program_id(2) == 0)
    def _():
        c_st[...] = jnp.zeros_like(c_st)
        n_st[...] = jnp.zeros_like(n_st)
        m_st[...] = jnp.full_like(m_st, M_INIT)

    for hh in range(MLSTM_HEADS_PER_STEP):
        sf, sb = hh, MLSTM_HEADS_PER_STEP + hh
        _mlstm_chunk(0, hh, qf, kf, vf, gcf, grf, bc_ref, br_ref, of_ref, c_st.at[sf], n_st.at[sf], m_st.at[sf])
        _mlstm_chunk(1, hh, qb, kb, vb, gcb, grb, bc_ref, br_ref, ob_ref, c_st.at[sb], n_st.at[sb], m_st.at[sb])


def _mlstm_chunk(direction, hh, q_ref, k_ref, v_ref, gc_ref, gr_ref, bc_ref, br_ref, o_ref, c_st, n_st, m_st):
    h = pl.program_id(1) * MLSTM_HEADS_PER_STEP + hh
    qk_l = slice(hh * DQK_B, (hh + 1) * DQK_B)
    v_l = slice(hh * DV_B, (hh + 1) * DV_B)
    L = MLSTM_CHUNK
    hi = lax.Precision.HIGHEST
    gi = direction * (2 * H_B) + h
    gcol = gc_ref[...] + bc_ref[...]
    lane = lax.broadcasted_iota(jnp.int32, gcol.shape, 1)
    li_c = jnp.sum(jnp.where(lane == gi, gcol, 0.0), axis=1, keepdims=True)
    lf_c = _log_sigmoid(jnp.sum(jnp.where(lane == gi + H_B, gcol, 0.0), axis=1, keepdims=True))
    grow = gr_ref[...] + br_ref[...]
    sub = lax.broadcasted_iota(jnp.int32, grow.shape, 0)
    li_r = jnp.sum(jnp.where(sub == gi, grow, 0.0), axis=0, keepdims=True)
    lf_r = _log_sigmoid(jnp.sum(jnp.where(sub == gi + H_B, grow, 0.0), axis=0, keepdims=True))

    t_i = lax.broadcasted_iota(jnp.int32, (L, L), 0)
    s_i = lax.broadcasted_iota(jnp.int32, (L, L), 1)
    sgn = 1 - 2 * direction
    incl = (s_i - t_i) * sgn <= 0
    incl_f = incl.astype(F32)
    incl_t = ((t_i - s_i) * sgn <= 0).astype(F32)
    bcum_c = jnp.dot(incl_f, jnp.broadcast_to(lf_c, (L, L)), precision=hi, preferred_element_type=F32)
    bcum_r = jnp.dot(jnp.broadcast_to(lf_r, (8, L)), incl_t, precision=hi, preferred_element_type=F32)[0:1, :]
    b_last = jnp.sum(lf_r, axis=1, keepdims=True)

    m_prev = m_st[...]
    a_c = bcum_c[:, 0:1] + m_prev
    dlog = jnp.where(incl, bcum_c - bcum_r + li_r, -jnp.inf)
    m_t = jnp.maximum(a_c, jnp.max(dlog, axis=1, keepdims=True))
    w_intra = jnp.exp(dlog - m_t)
    w_inter = jnp.exp(a_c - m_t)

    q = q_ref[:, qk_l]
    k = k_ref[:, qk_l]
    v = v_ref[:, v_l].astype(BF16)
    c_prev = c_st[...]
    n_prev = n_st[...]
    s = lax.dot_general(q, k, (((1,), (1,)), ((), ())), preferred_element_type=F32) * w_intra
    num = (w_inter * jnp.dot(q, c_prev.astype(BF16), preferred_element_type=F32)
           + jnp.dot(s.astype(BF16), v, preferred_element_type=F32))
    den = (w_inter * jnp.sum(q.astype(F32) * n_prev, axis=1, keepdims=True)
           + jnp.sum(s, axis=1, keepdims=True))
    o_ref[:, v_l] = num / jnp.maximum(jnp.abs(den), jnp.exp(-m_t))

    g_r = b_last - bcum_r + li_r
    g_c = b_last - bcum_c[:, 0:1] + li_c
    m_new = jnp.maximum(b_last + m_prev, jnp.max(g_r, axis=1, keepdims=True))
    decay = jnp.exp(b_last + m_prev - m_new)
    kw = k.astype(F32) * jnp.exp(g_c - m_new)
    c_st[...] = decay * c_prev + lax.dot_general(kw.astype(BF16), v, (((0,), (0,)), ((), ())),
                                                 preferred_element_type=F32)
    n_st[...] = decay * n_prev + jnp.sum(kw, axis=0, keepdims=True)
    m_st[...] = m_new


def mlstm_scan(qk, p_all, gates_t, gate_b, *, batch, seq, ctx_len):
    n = qk.shape[0]
    L = MLSTM_CHUNK
    cc, lc = ctx_len // L, seq // L
    n_lat_blk = batch * lc

    def rowblk(b, d, c):
        ctx_blk = n_lat_blk + b * cc + (c if d == 0 else cc - 1 - c)
        lat_blk = b * lc + (c - cc if d == 0 else lc - 1 - (c - cc))
        return jnp.where(c < cc, ctx_blk, lat_blk)

    hps = MLSTM_HEADS_PER_STEP
    wqk, wv = hps * DQK_B, hps * DV_B
    assert H_B % hps == 0

    def chunk_specs(d):
        return [
            pl.BlockSpec((L, wqk), lambda b, h, c: (rowblk(b, d, c), h)),
            pl.BlockSpec((L, wqk), lambda b, h, c: (rowblk(b, d, c), H_B // hps + h)),
            pl.BlockSpec((L, wv), lambda b, h, c: (rowblk(b, d, c), P_V // wv + h)),
            pl.BlockSpec((L, 128), lambda b, h, c: (rowblk(b, d, c), P_GATE // 128)),
            pl.BlockSpec((4 * H_B, L), lambda b, h, c: (0, rowblk(b, d, c))),
        ]

    def out_spec(d):
        return pl.BlockSpec((L, wv), lambda b, h, c: (rowblk(b, d, c), h))

    gb_col = jnp.zeros((1, 128), F32).at[0, :4 * H_B].set(gate_b)
    gb_row = gate_b.reshape(4 * H_B, 1)
    chunk_args = [qk, qk, p_all, p_all, gates_t]
    return pl.pallas_call(
        _mlstm_body,
        out_shape=(jax.ShapeDtypeStruct((n, W_B), F32), jax.ShapeDtypeStruct((n, W_B), F32)),
        grid=(batch, H_B // hps, cc + lc),
        in_specs=chunk_specs(0) + chunk_specs(1) + [
            pl.BlockSpec((1, 128), lambda b, h, c: (0, 0)),
            pl.BlockSpec((4 * H_B, 1), lambda b, h, c: (0, 0)),
        ],
        out_specs=(out_spec(0), out_spec(1)),
        scratch_shapes=[pltpu.VMEM((2 * hps, DQK_B, DV_B), F32), pltpu.VMEM((2 * hps, 1, DQK_B), F32),
                        pltpu.VMEM((2 * hps, 1, 1), F32)],
        compiler_params=_cparams(("arbitrary",) * 3),
        name="mlstm_scan",
    )(*chunk_args, *chunk_args, gb_col, gb_row)


def _mlstm_finish_body(hf_ref, hb_ref, o_ref, g_ref, out_ref):
    hs = hf_ref[...] + hb_ref[...]
    g = g_ref[...]
    og = o_ref[...]
    for hd in range(H_B):
        sl = slice(hd * DV_B, (hd + 1) * DV_B)
        x = hs[:, sl]
        ms = jnp.mean(x * x, axis=-1, keepdims=True)
        hn = x * lax.rsqrt(ms + NORM_EPS) * g[:, sl]
        out_ref[:, sl] = (jax.nn.sigmoid(og[:, sl]) * hn).astype(out_ref.dtype)


def mlstm_finish(h_fwd, h_bwd, p_all, out_norm_g):
    n = h_fwd.shape[0]
    rb = ROW_BLOCK
    return pl.pallas_call(
        _mlstm_finish_body,
        out_shape=jax.ShapeDtypeStruct((n, W_B), BF16),
        grid=(n // rb,),
        in_specs=[
            pl.BlockSpec((rb, W_B), lambda i: (i, 0)),
            pl.BlockSpec((rb, W_B), lambda i: (i, 0)),
            pl.BlockSpec((rb, W_B), lambda i: (i, P_O // W_B)),
            pl.BlockSpec((1, W_B), lambda i: (0, 0)),
        ],
        out_specs=pl.BlockSpec((rb, W_B), lambda i: (i, 0)),
        compiler_params=_cparams(("parallel",)),
        name="mlstm_finish",
    )(h_fwd, h_bwd, p_all, out_norm_g.reshape(1, W_B))


NA_ROWS_PER_STEP = 4
NA_TILES_PER_STEP = 4


def _na_geometry(rows_n):
    r_t = NA_ROWS_PER_STEP
    kh = min(NA_KH, rows_n)
    win = min(r_t + kh - 1, rows_n)
    tiles = rows_n // r_t
    starts, sigs, cls = [], [], []
    for t in range(tiles):
        rq0 = t * r_t
        start = int(np.clip(rq0 - kh // 2, 0, rows_n - win))
        r0 = np.clip(np.arange(rq0, rq0 + r_t) - kh // 2, 0, rows_n - kh)
        sig = (start - rq0, tuple((r0 - rq0).tolist()))
        if sig not in sigs:
            sigs.append(sig)
        starts.append(start)
        cls.append(sigs.index(sig))
    return kh, win, np.array(starts, np.int32), np.array(cls, np.int32), sigs


def na_bias_rows(rpb):
    cols = np.arange(GRID_W)
    c0 = np.clip(cols - NA_KW // 2, 0, GRID_W - NA_KW)
    col_valid = (cols[None, :] >= c0[:, None]) & (cols[None, :] < c0[:, None] + NA_KW)
    col_off = cols[None, :] - cols[:, None] + (NA_KW - 1)
    c_sel = (col_valid[:, :, None] & (col_off[:, :, None] == np.arange(2 * NA_KW - 1))).astype(np.float32)
    t = jnp.einsum('hrs,cds->hrcd', rpb, c_sel, precision=lax.Precision.HIGHEST)
    t = t + np.where(col_valid, 0.0, NEG).astype(np.float32)[None, None]
    return jnp.concatenate([t, t], axis=-1)


def _na_bias_plan(rows_n):
    r_t = NA_ROWS_PER_STEP
    kh, win, _, _, sigs = _na_geometry(rows_n)
    plan = []
    for (dstart, dr0) in sigs:
        cls_plan = []
        for a in range(r_t):
            row = []
            for j in range(win):
                krow = dstart + j
                ok = dr0[a] <= krow < dr0[a] + kh
                row.append(krow - a + (NA_KH - 1) if ok else None)
            cls_plan.append(row)
        plan.append(cls_plan)
    return plan


def _na_build_bias(rows_ref, btab, plan):
    neg = jnp.full((GRID_W, 2 * GRID_W), NEG, F32)
    lane = lax.broadcasted_iota(jnp.int32, (GRID_W, 2 * GRID_W), 1)

    def piece(ro):
        return neg if ro is None else rows_ref[0, ro]

    for k, cls_plan in enumerate(plan):
        for a, row in enumerate(cls_plan):
            rs = slice(a * GRID_W, (a + 1) * GRID_W)
            for j in range(0, len(row) - 1, 2):
                both = jnp.where(lane < GRID_W, piece(row[j]), piece(row[j + 1]))
                btab[k, rs, j * GRID_W:(j + 2) * GRID_W] = both
            if len(row) % 2:
                j = len(row) - 1
                btab[k, rs, j * GRID_W:(j + 1) * GRID_W] = piece(row[j])[:, :GRID_W]


def _head_rmsnorm(x, g):
    xf = x.astype(F32)
    ms = jnp.mean(xf * xf, axis=-1, keepdims=True)
    return xf * lax.rsqrt(ms + NORM_EPS) * g


def _na_body(start_ref, cls_ref, q_ref, k_ref, v_ref, kc_ref, vc_ref, brow_ref, gq_ref, gk_ref, *rest,
             seq, ctx_len, win_rows, scale, bias_plan):
    o_ref, kn, knc, btab = rest[-4:]
    i = pl.program_id(2)
    chunk = min(512, seq)

    @pl.when(i == 0)
    def _():
        _na_build_bias(brow_ref, btab, bias_plan)
        gk = gk_ref[...]

        def body(c, carry):
            rows = pl.ds(pl.multiple_of(c * chunk, chunk), chunk)
            kn[rows, :] = _head_rmsnorm(k_ref[rows, :], gk).astype(BF16)
            return carry

        lax.fori_loop(0, seq // chunk, body, 0)
        knc[...] = _head_rmsnorm(kc_ref[...], gk).astype(BF16)

    nt = (((1,), (1,)), ((), ()))
    tq = NA_ROWS_PER_STEP * GRID_W
    for tl in range(NA_TILES_PER_STEP):
        rows = slice(tl * tq, (tl + 1) * tq)
        q = (_head_rmsnorm(q_ref[rows, :], gq_ref[...]) * scale).astype(BF16)
        krow0 = pl.multiple_of(start_ref[i * NA_TILES_PER_STEP + tl] * GRID_W, GRID_W)
        kw = kn[pl.ds(krow0, win_rows), :]
        vw = v_ref[pl.ds(krow0, win_rows), :]
        s = lax.dot_general(q, kw, nt, preferred_element_type=F32) + btab[cls_ref[i * NA_TILES_PER_STEP + tl]]
        sc = lax.dot_general(q, knc[...], nt, preferred_element_type=F32)
        m = jnp.maximum(jnp.max(s, axis=-1, keepdims=True), jnp.max(sc, axis=-1, keepdims=True))
        p = jnp.exp(s - m)
        pc = jnp.exp(sc - m)
        l = jnp.sum(p, axis=-1, keepdims=True) + jnp.sum(pc, axis=-1, keepdims=True)
        o = (jnp.dot(p.astype(BF16), vw, preferred_element_type=F32)
             + jnp.dot(pc.astype(BF16), vc_ref[...], preferred_element_type=F32))
        o_ref[rows, :] = (o / l).astype(o_ref.dtype)


def na_attention(qkv, bias_rows, gq, gk, *, batch, seq, ctx_len, out_buf=None):
    rows_n = seq // GRID_W
    r_t = NA_ROWS_PER_STEP
    _, win, starts, cls, _ = _na_geometry(rows_n)
    tps = NA_TILES_PER_STEP
    tq = tps * r_t * GRID_W
    tiles = rows_n // (r_t * tps)
    assert rows_n % (r_t * tps) == 0
    n_lat = batch * seq
    ctx_blk0 = n_lat // ctx_len
    plan = _na_bias_plan(rows_n)
    body = functools.partial(_na_body, seq=seq, ctx_len=ctx_len, win_rows=win * GRID_W, scale=DH_C ** -0.5,
                             bias_plan=plan)
    n_ro = 2 * NA_KH - 1
    in_specs = [
        pl.BlockSpec((tq, DH_C), lambda b, h, i, st, cl: (b * tiles + i, h)),
        pl.BlockSpec((seq, DH_C), lambda b, h, i, st, cl: (b, H_C + h)),
        pl.BlockSpec((seq, DH_C), lambda b, h, i, st, cl: (b, 2 * H_C + h)),
        pl.BlockSpec((ctx_len, DH_C), lambda b, h, i, st, cl: (ctx_blk0 + b, H_C + h)),
        pl.BlockSpec((ctx_len, DH_C), lambda b, h, i, st, cl: (ctx_blk0 + b, 2 * H_C + h)),
        pl.BlockSpec((1, n_ro, GRID_W, 2 * GRID_W), lambda b, h, i, st, cl: (h, 0, 0, 0)),
        pl.BlockSpec((1, DH_C), lambda b, h, i, st, cl: (0, 0)),
        pl.BlockSpec((1, DH_C), lambda b, h, i, st, cl: (0, 0)),
    ]
    args = [jnp.asarray(starts), jnp.asarray(cls), qkv, qkv, qkv, qkv, qkv, bias_rows, gq, gk]
    aliases = {}
    out_rows = n_lat
    if out_buf is not None:
        in_specs.append(pl.BlockSpec(memory_space=pl.ANY))
        aliases = {len(args): 0}
        args.append(out_buf)
        out_rows = out_buf.shape[0]
    grid_spec = pltpu.PrefetchScalarGridSpec(
        num_scalar_prefetch=2,
        grid=(batch, H_C, tiles),
        in_specs=in_specs,
        out_specs=pl.BlockSpec((tq, DH_C), lambda b, h, i, st, cl: (b * tiles + i, h)),
        scratch_shapes=[pltpu.VMEM((seq, DH_C), BF16), pltpu.VMEM((ctx_len, DH_C), BF16),
                        pltpu.VMEM((len(plan), r_t * GRID_W, win * GRID_W), F32)],
    )
    return pl.pallas_call(
        body,
        out_shape=jax.ShapeDtypeStruct((out_rows, H_C * DH_C), BF16),
        grid_spec=grid_spec,
        input_output_aliases=aliases,
        compiler_params=_cparams(("arbitrary", "arbitrary", "arbitrary")),
        name="na_attention",
    )(*args)


def _ctx_attn_body(q_ref, k_ref, v_ref, gq_ref, gk_ref, buf_ref, o_ref, *, scale):
    del buf_ref
    q = (_head_rmsnorm(q_ref[...], gq_ref[...]) * scale).astype(BF16)
    k = _head_rmsnorm(k_ref[...], gk_ref[...]).astype(BF16)
    s = lax.dot_general(q, k, (((1,), (1,)), ((), ())), preferred_element_type=F32)
    m = jnp.max(s, axis=-1, keepdims=True)
    p = jnp.exp(s - m)
    l = jnp.sum(p, axis=-1, keepdims=True)
    o = jnp.dot(p.astype(BF16), v_ref[...], preferred_element_type=F32)
    o_ref[...] = (o / l).astype(o_ref.dtype)


def na_ctx_attention(qkv, gq, gk, out_buf, *, batch, seq, ctx_len):
    ctx_blk0 = batch * seq // ctx_len
    return pl.pallas_call(
        functools.partial(_ctx_attn_body, scale=DH_C ** -0.5),
        out_shape=jax.ShapeDtypeStruct(out_buf.shape, out_buf.dtype),
        grid=(batch, H_C),
        in_specs=[
            pl.BlockSpec((ctx_len, DH_C), lambda b, h: (ctx_blk0 + b, h)),
            pl.BlockSpec((ctx_len, DH_C), lambda b, h: (ctx_blk0 + b, H_C + h)),
            pl.BlockSpec((ctx_len, DH_C), lambda b, h: (ctx_blk0 + b, 2 * H_C + h)),
            pl.BlockSpec((1, DH_C), lambda b, h: (0, 0)),
            pl.BlockSpec((1, DH_C), lambda b, h: (0, 0)),
            pl.BlockSpec(memory_space=pl.ANY),
        ],
        out_specs=pl.BlockSpec((ctx_len, DH_C), lambda b, h: (ctx_blk0 + b, h)),
        input_output_aliases={5: 0},
        compiler_params=_cparams(("parallel", "parallel")),
        name="na_ctx_attention",
    )(qkv, qkv, qkv, gq, gk, out_buf)


def _slab_copy(src_ref, src_tok, dst_ref, dst_tok, sem):
    def rows(tok):
        off = tok * SLAB_STRIDE
        return pl.ds(off if isinstance(off, int) else pl.multiple_of(off, 8), SLAB_ROWS)

    return pltpu.make_async_copy(src_ref.at[rows(src_tok), :], dst_ref.at[rows(dst_tok), :], sem)


def _load_slabs(ref, rows, dtype):
    return jnp.concatenate([ref[pl.ds(j, rows, stride=SLAB_STRIDE), :].astype(dtype) for j in range(SLAB_ROWS)],
                           axis=1)


def moe_slot_plan(route, counts, n_tiles):
    cnt = counts[0, :N_EXPERTS].astype(jnp.int32)
    padded = (cnt + MOE_TILE - 1) // MOE_TILE * MOE_TILE
    ends = jnp.cumsum(padded)
    base = ends - padded
    n_used = ends[-1] // MOE_TILE
    tile_start = jnp.arange(n_tiles, dtype=jnp.int32) * MOE_TILE
    last_start = jnp.maximum(ends[-1] - MOE_TILE, 0)
    tile_expert = jnp.searchsorted(ends, jnp.minimum(tile_start, last_start), side='right').astype(jnp.int32)
    tile_expert = jnp.minimum(tile_expert, N_EXPERTS - 1)
    ids = route[:, 0:2].astype(jnp.int32)
    pos = jnp.take(base, ids, axis=0) + route[:, 4:6].astype(jnp.int32)
    return pos[:, 0], pos[:, 1], tile_expert, n_used.reshape(1).astype(jnp.int32)


def _slot_source_body(p1_ref, p2_ref, src_ref, *, n_tok):
    def clear(i, c):
        src_ref[i] = 0
        return c

    lax.fori_loop(0, src_ref.shape[0], clear, 0, unroll=8)

    def fill(n, c):
        src_ref[p1_ref[n]] = n
        src_ref[p2_ref[n]] = n
        return c

    lax.fori_loop(0, n_tok, fill, 0, unroll=4)


def moe_slot_source(pos1, pos2, n_slots):
    smem = pl.BlockSpec(memory_space=pltpu.SMEM)
    return pl.pallas_call(
        functools.partial(_slot_source_body, n_tok=pos1.shape[0]),
        out_shape=jax.ShapeDtypeStruct((n_slots,), jnp.int32),
        in_specs=[smem, smem],
        out_specs=smem,
        name="moe_slot_source",
    )(pos1, pos2)


def _gather_slabs(src_ref, tok_ref, tok0, dst_ref, sem, n):
    def issue(r, c):
        _slab_copy(src_ref, tok_ref[tok0 + r], dst_ref, r, sem).start()
        return c

    lax.fori_loop(0, n, issue, 0, unroll=8)


def _wait_slabs(src_ref, dst_ref, sem, n):
    rows = pl.ds(0, n * SLAB_ROWS)
    pltpu.make_async_copy(src_ref.at[rows, :], dst_ref.at[rows, :], sem).wait()


def _moe_expert_body(te_ref, nu_ref, src_ref, hs_ref, wg_ref, wu_ref, wd_ref, o_ref, xbuf, wg_bf, wu_bf, wd_bf,
                     sems):
    t = pl.program_id(0)
    tm = MOE_TILE
    slot = t % 2

    @pl.when(t < nu_ref[0])
    def _():
        @pl.when(t == 0)
        def _():
            _gather_slabs(hs_ref, src_ref, 0, xbuf.at[0], sems.at[0], tm)

        @pl.when(t + 1 < nu_ref[0])
        def _():
            _gather_slabs(hs_ref, src_ref, (t + 1) * tm, xbuf.at[1 - slot], sems.at[1 - slot], tm)

        @pl.when((t == 0) | (te_ref[t] != te_ref[jnp.maximum(t - 1, 0)]))
        def _():
            wg_bf[...] = wg_ref[...].astype(BF16)
            wu_bf[...] = wu_ref[...].astype(BF16)
            wd_bf[...] = wd_ref[...].astype(BF16)

        _wait_slabs(hs_ref, xbuf.at[slot], sems.at[slot], tm)
        x = _load_slabs(xbuf.at[slot], tm, BF16)
        nt = (((1,), (1,)), ((), ()))
        a = lax.dot_general(x, wg_bf[...], nt, preferred_element_type=F32)
        u = lax.dot_general(x, wu_bf[...], nt, preferred_element_type=F32)
        y = (a * jax.nn.sigmoid(a) * u).astype(BF16)
        _store_slabs(o_ref, jnp.dot(y, wd_bf[...], preferred_element_type=F32), tm)

    @pl.when(t >= nu_ref[0])
    def _():
        o_ref[...] = jnp.zeros_like(o_ref)


def moe_experts(hs, src, tile_expert, n_used, w_gate_t, w_up_t, w_down, layer):
    n_tiles = tile_expert.shape[0]
    f, d = w_gate_t.shape[-2:]
    blk = MOE_TILE * SLAB_STRIDE
    wspec = pl.BlockSpec((None, None, f, d), lambda t, te, nu, src: (layer, te[t], 0, 0))
    grid_spec = pltpu.PrefetchScalarGridSpec(
        num_scalar_prefetch=3,
        grid=(n_tiles,),
        in_specs=[pl.BlockSpec(memory_space=pl.ANY), wspec, wspec, wspec],
        out_specs=pl.BlockSpec((blk, 128), lambda t, te, nu, src: (t, 0)),
        scratch_shapes=[pltpu.VMEM((2, blk, 128), F32), pltpu.VMEM((f, d), BF16), pltpu.VMEM((f, d), BF16),
                        pltpu.VMEM((f, d), BF16), pltpu.SemaphoreType.DMA((2,))],
    )
    return pl.pallas_call(
        _moe_expert_body,
        out_shape=jax.ShapeDtypeStruct((n_tiles * blk, 128), F32),
        grid_spec=grid_spec,
        compiler_params=_dma_cparams(("arbitrary",)),
        name="moe_experts",
    )(tile_expert, n_used, src, hs, w_gate_t, w_up_t, w_down)


def _moe_combine_body(p1_ref, p2_ref, ys_ref, route_ref, res_ref, gate_ref, o_ref, abuf, bbuf, sems):
    i = pl.program_id(0)
    rb = ROW_BLOCK
    slot = i % 2

    def gather(blk, s):
        _gather_slabs(ys_ref, p1_ref, blk * rb, abuf.at[s], sems.at[0, s], rb)
        _gather_slabs(ys_ref, p2_ref, blk * rb, bbuf.at[s], sems.at[1, s], rb)

    @pl.when(i == 0)
    def _():
        gather(0, 0)

    @pl.when(i + 1 < pl.num_programs(0))
    def _():
        gather(i + 1, 1 - slot)

    route = route_ref[...]
    lane = lax.broadcasted_iota(jnp.int32, route.shape, 1)
    w1 = jnp.sum(jnp.where(lane == 2, route, 0.0), axis=1, keepdims=True)
    w2 = jnp.sum(jnp.where(lane == 3, route, 0.0), axis=1, keepdims=True)
    _wait_slabs(ys_ref, abuf.at[slot], sems.at[0, slot], rb)
    _wait_slabs(ys_ref, bbuf.at[slot], sems.at[1, slot], rb)
    for j in range(SLAB_ROWS):
        sl = slice(j * 128, (j + 1) * 128)
        rows = pl.ds(j, rb, stride=SLAB_STRIDE)
        mix = w1 * abuf[slot, rows, :] + w2 * bbuf[slot, rows, :]
        o_ref[:, sl] = res_ref[:, sl] + gate_ref[0][:, sl] * mix


def moe_combine(ys, pos1, pos2, route, resid, mods, k_gate, dims):
    n, d = route.shape[0], resid.shape[1]
    n_lat, seq, batch = dims
    rb = ROW_BLOCK
    mrow = functools.partial(_mod_row, rows_per_block=rb, n_lat=n_lat, seq=seq, batch=batch)
    grid_spec = pltpu.PrefetchScalarGridSpec(
        num_scalar_prefetch=2,
        grid=(n // rb,),
        in_specs=[
            pl.BlockSpec(memory_space=pl.ANY),
            pl.BlockSpec((rb, 128), lambda i, p1, p2: (i, 0)),
            pl.BlockSpec((rb, d), lambda i, p1, p2: (i, 0)),
            pl.BlockSpec((1, 1, d), lambda i, p1, p2: (mrow(i) * 6 + k_gate, 0, 0)),
        ],
        out_specs=pl.BlockSpec((rb, d), lambda i, p1, p2: (i, 0)),
        scratch_shapes=[pltpu.VMEM((2, rb * SLAB_STRIDE, 128), F32), pltpu.VMEM((2, rb * SLAB_STRIDE, 128), F32),
                        pltpu.SemaphoreType.DMA((2, 2))],
    )
    return pl.pallas_call(
        _moe_combine_body,
        out_shape=jax.ShapeDtypeStruct((n, d), F32),
        grid_spec=grid_spec,
        compiler_params=_dma_cparams(("arbitrary",)),
        name="moe_combine",
    )(pos1, pos2, ys, route, resid, mods)


def _swap16(a):
    lead = a.shape[:-1]
    return jnp.flip(a.reshape(*lead, D_ROPE // 32, 2, 16), axis=-2).reshape(*lead, D_ROPE)


W_IN_TILE = 512
_C_KR = Q_LORA + KV_LORA
_C_QB = _C_KR + D_ROPE
_C_G = _C_QB + 2 * H_B * DQK_B + 2 * W_B


def _w_in_tail_t(wt_all, j):
    wt = wt_all[j]
    kr = wt[_C_KR:_C_QB]
    kr_sw = jnp.flip(kr.reshape(D_ROPE // 32, 2, 16, -1), axis=1).reshape(D_ROPE, -1)
    gates = wt[_C_G:]
    pad = W_IN_TILE - 2 * D_ROPE - gates.shape[0]
    return jnp.concatenate([kr, kr_sw, jnp.pad(gates, ((0, pad), (0, 0)))], axis=0)


def _win_body(h_ref, wt_ref, tail_ref, o_ref, wbf_ref, *, n_main):
    j = pl.program_id(0)

    @pl.when((pl.program_id(1) == 0) & (j < n_main))
    def _():
        wbf_ref[...] = wt_ref[0].astype(BF16)

    @pl.when((pl.program_id(1) == 0) & (j >= n_main))
    def _():
        wbf_ref[...] = tail_ref[...].astype(BF16)

    o_ref[...] = lax.dot_general(h_ref[...], wbf_ref[...], (((1,), (1,)), ((), ())),
                                 preferred_element_type=F32).astype(o_ref.dtype)


def input_projection(h, wt_all, layer):
    m, d = h.shape
    tm = _row_tile(m)
    tn = W_IN_TILE
    n_a = (_C_G - _C_QB) // tn
    n_b = _C_KR // tn
    n_main = n_a + n_b
    assert (_C_G - _C_QB) % tn == 0 and _C_KR % tn == 0 and (n_main + 1) * tn == P_WIDTH and _C_QB % 8 == 0

    def w_row(j):
        row = jnp.where(j < n_a, _C_QB + j * tn, jnp.where(j < n_main, (j - n_a) * tn, 0))
        return pl.multiple_of(row, 8)

    return pl.pallas_call(
        functools.partial(_win_body, n_main=n_main),
        out_shape=jax.ShapeDtypeStruct((m, P_WIDTH), F32),
        grid=(n_main + 1, m // tm),
        in_specs=[
            pl.BlockSpec((tm, d), lambda j, i: (i, 0)),
            pl.BlockSpec((pl.Element(1), pl.Element(tn), pl.Element(d)), lambda j, i: (layer, w_row(j), 0)),
            pl.BlockSpec((tn, d), lambda j, i: (0, 0), pipeline_mode=pl.Buffered(1)),
        ],
        out_specs=pl.BlockSpec((tm, tn), lambda j, i: (i, j)),
        scratch_shapes=[pltpu.VMEM((tn, d), BF16)],
        compiler_params=_cparams(("arbitrary", "arbitrary")),
        name="input_projection",
    )(h, wt_all, _w_in_tail_t(wt_all, layer))


def _ext_w_q_up(w):
    w3 = w.reshape(Q_LORA, H_A, D_NOPE + D_ROPE)
    return jnp.concatenate([w3, _swap16(w3[:, :, D_NOPE:])], axis=2).reshape(Q_LORA, H_A * 256)


def _ext_gain(g):
    return jnp.concatenate([g, _swap16(g[D_NOPE:])]).reshape(1, 256)


def _rope_tables(seq):
    t = jnp.arange(seq)
    half = D_ROPE // 4
    inv = ROPE_THETA ** (-jnp.arange(half, dtype=F32) / half)
    ar = (t // GRID_W).astype(F32)[:, None] * inv[None, :]
    ac = (t % GRID_W).astype(F32)[:, None] * inv[None, :]
    cr, sr, cc, sc = jnp.cos(ar), jnp.sin(ar), jnp.cos(ac), jnp.sin(ac)
    return jnp.concatenate([cr, cr, cc, cc, -sr, sr, -sc, sc], axis=1)


def kernel(x, c, ctx, c_ctx, ada_w, ada_b, norm1_g, norm2_g, ab_w_in, mla_q_norm_g, mla_kv_norm_g, mla_w_q_up, mla_w_kv_up, mla_qn_g, mla_kn_g, mlstm_conv_w, mlstm_gate_b, mlstm_out_norm_g, ab_w_out, na_w_qkv, na_qn_g, na_kn_g, na_rpb, na_w_out, moe_w_rg, moe_b_rg, moe_w_re, moe_b_re, moe_w_gate, moe_w_up, moe_w_down):
    batch, seq, d = x.shape
    ctx_len = ctx.shape[1]
    depth = ada_w.shape[0]
    n_lat = batch * seq
    dims = (n_lat, seq, batch)
    assert d == D_MODEL and batch + 1 <= MOD_ROWS and seq % 512 == 0 and ctx_len == ROW_BLOCK

    xs = jnp.concatenate([x.reshape(n_lat, d), ctx.reshape(batch * ctx_len, d)], axis=0)
    cvec = jnp.concatenate([c, c_ctx[None, :]], axis=0)
    mods_all = ada_mods(cvec, ada_w, ada_b).reshape(depth, MOD_ROWS * 6, 1, d)
    trig = _rope_tables(seq)

    w_gate_t, w_up_t = jnp.swapaxes(moe_w_gate, 2, 3), jnp.swapaxes(moe_w_up, 2, 3)
    w_in_t = jnp.swapaxes(ab_w_in, 1, 2)
    kw = dict(batch=batch, seq=seq, ctx_len=ctx_len)
    for layer in range(depth):
        j = layer // 2
        last = layer == depth - 1
        mods = mods_all[layer]
        h = norm_mod(xs, norm1_g[layer], mods, (0, 1), dims)
        if layer % 2 == 0:
            p_all = input_projection(h, w_in_t, j)
            tm_big = _row_tile(p_all.shape[0])
            q_all = matmul(p_all, _ext_w_q_up(mla_w_q_up[j]), out_dtype=BF16, a_col=P_CQ // Q_LORA, k=Q_LORA,
                           rms_gain=mla_q_norm_g[j], tm=tm_big)
            kv_all = matmul(p_all, mla_w_kv_up, layer=j, out_dtype=BF16, a_col=P_CKV // KV_LORA, k=KV_LORA,
                            rms_gain=mla_kv_norm_g[j], tm=tm_big)
            gq, gk = _ext_gain(mla_qn_g[j]), _ext_gain(mla_kn_g[j])
            mix_a = jnp.zeros((xs.shape[0], W_A), BF16)
            mix_a = mla_attention(q_all, kv_all, p_all, trig, gq, gk, latent_queries=True, out_buf=mix_a, **kw)
            mix_a = mla_attention(q_all, kv_all, p_all, trig, gq, gk, latent_queries=False, out_buf=mix_a, **kw)
            qk = conv_silu(p_all, mlstm_conv_w[j], n_lat=n_lat, seq=seq, ctx_len=ctx_len)
            gates_t = p_all[:, P_GATE:P_GATE + 4 * H_B].T
            h_fwd, h_bwd = mlstm_scan(qk, p_all, gates_t, mlstm_gate_b[j], **kw)
            mix_b = mlstm_finish(h_fwd, h_bwd, p_all, mlstm_out_norm_g[j])
            xs = matmul(mix_a, ab_w_out, layer=j, a2=mix_b, out_dtype=F32, resid=xs, mods=mods, k_gate=2,
                        dims=dims)
        else:
            qkv = matmul(h, na_w_qkv, layer=j, out_dtype=BF16, tm=_row_tile(h.shape[0]))
            gq, gk = na_qn_g[j].reshape(1, DH_C), na_kn_g[j].reshape(1, DH_C)
            bias = na_bias_rows(na_rpb[j])
            if last:
                mixed = na_attention(qkv, bias, gq, gk, **kw)
            else:
                mixed = jnp.zeros((xs.shape[0], H_C * DH_C), BF16)
                mixed = na_attention(qkv, bias, gq, gk, out_buf=mixed, **kw)
                mixed = na_ctx_attention(qkv, gq, gk, mixed, **kw)
            xs = matmul(mixed, na_w_out, layer=j, out_dtype=F32, resid=xs, mods=mods, k_gate=2, dims=dims)

        w_r = jnp.pad(jnp.concatenate([moe_w_re[layer], moe_w_rg[layer]], axis=1),
                      ((0, 0), (0, 128 - N_EXPERTS - N_GROUPS)))
        b_r = jnp.pad(jnp.concatenate([moe_b_re[layer], moe_b_rg[layer]]), (0, 128 - N_EXPERTS - N_GROUPS))
        hs, route, counts = norm_router(xs, norm2_g[layer], mods, (3, 4), dims, w_r, b_r.reshape(1, 128))
        n_tiles = 2 * xs.shape[0] // MOE_TILE + N_EXPERTS
        pos1, pos2, tile_expert, n_used = moe_slot_plan(route, counts, n_tiles)
        src = moe_slot_source(pos1, pos2, n_tiles * MOE_TILE)
        ys = moe_experts(hs, src, tile_expert, n_used, w_gate_t, w_up_t, moe_w_down, layer)
        xs = moe_combine(ys, pos1, pos2, route, xs, mods, 5, dims)
    return xs[:n_lat].reshape(batch, seq, d) if xs.shape[0] != n_lat else xs.reshape(batch, seq, d)
```

```python
import functools

import numpy as np
import jax
import jax.numpy as jnp
from jax import lax
from jax.experimental import pallas as pl
from jax.experimental.pallas import tpu as pltpu

F32 = jnp.float32
BF16 = jnp.bfloat16

D_MODEL = 4096
GRID_W = 64
H_A = 16
Q_LORA = 1024
KV_LORA = 512
D_NOPE = 128
D_ROPE = 64
D_V_A = 128
H_B = 4
DQK_B = 256
DV_B = 512
MLSTM_CHUNK = 128
M_INIT = -1e30
H_C = 32
DH_C = 128
NA_KH = 8
NA_KW = 16
N_GROUPS = 4
EXPERTS_PER_GROUP = 8
N_EXPERTS = 32
D_FF_EXPERT = 192
ROPE_THETA = 10000.0
NORM_EPS = 1e-6
W_A = H_A * D_V_A
W_B = H_B * DV_B
NEG = -1e30

VMEM_LIMIT_BYTES = 56 * 1024 * 1024
ROW_BLOCK = 256
MOD_ROWS = 8
SLAB_ROWS = 32
SLAB_STRIDE = 40
MOE_TILE = 256

P_QK = 0
P_V = 2048
P_O = 4096
P_CQ = 6144
P_CKV = 7168
P_KR = 7680
P_GATE = 7808
P_WIDTH = 8192


def _cparams(sem):
    return pltpu.CompilerParams(dimension_semantics=sem, vmem_limit_bytes=VMEM_LIMIT_BYTES)


MAX_ROW_TILE = 1100


def _row_tile(m, limit=MAX_ROW_TILE):
    return max(t for t in range(16, limit + 1, 16) if m % t == 0)


def _dma_cparams(sem):
    return pltpu.CompilerParams(dimension_semantics=sem, vmem_limit_bytes=VMEM_LIMIT_BYTES,
                                disable_bounds_checks=True)


def _mod_row(i, rows_per_block, n_lat, seq, batch):
    return jnp.where(i < n_lat // rows_per_block, i // (seq // rows_per_block), batch)


ADA_ROW_CHUNK = 512


def _ada_body(c_ref, w_ref, b_ref, o_ref, sc_ref, *, n_vec):
    d, tn = w_ref.shape[1], w_ref.shape[2]
    reps = tn // 128

    @pl.when((pl.program_id(0) == 0) & (pl.program_id(1) == 0))
    def _():
        c = c_ref[...]
        sc_ref[...] = c * jax.nn.sigmoid(c)

    def chunk(ci, accs):
        rows = pl.ds(pl.multiple_of(ci * ADA_ROW_CHUNK, ADA_ROW_CHUNK), ADA_ROW_CHUNK)
        w = w_ref[0, rows, :]
        out = []
        for m in range(n_vec):
            cm = jnp.concatenate([sc_ref[m, rows, :]] * reps, axis=1)
            part = (w * cm).reshape(ADA_ROW_CHUNK // 8, 8, tn).sum(axis=0)
            out.append(accs[m] + part)
        return tuple(out)

    accs = lax.fori_loop(0, d // ADA_ROW_CHUNK, chunk, tuple(jnp.zeros((8, tn), F32) for _ in range(n_vec)))
    rows = [jnp.sum(a, axis=0, keepdims=True) for a in accs]
    rows += [jnp.zeros((1, tn), F32)] * (MOD_ROWS - n_vec)
    o_ref[0] = jnp.concatenate(rows, axis=0) + b_ref[0]


def ada_mods(cvec, ada_w, ada_b, tn=512):
    depth, d, n6 = ada_w.shape
    n_vec = cvec.shape[0]
    c_lanes = jnp.broadcast_to(cvec[:, :, None], (n_vec, d, 128))
    return pl.pallas_call(
        functools.partial(_ada_body, n_vec=n_vec),
        out_shape=jax.ShapeDtypeStruct((depth, MOD_ROWS, n6), F32),
        grid=(depth, n6 // tn),
        in_specs=[
            pl.BlockSpec((n_vec, d, 128), lambda l, j: (0, 0, 0)),
            pl.BlockSpec((1, d, tn), lambda l, j: (l, 0, j)),
            pl.BlockSpec((1, 1, tn), lambda l, j: (l, 0, j)),
        ],
        out_specs=pl.BlockSpec((1, MOD_ROWS, tn), lambda l, j: (l, 0, j)),
        scratch_shapes=[pltpu.VMEM((n_vec, d, 128), F32)],
        compiler_params=_cparams(("arbitrary", "arbitrary")),
        name="ada_mods",
    )(c_lanes, ada_w, ada_b.reshape(depth, 1, n6))


def _modulated_norm(x_ref, g_ref, sh_ref, sc_ref):
    x = x_ref[...]
    ms = jnp.mean(x * x, axis=-1, keepdims=True)
    y = x * lax.rsqrt(ms + NORM_EPS) * g_ref[...]
    return y * (1.0 + sc_ref[0]) + sh_ref[0]


def _norm_body(x_ref, g_ref, sh_ref, sc_ref, o_ref):
    o_ref[...] = _modulated_norm(x_ref, g_ref, sh_ref, sc_ref).astype(o_ref.dtype)


def _store_slabs(ref, mat, rows):
    for j in range(SLAB_ROWS):
        ref[pl.ds(j, rows, stride=SLAB_STRIDE), :] = mat[:, j * 128:(j + 1) * 128]
    for j in range(SLAB_ROWS, SLAB_STRIDE):
        ref[pl.ds(j, rows, stride=SLAB_STRIDE), :] = jnp.zeros((rows, 128), mat.dtype)


def _norm_router_body(x_ref, g_ref, sh_ref, sc_ref, wr_ref, br_ref, hs_ref, route_ref, cnt_ref, carry):
    @pl.when(pl.program_id(0) == 0)
    def _():
        carry[...] = jnp.zeros_like(carry)

    h = _modulated_norm(x_ref, g_ref, sh_ref, sc_ref)
    _store_slabs(hs_ref, h, ROW_BLOCK)
    logits = jnp.dot(h, wr_ref[...], precision=lax.Precision.HIGHEST,
                     preferred_element_type=F32) + br_ref[...]
    lane = lax.broadcasted_iota(jnp.int32, logits.shape, 1)
    big = jnp.int32(1 << 20)
    is_g = (lane >= N_EXPERTS) & (lane < N_EXPERTS + N_GROUPS)
    gl = jnp.where(is_g, logits, NEG)
    gmax = jnp.max(gl, axis=-1, keepdims=True)
    g_idx = jnp.min(jnp.where(gl == gmax, lane, big), axis=-1, keepdims=True) - N_EXPERTS
    p_g = 1.0 / jnp.sum(jnp.where(is_g, jnp.exp(gl - gmax), 0.0), axis=-1, keepdims=True)
    lo = g_idx * EXPERTS_PER_GROUP
    in_grp = (lane >= lo) & (lane < lo + EXPERTS_PER_GROUP)
    el = jnp.where(in_grp, logits, NEG)
    e1 = jnp.max(el, axis=-1, keepdims=True)
    i1 = jnp.min(jnp.where(el == e1, lane, big), axis=-1, keepdims=True)
    el2 = jnp.where(lane == i1, NEG, el)
    e2 = jnp.max(el2, axis=-1, keepdims=True)
    i2 = jnp.min(jnp.where(el2 == e2, lane, big), axis=-1, keepdims=True)
    t = jnp.exp(e2 - e1)
    w1 = p_g / (1.0 + t)
    w2 = w1 * t
    onehot = jnp.where((lane == i1) | (lane == i2), 1.0, 0.0)
    t_i = lax.broadcasted_iota(jnp.int32, (ROW_BLOCK, ROW_BLOCK), 0)
    s_i = lax.broadcasted_iota(jnp.int32, (ROW_BLOCK, ROW_BLOCK), 1)
    before = jnp.where(s_i < t_i, 1.0, 0.0).astype(BF16)
    seen = jnp.dot(before, onehot.astype(BF16), preferred_element_type=F32) + carry[...]
    rank1 = jnp.sum(jnp.where(lane == i1, seen, 0.0), axis=-1, keepdims=True)
    rank2 = jnp.sum(jnp.where(lane == i2, seen, 0.0), axis=-1, keepdims=True)
    total = carry[...] + jnp.sum(onehot, axis=0, keepdims=True)
    carry[...] = total
    cnt_ref[...] = jnp.broadcast_to(total, cnt_ref.shape)
    cols = (i1.astype(F32), i2.astype(F32), w1, w2, rank1, rank2)
    route = jnp.zeros(logits.shape, F32)
    for k, v in enumerate(cols):
        route = jnp.where(lane == k, v, route)
    route_ref[...] = route


def _norm_specs(d, mods_k, dims):
    n_lat, seq, batch = dims
    mrow = functools.partial(_mod_row, rows_per_block=ROW_BLOCK, n_lat=n_lat, seq=seq, batch=batch)
    k_shift, k_scale = mods_k
    return [
        pl.BlockSpec((ROW_BLOCK, d), lambda i: (i, 0)),
        pl.BlockSpec((1, d), lambda i: (0, 0)),
        pl.BlockSpec((1, 1, d), lambda i: (mrow(i) * 6 + k_shift, 0, 0)),
        pl.BlockSpec((1, 1, d), lambda i: (mrow(i) * 6 + k_scale, 0, 0)),
    ]


def norm_mod(x, g, mods, mods_k, dims):
    n, d = x.shape
    return pl.pallas_call(
        _norm_body,
        out_shape=jax.ShapeDtypeStruct((n, d), BF16),
        grid=(n // ROW_BLOCK,),
        in_specs=_norm_specs(d, mods_k, dims),
        out_specs=pl.BlockSpec((ROW_BLOCK, d), lambda i: (i, 0)),
        compiler_params=_cparams(("parallel",)),
        name="norm_mod",
    )(x, g.reshape(1, d), mods, mods)


def norm_router(x, g, mods, mods_k, dims, w_r, b_r):
    n, d = x.shape
    assert d == SLAB_ROWS * 128
    return pl.pallas_call(
        _norm_router_body,
        out_shape=(jax.ShapeDtypeStruct((n * SLAB_STRIDE, 128), F32), jax.ShapeDtypeStruct((n, 128), F32),
                   jax.ShapeDtypeStruct((8, 128), F32)),
        grid=(n // ROW_BLOCK,),
        in_specs=_norm_specs(d, mods_k, dims) + [
            pl.BlockSpec((d, 128), lambda i: (0, 0)),
            pl.BlockSpec((1, 128), lambda i: (0, 0)),
        ],
        out_specs=(pl.BlockSpec((ROW_BLOCK * SLAB_STRIDE, 128), lambda i: (i, 0)),
                   pl.BlockSpec((ROW_BLOCK, 128), lambda i: (i, 0)),
                   pl.BlockSpec((8, 128), lambda i: (0, 0))),
        scratch_shapes=[pltpu.VMEM((1, 128), F32)],
        compiler_params=_cparams(("arbitrary",)),
        name="norm_router",
    )(x, g.reshape(1, d), mods, mods, w_r, b_r)


def _mm_body(*refs, rms, gated, k1, two_a):
    it = iter(refs)
    a_ref = next(it)
    a2_ref = next(it) if two_a else None
    g_ref = next(it) if rms else None
    w_ref = next(it)
    res_ref = next(it) if gated else None
    gate_ref = next(it) if gated else None
    o_ref = next(it)
    wbf_ref = next(it)

    @pl.when(pl.program_id(1) == 0)
    def _():
        wbf_ref[...] = w_ref[...].astype(BF16)

    a = a_ref[...]
    if rms:
        af = a.astype(F32)
        ms = jnp.mean(af * af, axis=-1, keepdims=True)
        a = (af * lax.rsqrt(ms + NORM_EPS) * g_ref[...]).astype(BF16)
    if two_a:
        acc = (jnp.dot(a, wbf_ref[:k1, :], preferred_element_type=F32)
               + jnp.dot(a2_ref[...], wbf_ref[k1:, :], preferred_element_type=F32))
    else:
        acc = jnp.dot(a, wbf_ref[...], preferred_element_type=F32)
    if gated:
        acc = res_ref[...] + gate_ref[0] * acc
    o_ref[...] = acc.astype(o_ref.dtype)


def matmul(a, w, *, out_dtype, layer=None, a_col=0, k=None, a2=None, rms_gain=None, resid=None, mods=None,
           k_gate=None, dims=None, tm=512, tn=512):
    m = a.shape[0]
    kw_, nw = w.shape[-2], w.shape[-1]
    two_a = a2 is not None
    k = (a.shape[1] if two_a else kw_) if k is None else k
    assert kw_ == k + (a2.shape[1] if two_a else 0) and m % tm == 0 and nw % tn == 0
    rms = rms_gain is not None
    gated = resid is not None
    in_specs = [pl.BlockSpec((tm, k), lambda j, i: (i, a_col))]
    args = [a]
    if two_a:
        in_specs.append(pl.BlockSpec((tm, a2.shape[1]), lambda j, i: (i, 0)))
        args.append(a2)
    if rms:
        in_specs.append(pl.BlockSpec((1, k), lambda j, i: (0, 0)))
        args.append(rms_gain.reshape(1, k))
    if layer is None:
        in_specs.append(pl.BlockSpec((kw_, tn), lambda j, i: (0, j)))
    else:
        in_specs.append(pl.BlockSpec((None, kw_, tn), lambda j, i: (layer, 0, j)))
    args.append(w)
    if gated:
        n_lat, seq, batch = dims
        mrow = functools.partial(_mod_row, rows_per_block=tm, n_lat=n_lat, seq=seq, batch=batch)
        in_specs.append(pl.BlockSpec((tm, tn), lambda j, i: (i, j)))
        in_specs.append(pl.BlockSpec((1, 1, tn), lambda j, i: (mrow(i) * 6 + k_gate, 0, j)))
        args += [resid, mods]
    return pl.pallas_call(
        functools.partial(_mm_body, rms=rms, gated=gated, k1=k, two_a=two_a),
        out_shape=jax.ShapeDtypeStruct((m, nw), out_dtype),
        grid=(nw // tn, m // tm),
        in_specs=in_specs,
        out_specs=pl.BlockSpec((tm, tn), lambda j, i: (i, j)),
        scratch_shapes=[pltpu.VMEM((kw_, tn), BF16)],
        compiler_params=_cparams(("arbitrary", "arbitrary")),
        name="matmul",
    )(*args)


def _rope_rot(pr, trig):
    if trig is not None:
        pr = pr * trig
    else:
        lane = lax.broadcasted_iota(jnp.int32, pr.shape, 1)
        pr = jnp.where(lane < D_ROPE, pr, 0.0)
    return pr + pltpu.roll(pr, D_ROPE, axis=1)


def _mla_body(*refs, n_lat_keys, n_ctx_keys, rope_q, scale, key_chunk, aliased):
    it = iter(refs)
    q_ref = next(it)
    kvl_ref = next(it) if n_lat_keys else None
    krl_ref = next(it) if n_lat_keys else None
    tkl_ref = next(it) if n_lat_keys else None
    kvc_ref = next(it)
    krc_ref = next(it)
    tq_ref = next(it) if rope_q else None
    gq_ref = next(it)
    gk_ref = next(it)
    if aliased:
        next(it)
    o_ref = next(it)
    kbuf = next(it)
    vbuf = next(it)
    dqk = D_NOPE + D_ROPE

    def prep_keys(kv_ref, kr_ref, trig_ref, off, n):
        gk = gk_ref[...]

        def chunk(c, carry):
            rows = pl.ds(pl.multiple_of(c * key_chunk, key_chunk), key_chunk)
            kv = kv_ref[rows, :]
            kn = kv[:, :D_NOPE].astype(F32)
            kr = kr_ref[rows, :].astype(F32)
            lane = lax.broadcasted_iota(jnp.int32, kr.shape, 1)
            ss = (jnp.sum(kn * kn, axis=-1, keepdims=True)
                  + jnp.sum(jnp.where(lane < D_ROPE, kr * kr, 0.0), axis=-1, keepdims=True))
            r = lax.rsqrt(ss / dqk + NORM_EPS)
            trig = trig_ref[rows, :] if trig_ref is not None else None
            rot = _rope_rot(kr * (r * gk[:, D_NOPE:]), trig)
            orow = pl.ds(pl.multiple_of(off + c * key_chunk, key_chunk), key_chunk)
            kbuf[orow, :D_NOPE] = (kn * (r * gk[:, :D_NOPE])).astype(BF16)
            kbuf[orow, D_NOPE:] = rot.astype(BF16)
            vbuf[orow, :] = kv[:, D_NOPE:].astype(BF16)
            return carry

        lax.fori_loop(0, n // key_chunk, chunk, 0)

    @pl.when(pl.program_id(2) == 0)
    def _():
        if n_lat_keys:
            prep_keys(kvl_ref, krl_ref, tkl_ref, 0, n_lat_keys)
        prep_keys(kvc_ref, krc_ref, None, n_lat_keys, n_ctx_keys)

    gq = gq_ref[...]
    for sb in range(q_ref.shape[0] // MLA_SUB_Q):
        rows = slice(sb * MLA_SUB_Q, (sb + 1) * MLA_SUB_Q)
        q = q_ref[rows, :].astype(F32)
        qn = q[:, :D_NOPE]
        qr = q[:, D_NOPE:]
        lane = lax.broadcasted_iota(jnp.int32, qr.shape, 1)
        ss = (jnp.sum(qn * qn, axis=-1, keepdims=True)
              + jnp.sum(jnp.where(lane < D_ROPE, qr * qr, 0.0), axis=-1, keepdims=True))
        r = lax.rsqrt(ss / dqk + NORM_EPS) * scale
        rot = _rope_rot(qr * (r * gq[:, D_NOPE:]), tq_ref[rows, :] if rope_q else None)
        rot = jnp.where(lane < D_ROPE, rot, 0.0)
        q2 = jnp.concatenate([qn * (r * gq[:, :D_NOPE]), rot], axis=1).astype(BF16)
        s = lax.dot_general(q2, kbuf[...], (((1,), (1,)), ((), ())), preferred_element_type=F32)
        m = jnp.max(s, axis=-1, keepdims=True)
        p = jnp.exp2(s - m)
        l = jnp.sum(p, axis=-1, keepdims=True)
        o = jnp.dot(p.astype(BF16), vbuf[...], preferred_element_type=F32)
        o_ref[rows, :] = (o / l).astype(o_ref.dtype)


MLA_SUB_Q = 256
MLA_SUBS_PER_STEP = 8


def mla_attention(q_all, kv_all, p_all, trig, gq, gk, *, batch, seq, ctx_len, latent_queries, out_buf=None):
    n_lat = batch * seq
    kr_blk = P_KR // 128
    scale = (D_NOPE + D_ROPE) ** -0.5 * float(np.log2(np.e))
    if latent_queries:
        tq = MLA_SUB_Q * min(MLA_SUBS_PER_STEP, seq // MLA_SUB_Q)
        nq, n_lat_keys, q_row0 = seq, seq, 0
    else:
        tq = ctx_len
        nq, n_lat_keys, q_row0 = ctx_len, 0, n_lat // tq
    assert tq % MLA_SUB_Q == 0 and nq % tq == 0
    nqb = nq // tq
    ctx_blk0 = n_lat // ctx_len
    in_specs = [pl.BlockSpec((tq, 256), lambda b, h, i: (q_row0 + b * nqb + i, h))]
    args = [q_all]
    if latent_queries:
        in_specs += [
            pl.BlockSpec((seq, 256), lambda b, h, i: (b, h)),
            pl.BlockSpec((seq, 128), lambda b, h, i: (b, kr_blk)),
            pl.BlockSpec((seq, 128), lambda b, h, i: (0, 0)),
        ]
        args += [kv_all, p_all, trig]
    in_specs += [
        pl.BlockSpec((ctx_len, 256), lambda b, h, i: (ctx_blk0 + b, h)),
        pl.BlockSpec((ctx_len, 128), lambda b, h, i: (ctx_blk0 + b, kr_blk)),
    ]
    args += [kv_all, p_all]
    if latent_queries:
        in_specs.append(pl.BlockSpec((tq, 128), lambda b, h, i: (i, 0)))
        args.append(trig)
    in_specs += [pl.BlockSpec((1, 256), lambda b, h, i: (0, 0))] * 2
    args += [gq, gk]
    aliases = {}
    if out_buf is not None:
        in_specs.append(pl.BlockSpec(memory_space=pl.ANY))
        aliases = {len(args): 0}
        args.append(out_buf)
    nk = n_lat_keys + ctx_len
    body = functools.partial(_mla_body, n_lat_keys=n_lat_keys, n_ctx_keys=ctx_len, rope_q=latent_queries,
                             scale=scale, key_chunk=min(256, ctx_len), aliased=out_buf is not None)
    return pl.pallas_call(
        body,
        out_shape=jax.ShapeDtypeStruct((q_all.shape[0], W_A), BF16),
        grid=(batch, H_A, nqb),
        in_specs=in_specs,
        out_specs=pl.BlockSpec((tq, D_V_A), lambda b, h, i: (q_row0 + b * nqb + i, h)),
        scratch_shapes=[pltpu.VMEM((nk, 256), BF16), pltpu.VMEM((nk, D_V_A), BF16)],
        input_output_aliases=aliases,
        compiler_params=_cparams(("arbitrary", "arbitrary", "arbitrary")),
        name="mla_attention",
    )(*args)


def _conv_silu_body(x_ref, prev_ref, next_ref, w_ref, o_ref, *, lat_blocks, seq_blocks, ctx_blocks):
    i = pl.program_id(0)
    j = jnp.where(i < lat_blocks, i % seq_blocks, (i - lat_blocks) % ctx_blocks)
    nb = jnp.where(i < lat_blocks, seq_blocks, ctx_blocks)
    x = x_ref[...]
    rows = x.shape[0]
    row = lax.broadcasted_iota(jnp.int32, x.shape, 0)
    prev_row = jnp.where(j == 0, 0.0, prev_ref[7:8, :])
    next_row = jnp.where(j == nb - 1, 0.0, next_ref[0:1, :])
    xp = jnp.where(row == 0, prev_row, pltpu.roll(x, 1, axis=0))
    xn = jnp.where(row == rows - 1, next_row, pltpu.roll(x, rows - 1, axis=0))
    w = w_ref[...]
    y = xp * w[0:1, :] + x * w[1:2, :] + xn * w[2:3, :]
    y = y * jax.nn.sigmoid(y)
    half = y.shape[1] // 2
    o_ref[:, :half] = y[:, :half].astype(o_ref.dtype)
    o_ref[:, half:] = (y[:, half:] * (DQK_B ** -0.5)).astype(o_ref.dtype)


def conv_silu(p_all, conv_w, *, n_lat, seq, ctx_len):
    n = p_all.shape[0]
    width = 2 * H_B * DQK_B
    rb = ROW_BLOCK
    nblk = n // rb
    sub = rb // 8
    last8 = n // 8 - 1
    body = functools.partial(_conv_silu_body, lat_blocks=n_lat // rb, seq_blocks=seq // rb,
                             ctx_blocks=ctx_len // rb)
    return pl.pallas_call(
        body,
        out_shape=jax.ShapeDtypeStruct((n, width), BF16),
        grid=(nblk,),
        in_specs=[
            pl.BlockSpec((rb, width), lambda i: (i, P_QK // width)),
            pl.BlockSpec((8, width), lambda i: (jnp.maximum(i * sub - 1, 0), P_QK // width)),
            pl.BlockSpec((8, width), lambda i: (jnp.minimum((i + 1) * sub, last8), P_QK // width)),
            pl.BlockSpec((3, width), lambda i: (0, 0)),
        ],
        out_specs=pl.BlockSpec((rb, width), lambda i: (i, 0)),
        compiler_params=_cparams(("parallel",)),
        name="conv_silu",
    )(p_all, p_all, p_all, conv_w)


MLSTM_HEADS_PER_STEP = 2


def _log_sigmoid(x):
    return jnp.minimum(x, 0.0) - jnp.log(1.0 + jnp.exp(-jnp.abs(x)))


def _mlstm_body(qf, kf, vf, gcf, grf, qb, kb, vb, gcb, grb, bc_ref, br_ref, of_ref, ob_ref, c_st, n_st, m_st):
    @pl.when(pl.program_id(2) == 0)
    def _():
        c_st[...] = jnp.zeros_like(c_st)
        n_st[...] = jnp.zeros_like(n_st)
        m_st[...] = jnp.full_like(m_st, M_INIT)

    for hh in range(MLSTM_HEADS_PER_STEP):
        sf, sb = hh, MLSTM_HEADS_PER_STEP + hh
        _mlstm_chunk(0, hh, qf, kf, vf, gcf, grf, bc_ref, br_ref, of_ref, c_st.at[sf], n_st.at[sf], m_st.at[sf])
        _mlstm_chunk(1, hh, qb, kb, vb, gcb, grb, bc_ref, br_ref, ob_ref, c_st.at[sb], n_st.at[sb], m_st.at[sb])


def _mlstm_chunk(direction, hh, q_ref, k_ref, v_ref, gc_ref, gr_ref, bc_ref, br_ref, o_ref, c_st, n_st, m_st):
    h = pl.program_id(1) * MLSTM_HEADS_PER_STEP + hh
    qk_l = slice(hh * DQK_B, (hh + 1) * DQK_B)
    v_l = slice(hh * DV_B, (hh + 1) * DV_B)
    L = MLSTM_CHUNK
    hi = lax.Precision.HIGHEST
    gi = direction * (2 * H_B) + h
    gcol = gc_ref[...] + bc_ref[...]
    lane = lax.broadcasted_iota(jnp.int32, gcol.shape, 1)
    li_c = jnp.sum(jnp.where(lane == gi, gcol, 0.0), axis=1, keepdims=True)
    lf_c = _log_sigmoid(jnp.sum(jnp.where(lane == gi + H_B, gcol, 0.0), axis=1, keepdims=True))
    grow = gr_ref[...] + br_ref[...]
    sub = lax.broadcasted_iota(jnp.int32, grow.shape, 0)
    li_r = jnp.sum(jnp.where(sub == gi, grow, 0.0), axis=0, keepdims=True)
    lf_r = _log_sigmoid(jnp.sum(jnp.where(sub == gi + H_B, grow, 0.0), axis=0, keepdims=True))

    t_i = lax.broadcasted_iota(jnp.int32, (L, L), 0)
    s_i = lax.broadcasted_iota(jnp.int32, (L, L), 1)
    sgn = 1 - 2 * direction
    incl = (s_i - t_i) * sgn <= 0
    incl_f = incl.astype(F32)
    incl_t = ((t_i - s_i) * sgn <= 0).astype(F32)
    bcum_c = jnp.dot(incl_f, jnp.broadcast_to(lf_c, (L, L)), precision=hi, preferred_element_type=F32)
    bcum_r = jnp.dot(jnp.broadcast_to(lf_r, (8, L)), incl_t, precision=hi, preferred_element_type=F32)[0:1, :]
    b_last = jnp.sum(lf_r, axis=1, keepdims=True)

    m_prev = m_st[...]
    a_c = bcum_c[:, 0:1] + m_prev
    dlog = jnp.where(incl, bcum_c - bcum_r + li_r, -jnp.inf)
    m_t = jnp.maximum(a_c, jnp.max(dlog, axis=1, keepdims=True))
    w_intra = jnp.exp(dlog - m_t)
    w_inter = jnp.exp(a_c - m_t)

    q = q_ref[:, qk_l]
    k = k_ref[:, qk_l]
    v = v_ref[:, v_l].astype(BF16)
    c_prev = c_st[...]
    n_prev = n_st[...]
    s = lax.dot_general(q, k, (((1,), (1,)), ((), ())), preferred_element_type=F32) * w_intra
    num = (w_inter * jnp.dot(q, c_prev.astype(BF16), preferred_element_type=F32)
           + jnp.dot(s.astype(BF16), v, preferred_element_type=F32))
    den = (w_inter * jnp.sum(q.astype(F32) * n_prev, axis=1, keepdims=True)
           + jnp.sum(s, axis=1, keepdims=True))
    o_ref[:, v_l] = num / jnp.maximum(jnp.abs(den), jnp.exp(-m_t))

    g_r = b_last - bcum_r + li_r
    g_c = b_last - bcum_c[:, 0:1] + li_c
    m_new = jnp.maximum(b_last + m_prev, jnp.max(g_r, axis=1, keepdims=True))
    decay = jnp.exp(b_last + m_prev - m_new)
    kw = k.astype(F32) * jnp.exp(g_c - m_new)
    c_st[...] = decay * c_prev + lax.dot_general(kw.astype(BF16), v, (((0,), (0,)), ((), ())),
                                                 preferred_element_type=F32)
    n_st[...] = decay * n_prev + jnp.sum(kw, axis=0, keepdims=True)
    m_st[...] = m_new


def mlstm_scan(qk, p_all, gates_t, gate_b, *, batch, seq, ctx_len):
    n = qk.shape[0]
    L = MLSTM_CHUNK
    cc, lc = ctx_len // L, seq // L
    n_lat_blk = batch * lc

    def rowblk(b, d, c):
        ctx_blk = n_lat_blk + b * cc + (c if d == 0 else cc - 1 - c)
        lat_blk = b * lc + (c - cc if d == 0 else lc - 1 - (c - cc))
        return jnp.where(c < cc, ctx_blk, lat_blk)

    hps = MLSTM_HEADS_PER_STEP
    wqk, wv = hps * DQK_B, hps * DV_B
    assert H_B % hps == 0

    def chunk_specs(d):
        return [
            pl.BlockSpec((L, wqk), lambda b, h, c: (rowblk(b, d, c), h)),
            pl.BlockSpec((L, wqk), lambda b, h, c: (rowblk(b, d, c), H_B // hps + h)),
            pl.BlockSpec((L, wv), lambda b, h, c: (rowblk(b, d, c), P_V // wv + h)),
            pl.BlockSpec((L, 128), lambda b, h, c: (rowblk(b, d, c), P_GATE // 128)),
            pl.BlockSpec((4 * H_B, L), lambda b, h, c: (0, rowblk(b, d, c))),
        ]

    def out_spec(d):
        return pl.BlockSpec((L, wv), lambda b, h, c: (rowblk(b, d, c), h))

    gb_col = jnp.zeros((1, 128), F32).at[0, :4 * H_B].set(gate_b)
    gb_row = gate_b.reshape(4 * H_B, 1)
    chunk_args = [qk, qk, p_all, p_all, gates_t]
    return pl.pallas_call(
        _mlstm_body,
        out_shape=(jax.ShapeDtypeStruct((n, W_B), F32), jax.ShapeDtypeStruct((n, W_B), F32)),
        grid=(batch, H_B // hps, cc + lc),
        in_specs=chunk_specs(0) + chunk_specs(1) + [
            pl.BlockSpec((1, 128), lambda b, h, c: (0, 0)),
            pl.BlockSpec((4 * H_B, 1), lambda b, h, c: (0, 0)),
        ],
        out_specs=(out_spec(0), out_spec(1)),
        scratch_shapes=[pltpu.VMEM((2 * hps, DQK_B, DV_B), F32), pltpu.VMEM((2 * hps, 1, DQK_B), F32),
                        pltpu.VMEM((2 * hps, 1, 1), F32)],
        compiler_params=_cparams(("arbitrary",) * 3),
        name="mlstm_scan",
    )(*chunk_args, *chunk_args, gb_col, gb_row)


def _mlstm_finish_body(hf_ref, hb_ref, o_ref, g_ref, out_ref):
    hs = hf_ref[...] + hb_ref[...]
    g = g_ref[...]
    og = o_ref[...]
    for hd in range(H_B):
        sl = slice(hd * DV_B, (hd + 1) * DV_B)
        x = hs[:, sl]
        ms = jnp.mean(x * x, axis=-1, keepdims=True)
        hn = x * lax.rsqrt(ms + NORM_EPS) * g[:, sl]
        out_ref[:, sl] = (jax.nn.sigmoid(og[:, sl]) * hn).astype(out_ref.dtype)


def mlstm_finish(h_fwd, h_bwd, p_all, out_norm_g):
    n = h_fwd.shape[0]
    rb = ROW_BLOCK
    return pl.pallas_call(
        _mlstm_finish_body,
        out_shape=jax.ShapeDtypeStruct((n, W_B), BF16),
        grid=(n // rb,),
        in_specs=[
            pl.BlockSpec((rb, W_B), lambda i: (i, 0)),
            pl.BlockSpec((rb, W_B), lambda i: (i, 0)),
            pl.BlockSpec((rb, W_B), lambda i: (i, P_O // W_B)),
            pl.BlockSpec((1, W_B), lambda i: (0, 0)),
        ],
        out_specs=pl.BlockSpec((rb, W_B), lambda i: (i, 0)),
        compiler_params=_cparams(("parallel",)),
        name="mlstm_finish",
    )(h_fwd, h_bwd, p_all, out_norm_g.reshape(1, W_B))


NA_ROWS_PER_STEP = 4
NA_TILES_PER_STEP = 8


def _na_geometry(rows_n):
    r_t = NA_ROWS_PER_STEP
    kh = min(NA_KH, rows_n)
    win = min(r_t + kh - 1, rows_n)
    tiles = rows_n // r_t
    starts, sigs, cls = [], [], []
    for t in range(tiles):
        rq0 = t * r_t
        start = int(np.clip(rq0 - kh // 2, 0, rows_n - win))
        r0 = np.clip(np.arange(rq0, rq0 + r_t) - kh // 2, 0, rows_n - kh)
        sig = (start - rq0, tuple((r0 - rq0).tolist()))
        if sig not in sigs:
            sigs.append(sig)
        starts.append(start)
        cls.append(sigs.index(sig))
    return kh, win, np.array(starts, np.int32), np.array(cls, np.int32), sigs


def na_bias_rows(rpb):
    cols = np.arange(GRID_W)
    c0 = np.clip(cols - NA_KW // 2, 0, GRID_W - NA_KW)
    col_valid = (cols[None, :] >= c0[:, None]) & (cols[None, :] < c0[:, None] + NA_KW)
    col_off = cols[None, :] - cols[:, None] + (NA_KW - 1)
    c_sel = (col_valid[:, :, None] & (col_off[:, :, None] == np.arange(2 * NA_KW - 1))).astype(np.float32)
    t = jnp.einsum('hrs,cds->hrcd', rpb, c_sel, precision=lax.Precision.HIGHEST)
    t = t + np.where(col_valid, 0.0, NEG).astype(np.float32)[None, None]
    return jnp.concatenate([t, t], axis=-1)


def _na_bias_plan(rows_n):
    r_t = NA_ROWS_PER_STEP
    kh, win, _, _, sigs = _na_geometry(rows_n)
    plan = []
    for (dstart, dr0) in sigs:
        cls_plan = []
        for a in range(r_t):
            row = []
            for j in range(win):
                krow = dstart + j
                ok = dr0[a] <= krow < dr0[a] + kh
                row.append(krow - a + (NA_KH - 1) if ok else None)
            cls_plan.append(row)
        plan.append(cls_plan)
    return plan


def _na_build_bias(rows_ref, btab, plan):
    neg = jnp.full((GRID_W, 2 * GRID_W), NEG, F32)
    lane = lax.broadcasted_iota(jnp.int32, (GRID_W, 2 * GRID_W), 1)

    def piece(ro):
        return neg if ro is None else rows_ref[0, ro]

    for k, cls_plan in enumerate(plan):
        for a, row in enumerate(cls_plan):
            rs = slice(a * GRID_W, (a + 1) * GRID_W)
            for j in range(0, len(row) - 1, 2):
                both = jnp.where(lane < GRID_W, piece(row[j]), piece(row[j + 1]))
                btab[k, rs, j * GRID_W:(j + 2) * GRID_W] = both
            if len(row) % 2:
                j = len(row) - 1
                btab[k, rs, j * GRID_W:(j + 1) * GRID_W] = piece(row[j])[:, :GRID_W]


def _head_rmsnorm(x, g):
    xf = x.astype(F32)
    ms = jnp.mean(xf * xf, axis=-1, keepdims=True)
    return xf * lax.rsqrt(ms + NORM_EPS) * g


def _na_body(start_ref, cls_ref, q_ref, k_ref, v_ref, kc_ref, vc_ref, brow_ref, gq_ref, gk_ref, *rest,
             seq, ctx_len, win_rows, scale, bias_plan, tps):
    o_ref, kn, knc, btab = rest[-4:]
    i = pl.program_id(2)
    chunk = min(512, seq)

    @pl.when(i == 0)
    def _():
        _na_build_bias(brow_ref, btab, bias_plan)
        gk = gk_ref[...]

        def body(c, carry):
            rows = pl.ds(pl.multiple_of(c * chunk, chunk), chunk)
            kn[rows, :] = _head_rmsnorm(k_ref[rows, :], gk).astype(BF16)
            return carry

        lax.fori_loop(0, seq // chunk, body, 0)
        knc[...] = _head_rmsnorm(kc_ref[...], gk).astype(BF16)

    nt = (((1,), (1,)), ((), ()))
    tq = NA_ROWS_PER_STEP * GRID_W
    for tl in range(tps):
        rows = slice(tl * tq, (tl + 1) * tq)
        q = (_head_rmsnorm(q_ref[rows, :], gq_ref[...]) * scale).astype(BF16)
        krow0 = pl.multiple_of(start_ref[i * tps + tl] * GRID_W, GRID_W)
        kw = kn[pl.ds(krow0, win_rows), :]
        vw = v_ref[pl.ds(krow0, win_rows), :]
        s = lax.dot_general(q, kw, nt, preferred_element_type=F32) + btab[cls_ref[i * tps + tl]]
        sc = lax.dot_general(q, knc[...], nt, preferred_element_type=F32)
        m = jnp.maximum(jnp.max(s, axis=-1, keepdims=True), jnp.max(sc, axis=-1, keepdims=True))
        p = jnp.exp(s - m)
        pc = jnp.exp(sc - m)
        l = jnp.sum(p, axis=-1, keepdims=True) + jnp.sum(pc, axis=-1, keepdims=True)
        o = (jnp.dot(p.astype(BF16), vw, preferred_element_type=F32)
             + jnp.dot(pc.astype(BF16), vc_ref[...], preferred_element_type=F32))
        o_ref[rows, :] = (o / l).astype(o_ref.dtype)


def na_attention(qkv, bias_rows, gq, gk, *, batch, seq, ctx_len, out_buf=None):
    rows_n = seq // GRID_W
    r_t = NA_ROWS_PER_STEP
    _, win, starts, cls, _ = _na_geometry(rows_n)
    tps = min(NA_TILES_PER_STEP, rows_n // r_t)
    tq = tps * r_t * GRID_W
    tiles = rows_n // (r_t * tps)
    assert rows_n % (r_t * tps) == 0
    n_lat = batch * seq
    ctx_blk0 = n_lat // ctx_len
    plan = _na_bias_plan(rows_n)
    body = functools.partial(_na_body, seq=seq, ctx_len=ctx_len, win_rows=win * GRID_W, scale=DH_C ** -0.5,
                             bias_plan=plan, tps=tps)
    n_ro = 2 * NA_KH - 1
    in_specs = [
        pl.BlockSpec((tq, DH_C), lambda b, h, i, st, cl: (b * tiles + i, h)),
        pl.BlockSpec((seq, DH_C), lambda b, h, i, st, cl: (b, H_C + h)),
        pl.BlockSpec((seq, DH_C), lambda b, h, i, st, cl: (b, 2 * H_C + h)),
        pl.BlockSpec((ctx_len, DH_C), lambda b, h, i, st, cl: (ctx_blk0 + b, H_C + h)),
        pl.BlockSpec((ctx_len, DH_C), lambda b, h, i, st, cl: (ctx_blk0 + b, 2 * H_C + h)),
        pl.BlockSpec((1, n_ro, GRID_W, 2 * GRID_W), lambda b, h, i, st, cl: (h, 0, 0, 0)),
        pl.BlockSpec((1, DH_C), lambda b, h, i, st, cl: (0, 0)),
        pl.BlockSpec((1, DH_C), lambda b, h, i, st, cl: (0, 0)),
    ]
    args = [jnp.asarray(starts), jnp.asarray(cls), qkv, qkv, qkv, qkv, qkv, bias_rows, gq, gk]
    aliases = {}
    out_rows = n_lat
    if out_buf is not None:
        in_specs.append(pl.BlockSpec(memory_space=pl.ANY))
        aliases = {len(args): 0}
        args.append(out_buf)
        out_rows = out_buf.shape[0]
    grid_spec = pltpu.PrefetchScalarGridSpec(
        num_scalar_prefetch=2,
        grid=(batch, H_C, tiles),
        in_specs=in_specs,
        out_specs=pl.BlockSpec((tq, DH_C), lambda b, h, i, st, cl: (b * tiles + i, h)),
        scratch_shapes=[pltpu.VMEM((seq, DH_C), BF16), pltpu.VMEM((ctx_len, DH_C), BF16),
                        pltpu.VMEM((len(plan), r_t * GRID_W, win * GRID_W), F32)],
    )
    return pl.pallas_call(
        body,
        out_shape=jax.ShapeDtypeStruct((out_rows, H_C * DH_C), BF16),
        grid_spec=grid_spec,
        input_output_aliases=aliases,
        compiler_params=_cparams(("arbitrary", "arbitrary", "arbitrary")),
        name="na_attention",
    )(*args)


def _ctx_attn_body(q_ref, k_ref, v_ref, gq_ref, gk_ref, buf_ref, o_ref, *, scale):
    del buf_ref
    q = (_head_rmsnorm(q_ref[...], gq_ref[...]) * scale).astype(BF16)
    k = _head_rmsnorm(k_ref[...], gk_ref[...]).astype(BF16)
    s = lax.dot_general(q, k, (((1,), (1,)), ((), ())), preferred_element_type=F32)
    m = jnp.max(s, axis=-1, keepdims=True)
    p = jnp.exp(s - m)
    l = jnp.sum(p, axis=-1, keepdims=True)
    o = jnp.dot(p.astype(BF16), v_ref[...], preferred_element_type=F32)
    o_ref[...] = (o / l).astype(o_ref.dtype)


def na_ctx_attention(qkv, gq, gk, out_buf, *, batch, seq, ctx_len):
    ctx_blk0 = batch * seq // ctx_len
    return pl.pallas_call(
        functools.partial(_ctx_attn_body, scale=DH_C ** -0.5),
        out_shape=jax.ShapeDtypeStruct(out_buf.shape, out_buf.dtype),
        grid=(batch, H_C),
        in_specs=[
            pl.BlockSpec((ctx_len, DH_C), lambda b, h: (ctx_blk0 + b, h)),
            pl.BlockSpec((ctx_len, DH_C), lambda b, h: (ctx_blk0 + b, H_C + h)),
            pl.BlockSpec((ctx_len, DH_C), lambda b, h: (ctx_blk0 + b, 2 * H_C + h)),
            pl.BlockSpec((1, DH_C), lambda b, h: (0, 0)),
            pl.BlockSpec((1, DH_C), lambda b, h: (0, 0)),
            pl.BlockSpec(memory_space=pl.ANY),
        ],
        out_specs=pl.BlockSpec((ctx_len, DH_C), lambda b, h: (ctx_blk0 + b, h)),
        input_output_aliases={5: 0},
        compiler_params=_cparams(("parallel", "parallel")),
        name="na_ctx_attention",
    )(qkv, qkv, qkv, gq, gk, out_buf)


def _slab_copy(src_ref, src_tok, dst_ref, dst_tok, sem):
    def rows(tok):
        off = tok * SLAB_STRIDE
        return pl.ds(off if isinstance(off, int) else pl.multiple_of(off, 8), SLAB_ROWS)

    return pltpu.make_async_copy(src_ref.at[rows(src_tok), :], dst_ref.at[rows(dst_tok), :], sem)


def _load_slabs(ref, rows, dtype):
    return jnp.concatenate([ref[pl.ds(j, rows, stride=SLAB_STRIDE), :].astype(dtype) for j in range(SLAB_ROWS)],
                           axis=1)


def moe_slot_plan(route, counts, n_tiles):
    cnt = counts[0, :N_EXPERTS].astype(jnp.int32)
    padded = (cnt + MOE_TILE - 1) // MOE_TILE * MOE_TILE
    ends = jnp.cumsum(padded)
    base = ends - padded
    n_used = ends[-1] // MOE_TILE
    tile_start = jnp.arange(n_tiles, dtype=jnp.int32) * MOE_TILE
    last_start = jnp.maximum(ends[-1] - MOE_TILE, 0)
    tile_expert = jnp.searchsorted(ends, jnp.minimum(tile_start, last_start), side='right').astype(jnp.int32)
    tile_expert = jnp.minimum(tile_expert, N_EXPERTS - 1)
    ids = route[:, 0:2].astype(jnp.int32)
    pos = jnp.take(base, ids, axis=0) + route[:, 4:6].astype(jnp.int32)
    return pos[:, 0], pos[:, 1], tile_expert, n_used.reshape(1).astype(jnp.int32)


def _slot_source_body(p1_ref, p2_ref, src_ref, *, n_tok):
    def clear(i, c):
        src_ref[i] = 0
        return c

    lax.fori_loop(0, src_ref.shape[0], clear, 0, unroll=8)

    def fill(n, c):
        src_ref[p1_ref[n]] = n
        src_ref[p2_ref[n]] = n
        return c

    lax.fori_loop(0, n_tok, fill, 0, unroll=4)


def moe_slot_source(pos1, pos2, n_slots):
    smem = pl.BlockSpec(memory_space=pltpu.SMEM)
    return pl.pallas_call(
        functools.partial(_slot_source_body, n_tok=pos1.shape[0]),
        out_shape=jax.ShapeDtypeStruct((n_slots,), jnp.int32),
        in_specs=[smem, smem],
        out_specs=smem,
        name="moe_slot_source",
    )(pos1, pos2)


def _gather_slabs(src_ref, tok_ref, tok0, dst_ref, sem, n):
    def issue(r, c):
        _slab_copy(src_ref, tok_ref[tok0 + r], dst_ref, r, sem).start()
        return c

    lax.fori_loop(0, n, issue, 0, unroll=8)


def _wait_slabs(src_ref, dst_ref, sem, n):
    rows = pl.ds(0, n * SLAB_ROWS)
    pltpu.make_async_copy(src_ref.at[rows, :], dst_ref.at[rows, :], sem).wait()


def _moe_expert_body(te_ref, nu_ref, src_ref, hs_ref, wg_ref, wu_ref, wd_ref, o_ref, xbuf, wg_bf, wu_bf, wd_bf,
                     sems):
    t = pl.program_id(0)
    tm = MOE_TILE
    slot = t % 2

    @pl.when(t < nu_ref[0])
    def _():
        @pl.when(t == 0)
        def _():
            _gather_slabs(hs_ref, src_ref, 0, xbuf.at[0], sems.at[0], tm)

        @pl.when(t + 1 < nu_ref[0])
        def _():
            _gather_slabs(hs_ref, src_ref, (t + 1) * tm, xbuf.at[1 - slot], sems.at[1 - slot], tm)

        @pl.when((t == 0) | (te_ref[t] != te_ref[jnp.maximum(t - 1, 0)]))
        def _():
            wg_bf[...] = wg_ref[...].astype(BF16)
            wu_bf[...] = wu_ref[...].astype(BF16)
            wd_bf[...] = wd_ref[...].astype(BF16)

        _wait_slabs(hs_ref, xbuf.at[slot], sems.at[slot], tm)
        x = _load_slabs(xbuf.at[slot], tm, BF16)
        nt = (((1,), (1,)), ((), ()))
        a = lax.dot_general(x, wg_bf[...], nt, preferred_element_type=F32)
        u = lax.dot_general(x, wu_bf[...], nt, preferred_element_type=F32)
        y = (a * jax.nn.sigmoid(a) * u).astype(BF16)
        _store_slabs(o_ref, jnp.dot(y, wd_bf[...], preferred_element_type=F32), tm)

    @pl.when(t >= nu_ref[0])
    def _():
        o_ref[...] = jnp.zeros_like(o_ref)


def moe_experts(hs, src, tile_expert, n_used, w_gate_t, w_up_t, w_down, layer):
    n_tiles = tile_expert.shape[0]
    f, d = w_gate_t.shape[-2:]
    blk = MOE_TILE * SLAB_STRIDE
    wspec = pl.BlockSpec((None, None, f, d), lambda t, te, nu, src: (layer, te[t], 0, 0))
    grid_spec = pltpu.PrefetchScalarGridSpec(
        num_scalar_prefetch=3,
        grid=(n_tiles,),
        in_specs=[pl.BlockSpec(memory_space=pl.ANY), wspec, wspec, wspec],
        out_specs=pl.BlockSpec((blk, 128), lambda t, te, nu, src: (t, 0)),
        scratch_shapes=[pltpu.VMEM((2, blk, 128), F32), pltpu.VMEM((f, d), BF16), pltpu.VMEM((f, d), BF16),
                        pltpu.VMEM((f, d), BF16), pltpu.SemaphoreType.DMA((2,))],
    )
    return pl.pallas_call(
        _moe_expert_body,
        out_shape=jax.ShapeDtypeStruct((n_tiles * blk, 128), F32),
        grid_spec=grid_spec,
        compiler_params=_dma_cparams(("arbitrary",)),
        name="moe_experts",
    )(tile_expert, n_used, src, hs, w_gate_t, w_up_t, w_down)


def _moe_combine_body(p1_ref, p2_ref, ys_ref, route_ref, res_ref, gate_ref, o_ref, abuf, bbuf, sems):
    i = pl.program_id(0)
    rb = ROW_BLOCK
    slot = i % 2

    def gather(blk, s):
        _gather_slabs(ys_ref, p1_ref, blk * rb, abuf.at[s], sems.at[0, s], rb)
        _gather_slabs(ys_ref, p2_ref, blk * rb, bbuf.at[s], sems.at[1, s], rb)

    @pl.when(i == 0)
    def _():
        gather(0, 0)

    @pl.when(i + 1 < pl.num_programs(0))
    def _():
        gather(i + 1, 1 - slot)

    route = route_ref[...]
    lane = lax.broadcasted_iota(jnp.int32, route.shape, 1)
    w1 = jnp.sum(jnp.where(lane == 2, route, 0.0), axis=1, keepdims=True)
    w2 = jnp.sum(jnp.where(lane == 3, route, 0.0), axis=1, keepdims=True)
    _wait_slabs(ys_ref, abuf.at[slot], sems.at[0, slot], rb)
    _wait_slabs(ys_ref, bbuf.at[slot], sems.at[1, slot], rb)
    for j in range(SLAB_ROWS):
        sl = slice(j * 128, (j + 1) * 128)
        rows = pl.ds(j, rb, stride=SLAB_STRIDE)
        mix = w1 * abuf[slot, rows, :] + w2 * bbuf[slot, rows, :]
        o_ref[:, sl] = res_ref[:, sl] + gate_ref[0][:, sl] * mix


def moe_combine(ys, pos1, pos2, route, resid, mods, k_gate, dims):
    n, d = route.shape[0], resid.shape[1]
    n_lat, seq, batch = dims
    rb = ROW_BLOCK
    mrow = functools.partial(_mod_row, rows_per_block=rb, n_lat=n_lat, seq=seq, batch=batch)
    grid_spec = pltpu.PrefetchScalarGridSpec(
        num_scalar_prefetch=2,
        grid=(n // rb,),
        in_specs=[
            pl.BlockSpec(memory_space=pl.ANY),
            pl.BlockSpec((rb, 128), lambda i, p1, p2: (i, 0)),
            pl.BlockSpec((rb, d), lambda i, p1, p2: (i, 0)),
            pl.BlockSpec((1, 1, d), lambda i, p1, p2: (mrow(i) * 6 + k_gate, 0, 0)),
        ],
        out_specs=pl.BlockSpec((rb, d), lambda i, p1, p2: (i, 0)),
        scratch_shapes=[pltpu.VMEM((2, rb * SLAB_STRIDE, 128), F32), pltpu.VMEM((2, rb * SLAB_STRIDE, 128), F32),
                        pltpu.SemaphoreType.DMA((2, 2))],
    )
    return pl.pallas_call(
        _moe_combine_body,
        out_shape=jax.ShapeDtypeStruct((n, d), F32),
        grid_spec=grid_spec,
        compiler_params=_dma_cparams(("arbitrary",)),
        name="moe_combine",
    )(pos1, pos2, ys, route, resid, mods)


def _swap16(a):
    lead = a.shape[:-1]
    return jnp.flip(a.reshape(*lead, D_ROPE // 32, 2, 16), axis=-2).reshape(*lead, D_ROPE)


W_IN_TILE = 512
_C_KR = Q_LORA + KV_LORA
_C_QB = _C_KR + D_ROPE
_C_G = _C_QB + 2 * H_B * DQK_B + 2 * W_B


def _w_in_tail_t(wt_all, j):
    wt = wt_all[j]
    kr = wt[_C_KR:_C_QB]
    kr_sw = jnp.flip(kr.reshape(D_ROPE // 32, 2, 16, -1), axis=1).reshape(D_ROPE, -1)
    gates = wt[_C_G:]
    pad = W_IN_TILE - 2 * D_ROPE - gates.shape[0]
    return jnp.concatenate([kr, kr_sw, jnp.pad(gates, ((0, pad), (0, 0)))], axis=0)


def _win_body(h_ref, wt_ref, tail_ref, o_ref, wbf_ref, *, n_main):
    j = pl.program_id(0)

    @pl.when((pl.program_id(1) == 0) & (j < n_main))
    def _():
        wbf_ref[...] = wt_ref[0].astype(BF16)

    @pl.when((pl.program_id(1) == 0) & (j >= n_main))
    def _():
        wbf_ref[...] = tail_ref[...].astype(BF16)

    o_ref[...] = lax.dot_general(h_ref[...], wbf_ref[...], (((1,), (1,)), ((), ())),
                                 preferred_element_type=F32).astype(o_ref.dtype)


def input_projection(h, wt_all, layer):
    m, d = h.shape
    tm = _row_tile(m)
    tn = W_IN_TILE
    n_a = (_C_G - _C_QB) // tn
    n_b = _C_KR // tn
    n_main = n_a + n_b
    assert (_C_G - _C_QB) % tn == 0 and _C_KR % tn == 0 and (n_main + 1) * tn == P_WIDTH and _C_QB % 8 == 0

    def w_row(j):
        row = jnp.where(j < n_a, _C_QB + j * tn, jnp.where(j < n_main, (j - n_a) * tn, 0))
        return pl.multiple_of(row, 8)

    return pl.pallas_call(
        functools.partial(_win_body, n_main=n_main),
        out_shape=jax.ShapeDtypeStruct((m, P_WIDTH), F32),
        grid=(n_main + 1, m // tm),
        in_specs=[
            pl.BlockSpec((tm, d), lambda j, i: (i, 0)),
            pl.BlockSpec((pl.Element(1), pl.Element(tn), pl.Element(d)), lambda j, i: (layer, w_row(j), 0)),
            pl.BlockSpec((tn, d), lambda j, i: (0, 0), pipeline_mode=pl.Buffered(1)),
        ],
        out_specs=pl.BlockSpec((tm, tn), lambda j, i: (i, j)),
        scratch_shapes=[pltpu.VMEM((tn, d), BF16)],
        compiler_params=_cparams(("arbitrary", "arbitrary")),
        name="input_projection",
    )(h, wt_all, _w_in_tail_t(wt_all, layer))


def _ext_w_q_up(w):
    w3 = w.reshape(Q_LORA, H_A, D_NOPE + D_ROPE)
    return jnp.concatenate([w3, _swap16(w3[:, :, D_NOPE:])], axis=2).reshape(Q_LORA, H_A * 256)


def _ext_gain(g):
    return jnp.concatenate([g, _swap16(g[D_NOPE:])]).reshape(1, 256)


def _rope_tables(seq):
    t = jnp.arange(seq)
    half = D_ROPE // 4
    inv = ROPE_THETA ** (-jnp.arange(half, dtype=F32) / half)
    ar = (t // GRID_W).astype(F32)[:, None] * inv[None, :]
    ac = (t % GRID_W).astype(F32)[:, None] * inv[None, :]
    cr, sr, cc, sc = jnp.cos(ar), jnp.sin(ar), jnp.cos(ac), jnp.sin(ac)
    return jnp.concatenate([cr, cr, cc, cc, -sr, sr, -sc, sc], axis=1)


def kernel(x, c, ctx, c_ctx, ada_w, ada_b, norm1_g, norm2_g, ab_w_in, mla_q_norm_g, mla_kv_norm_g, mla_w_q_up, mla_w_kv_up, mla_qn_g, mla_kn_g, mlstm_conv_w, mlstm_gate_b, mlstm_out_norm_g, ab_w_out, na_w_qkv, na_qn_g, na_kn_g, na_rpb, na_w_out, moe_w_rg, moe_b_rg, moe_w_re, moe_b_re, moe_w_gate, moe_w_up, moe_w_down):
    batch, seq, d = x.shape
    ctx_len = ctx.shape[1]
    depth = ada_w.shape[0]
    n_lat = batch * seq
    dims = (n_lat, seq, batch)
    assert d == D_MODEL and batch + 1 <= MOD_ROWS and seq % 512 == 0 and ctx_len == ROW_BLOCK

    xs = jnp.concatenate([x.reshape(n_lat, d), ctx.reshape(batch * ctx_len, d)], axis=0)
    cvec = jnp.concatenate([c, c_ctx[None, :]], axis=0)
    mods_all = ada_mods(cvec, ada_w, ada_b).reshape(depth, MOD_ROWS * 6, 1, d)
    trig = _rope_tables(seq)

    w_gate_t, w_up_t = jnp.swapaxes(moe_w_gate, 2, 3), jnp.swapaxes(moe_w_up, 2, 3)
    w_in_t = jnp.swapaxes(ab_w_in, 1, 2)
    kw = dict(batch=batch, seq=seq, ctx_len=ctx_len)
    for layer in range(depth):
        j = layer // 2
        last = layer == depth - 1
        mods = mods_all[layer]
        h = norm_mod(xs, norm1_g[layer], mods, (0, 1), dims)
        if layer % 2 == 0:
            p_all = input_projection(h, w_in_t, j)
            tm_big = _row_tile(p_all.shape[0])
            q_all = matmul(p_all, _ext_w_q_up(mla_w_q_up[j]), out_dtype=BF16, a_col=P_CQ // Q_LORA, k=Q_LORA,
                           rms_gain=mla_q_norm_g[j], tm=tm_big)
            kv_all = matmul(p_all, mla_w_kv_up, layer=j, out_dtype=BF16, a_col=P_CKV // KV_LORA, k=KV_LORA,
                            rms_gain=mla_kv_norm_g[j], tm=tm_big)
            gq, gk = _ext_gain(mla_qn_g[j]), _ext_gain(mla_kn_g[j])
            mix_a = jnp.zeros((xs.shape[0], W_A), BF16)
            mix_a = mla_attention(q_all, kv_all, p_all, trig, gq, gk, latent_queries=True, out_buf=mix_a, **kw)
            mix_a = mla_attention(q_all, kv_all, p_all, trig, gq, gk, latent_queries=False, out_buf=mix_a, **kw)
            qk = conv_silu(p_all, mlstm_conv_w[j], n_lat=n_lat, seq=seq, ctx_len=ctx_len)
            gates_t = p_all[:, P_GATE:P_GATE + 4 * H_B].T
            h_fwd, h_bwd = mlstm_scan(qk, p_all, gates_t, mlstm_gate_b[j], **kw)
            mix_b = mlstm_finish(h_fwd, h_bwd, p_all, mlstm_out_norm_g[j])
            xs = matmul(mix_a, ab_w_out, layer=j, a2=mix_b, out_dtype=F32, resid=xs, mods=mods, k_gate=2,
                        dims=dims)
        else:
            qkv = matmul(h, na_w_qkv, layer=j, out_dtype=BF16, tm=_row_tile(h.shape[0]))
            gq, gk = na_qn_g[j].reshape(1, DH_C), na_kn_g[j].reshape(1, DH_C)
            bias = na_bias_rows(na_rpb[j])
            if last:
                mixed = na_attention(qkv, bias, gq, gk, **kw)
            else:
                mixed = jnp.zeros((xs.shape[0], H_C * DH_C), BF16)
                mixed = na_attention(qkv, bias, gq, gk, out_buf=mixed, **kw)
                mixed = na_ctx_attention(qkv, gq, gk, mixed, **kw)
            xs = matmul(mixed, na_w_out, layer=j, out_dtype=F32, resid=xs, mods=mods, k_gate=2, dims=dims)

        w_r = jnp.pad(jnp.concatenate([moe_w_re[layer], moe_w_rg[layer]], axis=1),
                      ((0, 0), (0, 128 - N_EXPERTS - N_GROUPS)))
        b_r = jnp.pad(jnp.concatenate([moe_b_re[layer], moe_b_rg[layer]]), (0, 128 - N_EXPERTS - N_GROUPS))
        hs, route, counts = norm_router(xs, norm2_g[layer], mods, (3, 4), dims, w_r, b_r.reshape(1, 128))
        n_tiles = 2 * xs.shape[0] // MOE_TILE + N_EXPERTS
        pos1, pos2, tile_expert, n_used = moe_slot_plan(route, counts, n_tiles)
        src = moe_slot_source(pos1, pos2, n_tiles * MOE_TILE)
        ys = moe_experts(hs, src, tile_expert, n_used, w_gate_t, w_up_t, moe_w_down, layer)
        xs = moe_combine(ys, pos1, pos2, route, xs, mods, 5, dims)
    return xs[:n_lat].reshape(batch, seq, d) if xs.shape[0] != n_lat else xs.reshape(batch, seq, d)
```

```python
import functools

import numpy as np
import jax
import jax.numpy as jnp
from jax import lax
from jax.experimental import pallas as pl
from jax.experimental.pallas import tpu as pltpu

F32 = jnp.float32
BF16 = jnp.bfloat16

D_MODEL = 4096
GRID_W = 64
H_A = 16
Q_LORA = 1024
KV_LORA = 512
D_NOPE = 128
D_ROPE = 64
D_V_A = 128
H_B = 4
DQK_B = 256
DV_B = 512
MLSTM_CHUNK = 128
M_INIT = -1e30
H_C = 32
DH_C = 128
NA_KH = 8
NA_KW = 16
N_GROUPS = 4
EXPERTS_PER_GROUP = 8
N_EXPERTS = 32
D_FF_EXPERT = 192
ROPE_THETA = 10000.0
NORM_EPS = 1e-6
W_A = H_A * D_V_A
W_B = H_B * DV_B
NEG = -1e30

VMEM_LIMIT_BYTES = 56 * 1024 * 1024
ROW_BLOCK = 256
MOD_ROWS = 8
SLAB_ROWS = 32
SLAB_STRIDE = 40
MOE_TILE = 256

P_QK = 0
P_V = 2048
P_O = 4096
P_CQ = 6144
P_CKV = 7168
P_KR = 7680
P_GATE = 7808
P_WIDTH = 8192


def _cparams(sem):
    return pltpu.CompilerParams(dimension_semantics=sem, vmem_limit_bytes=VMEM_LIMIT_BYTES)


MAX_ROW_TILE = 1100


def _row_tile(m, limit=MAX_ROW_TILE):
    return max(t for t in range(16, limit + 1, 16) if m % t == 0)


def _dma_cparams(sem):
    return pltpu.CompilerParams(dimension_semantics=sem, vmem_limit_bytes=VMEM_LIMIT_BYTES,
                                disable_bounds_checks=True)


def _mod_row(i, rows_per_block, n_lat, seq, batch):
    return jnp.where(i < n_lat // rows_per_block, i // (seq // rows_per_block), batch)


ADA_ROW_CHUNK = 512


def _ada_body(c_ref, w_ref, b_ref, o_ref, sc_ref, *, n_vec):
    d, tn = w_ref.shape[1], w_ref.shape[2]
    reps = tn // 128

    @pl.when((pl.program_id(0) == 0) & (pl.program_id(1) == 0))
    def _():
        c = c_ref[...]
        sc_ref[...] = c * jax.nn.sigmoid(c)

    def chunk(ci, accs):
        rows = pl.ds(pl.multiple_of(ci * ADA_ROW_CHUNK, ADA_ROW_CHUNK), ADA_ROW_CHUNK)
        w = w_ref[0, rows, :]
        out = []
        for m in range(n_vec):
            cm = jnp.concatenate([sc_ref[m, rows, :]] * reps, axis=1)
            part = (w * cm).reshape(ADA_ROW_CHUNK // 8, 8, tn).sum(axis=0)
            out.append(accs[m] + part)
        return tuple(out)

    accs = lax.fori_loop(0, d // ADA_ROW_CHUNK, chunk, tuple(jnp.zeros((8, tn), F32) for _ in range(n_vec)))
    rows = [jnp.sum(a, axis=0, keepdims=True) for a in accs]
    rows += [jnp.zeros((1, tn), F32)] * (MOD_ROWS - n_vec)
    o_ref[0] = jnp.concatenate(rows, axis=0) + b_ref[0]


def ada_mods(cvec, ada_w, ada_b, tn=512):
    depth, d, n6 = ada_w.shape
    n_vec = cvec.shape[0]
    c_lanes = jnp.broadcast_to(cvec[:, :, None], (n_vec, d, 128))
    return pl.pallas_call(
        functools.partial(_ada_body, n_vec=n_vec),
        out_shape=jax.ShapeDtypeStruct((depth, MOD_ROWS, n6), F32),
        grid=(depth, n6 // tn),
        in_specs=[
            pl.BlockSpec((n_vec, d, 128), lambda l, j: (0, 0, 0)),
            pl.BlockSpec((1, d, tn), lambda l, j: (l, 0, j)),
            pl.BlockSpec((1, 1, tn), lambda l, j: (l, 0, j)),
        ],
        out_specs=pl.BlockSpec((1, MOD_ROWS, tn), lambda l, j: (l, 0, j)),
        scratch_shapes=[pltpu.VMEM((n_vec, d, 128), F32)],
        compiler_params=_cparams(("arbitrary", "arbitrary")),
        name="ada_mods",
    )(c_lanes, ada_w, ada_b.reshape(depth, 1, n6))


def _modulated_norm(x_ref, g_ref, sh_ref, sc_ref):
    x = x_ref[...]
    ms = jnp.mean(x * x, axis=-1, keepdims=True)
    y = x * lax.rsqrt(ms + NORM_EPS) * g_ref[...]
    return y * (1.0 + sc_ref[0]) + sh_ref[0]


def _norm_body(x_ref, g_ref, sh_ref, sc_ref, o_ref):
    o_ref[...] = _modulated_norm(x_ref, g_ref, sh_ref, sc_ref).astype(o_ref.dtype)


def _store_slabs(ref, mat, rows):
    for j in range(SLAB_ROWS):
        ref[pl.ds(j, rows, stride=SLAB_STRIDE), :] = mat[:, j * 128:(j + 1) * 128]
    for j in range(SLAB_ROWS, SLAB_STRIDE):
        ref[pl.ds(j, rows, stride=SLAB_STRIDE), :] = jnp.zeros((rows, 128), mat.dtype)


def _norm_router_body(x_ref, g_ref, sh_ref, sc_ref, wr_ref, br_ref, hs_ref, route_ref, cnt_ref, carry):
    @pl.when(pl.program_id(0) == 0)
    def _():
        carry[...] = jnp.zeros_like(carry)

    h = _modulated_norm(x_ref, g_ref, sh_ref, sc_ref)
    _store_slabs(hs_ref, h, ROW_BLOCK)
    logits = jnp.dot(h, wr_ref[...], precision=lax.Precision.HIGHEST,
                     preferred_element_type=F32) + br_ref[...]
    lane = lax.broadcasted_iota(jnp.int32, logits.shape, 1)
    big = jnp.int32(1 << 20)
    is_g = (lane >= N_EXPERTS) & (lane < N_EXPERTS + N_GROUPS)
    gl = jnp.where(is_g, logits, NEG)
    gmax = jnp.max(gl, axis=-1, keepdims=True)
    g_idx = jnp.min(jnp.where(gl == gmax, lane, big), axis=-1, keepdims=True) - N_EXPERTS
    p_g = 1.0 / jnp.sum(jnp.where(is_g, jnp.exp(gl - gmax), 0.0), axis=-1, keepdims=True)
    lo = g_idx * EXPERTS_PER_GROUP
    in_grp = (lane >= lo) & (lane < lo + EXPERTS_PER_GROUP)
    el = jnp.where(in_grp, logits, NEG)
    e1 = jnp.max(el, axis=-1, keepdims=True)
    i1 = jnp.min(jnp.where(el == e1, lane, big), axis=-1, keepdims=True)
    el2 = jnp.where(lane == i1, NEG, el)
    e2 = jnp.max(el2, axis=-1, keepdims=True)
    i2 = jnp.min(jnp.where(el2 == e2, lane, big), axis=-1, keepdims=True)
    t = jnp.exp(e2 - e1)
    w1 = p_g / (1.0 + t)
    w2 = w1 * t
    onehot = jnp.where((lane == i1) | (lane == i2), 1.0, 0.0)
    t_i = lax.broadcasted_iota(jnp.int32, (ROW_BLOCK, ROW_BLOCK), 0)
    s_i = lax.broadcasted_iota(jnp.int32, (ROW_BLOCK, ROW_BLOCK), 1)
    before = jnp.where(s_i < t_i, 1.0, 0.0).astype(BF16)
    seen = jnp.dot(before, onehot.astype(BF16), preferred_element_type=F32) + carry[...]
    rank1 = jnp.sum(jnp.where(lane == i1, seen, 0.0), axis=-1, keepdims=True)
    rank2 = jnp.sum(jnp.where(lane == i2, seen, 0.0), axis=-1, keepdims=True)
    total = carry[...] + jnp.sum(onehot, axis=0, keepdims=True)
    carry[...] = total
    cnt_ref[...] = jnp.broadcast_to(total, cnt_ref.shape)
    cols = (i1.astype(F32), i2.astype(F32), w1, w2, rank1, rank2)
    route = jnp.zeros(logits.shape, F32)
    for k, v in enumerate(cols):
        route = jnp.where(lane == k, v, route)
    route_ref[...] = route


def _norm_specs(d, mods_k, dims):
    n_lat, seq, batch = dims
    mrow = functools.partial(_mod_row, rows_per_block=ROW_BLOCK, n_lat=n_lat, seq=seq, batch=batch)
    k_shift, k_scale = mods_k
    return [
        pl.BlockSpec((ROW_BLOCK, d), lambda i: (i, 0)),
        pl.BlockSpec((1, d), lambda i: (0, 0)),
        pl.BlockSpec((1, 1, d), lambda i: (mrow(i) * 6 + k_shift, 0, 0)),
        pl.BlockSpec((1, 1, d), lambda i: (mrow(i) * 6 + k_scale, 0, 0)),
    ]


def norm_mod(x, g, mods, mods_k, dims):
    n, d = x.shape
    return pl.pallas_call(
        _norm_body,
        out_shape=jax.ShapeDtypeStruct((n, d), BF16),
        grid=(n // ROW_BLOCK,),
        in_specs=_norm_specs(d, mods_k, dims),
        out_specs=pl.BlockSpec((ROW_BLOCK, d), lambda i: (i, 0)),
        compiler_params=_cparams(("parallel",)),
        name="norm_mod",
    )(x, g.reshape(1, d), mods, mods)


def norm_router(x, g, mods, mods_k, dims, w_r, b_r):
    n, d = x.shape
    assert d == SLAB_ROWS * 128
    return pl.pallas_call(
        _norm_router_body,
        out_shape=(jax.ShapeDtypeStruct((n * SLAB_STRIDE, 128), F32), jax.ShapeDtypeStruct((n, 128), F32),
                   jax.ShapeDtypeStruct((8, 128), F32)),
        grid=(n // ROW_BLOCK,),
        in_specs=_norm_specs(d, mods_k, dims) + [
            pl.BlockSpec((d, 128), lambda i: (0, 0)),
            pl.BlockSpec((1, 128), lambda i: (0, 0)),
        ],
        out_specs=(pl.BlockSpec((ROW_BLOCK * SLAB_STRIDE, 128), lambda i: (i, 0)),
                   pl.BlockSpec((ROW_BLOCK, 128), lambda i: (i, 0)),
                   pl.BlockSpec((8, 128), lambda i: (0, 0))),
        scratch_shapes=[pltpu.VMEM((1, 128), F32)],
        compiler_params=_cparams(("arbitrary",)),
        name="norm_router",
    )(x, g.reshape(1, d), mods, mods, w_r, b_r)


def _mm_body(*refs, rms, gated, k1, two_a):
    it = iter(refs)
    a_ref = next(it)
    a2_ref = next(it) if two_a else None
    g_ref = next(it) if rms else None
    w_ref = next(it)
    res_ref = next(it) if gated else None
    gate_ref = next(it) if gated else None
    o_ref = next(it)
    wbf_ref = next(it)

    @pl.when(pl.program_id(1) == 0)
    def _():
        wbf_ref[...] = w_ref[...].astype(BF16)

    a = a_ref[...]
    if rms:
        af = a.astype(F32)
        ms = jnp.mean(af * af, axis=-1, keepdims=True)
        a = (af * lax.rsqrt(ms + NORM_EPS) * g_ref[...]).astype(BF16)
    if two_a:
        acc = (jnp.dot(a, wbf_ref[:k1, :], preferred_element_type=F32)
               + jnp.dot(a2_ref[...], wbf_ref[k1:, :], preferred_element_type=F32))
    else:
        acc = jnp.dot(a, wbf_ref[...], preferred_element_type=F32)
    if gated:
        acc = res_ref[...] + gate_ref[0] * acc
    o_ref[...] = acc.astype(o_ref.dtype)


def matmul(a, w, *, out_dtype, layer=None, a_col=0, k=None, a2=None, rms_gain=None, resid=None, mods=None,
           k_gate=None, dims=None, tm=512, tn=512):
    m = a.shape[0]
    kw_, nw = w.shape[-2], w.shape[-1]
    two_a = a2 is not None
    k = (a.shape[1] if two_a else kw_) if k is None else k
    assert kw_ == k + (a2.shape[1] if two_a else 0) and m % tm == 0 and nw % tn == 0
    rms = rms_gain is not None
    gated = resid is not None
    in_specs = [pl.BlockSpec((tm, k), lambda j, i: (i, a_col))]
    args = [a]
    if two_a:
        in_specs.append(pl.BlockSpec((tm, a2.shape[1]), lambda j, i: (i, 0)))
        args.append(a2)
    if rms:
        in_specs.append(pl.BlockSpec((1, k), lambda j, i: (0, 0)))
        args.append(rms_gain.reshape(1, k))
    if layer is None:
        in_specs.append(pl.BlockSpec((kw_, tn), lambda j, i: (0, j)))
    else:
        in_specs.append(pl.BlockSpec((None, kw_, tn), lambda j, i: (layer, 0, j)))
    args.append(w)
    if gated:
        n_lat, seq, batch = dims
        mrow = functools.partial(_mod_row, rows_per_block=tm, n_lat=n_lat, seq=seq, batch=batch)
        in_specs.append(pl.BlockSpec((tm, tn), lambda j, i: (i, j)))
        in_specs.append(pl.BlockSpec((1, 1, tn), lambda j, i: (mrow(i) * 6 + k_gate, 0, j)))
        args += [resid, mods]
    return pl.pallas_call(
        functools.partial(_mm_body, rms=rms, gated=gated, k1=k, two_a=two_a),
        out_shape=jax.ShapeDtypeStruct((m, nw), out_dtype),
        grid=(nw // tn, m // tm),
        in_specs=in_specs,
        out_specs=pl.BlockSpec((tm, tn), lambda j, i: (i, j)),
        scratch_shapes=[pltpu.VMEM((kw_, tn), BF16)],
        compiler_params=_cparams(("arbitrary", "arbitrary")),
        name="matmul",
    )(*args)


def _rope_rot(pr, trig):
    if trig is not None:
        pr = pr * trig
    else:
        lane = lax.broadcasted_iota(jnp.int32, pr.shape, 1)
        pr = jnp.where(lane < D_ROPE, pr, 0.0)
    return pr + pltpu.roll(pr, D_ROPE, axis=1)


def _mla_body(*refs, n_lat_keys, n_ctx_keys, rope_q, scale, key_chunk, aliased):
    it = iter(refs)
    q_ref = next(it)
    kvl_ref = next(it) if n_lat_keys else None
    krl_ref = next(it) if n_lat_keys else None
    tkl_ref = next(it) if n_lat_keys else None
    kvc_ref = next(it)
    krc_ref = next(it)
    tq_ref = next(it) if rope_q else None
    gq_ref = next(it)
    gk_ref = next(it)
    if aliased:
        next(it)
    o_ref = next(it)
    kbuf = next(it)
    vbuf = next(it)
    dqk = D_NOPE + D_ROPE

    def prep_keys(kv_ref, kr_ref, trig_ref, off, n):
        gk = gk_ref[...]

        def chunk(c, carry):
            rows = pl.ds(pl.multiple_of(c * key_chunk, key_chunk), key_chunk)
            kv = kv_ref[rows, :]
            kn = kv[:, :D_NOPE].astype(F32)
            kr = kr_ref[rows, :].astype(F32)
            lane = lax.broadcasted_iota(jnp.int32, kr.shape, 1)
            ss = (jnp.sum(kn * kn, axis=-1, keepdims=True)
                  + jnp.sum(jnp.where(lane < D_ROPE, kr * kr, 0.0), axis=-1, keepdims=True))
            r = lax.rsqrt(ss / dqk + NORM_EPS)
            trig = trig_ref[rows, :] if trig_ref is not None else None
            rot = _rope_rot(kr * (r * gk[:, D_NOPE:]), trig)
            orow = pl.ds(pl.multiple_of(off + c * key_chunk, key_chunk), key_chunk)
            kbuf[orow, :D_NOPE] = (kn * (r * gk[:, :D_NOPE])).astype(BF16)
            kbuf[orow, D_NOPE:] = rot.astype(BF16)
            vbuf[orow, :] = kv[:, D_NOPE:].astype(BF16)
            return carry

        lax.fori_loop(0, n // key_chunk, chunk, 0)

    @pl.when(pl.program_id(2) == 0)
    def _():
        if n_lat_keys:
            prep_keys(kvl_ref, krl_ref, tkl_ref, 0, n_lat_keys)
        prep_keys(kvc_ref, krc_ref, None, n_lat_keys, n_ctx_keys)

    gq = gq_ref[...]
    for sb in range(q_ref.shape[0] // MLA_SUB_Q):
        rows = slice(sb * MLA_SUB_Q, (sb + 1) * MLA_SUB_Q)
        q = q_ref[rows, :].astype(F32)
        qn = q[:, :D_NOPE]
        qr = q[:, D_NOPE:]
        lane = lax.broadcasted_iota(jnp.int32, qr.shape, 1)
        ss = (jnp.sum(qn * qn, axis=-1, keepdims=True)
              + jnp.sum(jnp.where(lane < D_ROPE, qr * qr, 0.0), axis=-1, keepdims=True))
        r = lax.rsqrt(ss / dqk + NORM_EPS) * scale
        rot = _rope_rot(qr * (r * gq[:, D_NOPE:]), tq_ref[rows, :] if rope_q else None)
        rot = jnp.where(lane < D_ROPE, rot, 0.0)
        q2 = jnp.concatenate([qn * (r * gq[:, :D_NOPE]), rot], axis=1).astype(BF16)
        s = lax.dot_general(q2, kbuf[...], (((1,), (1,)), ((), ())), preferred_element_type=F32)
        m = jnp.max(s, axis=-1, keepdims=True)
        p = jnp.exp2(s - m)
        l = jnp.sum(p, axis=-1, keepdims=True)
        o = jnp.dot(p.astype(BF16), vbuf[...], preferred_element_type=F32)
        o_ref[rows, :] = (o / l).astype(o_ref.dtype)


MLA_SUB_Q = 256
MLA_SUBS_PER_STEP = 8


def mla_attention(q_all, kv_all, p_all, trig, gq, gk, *, batch, seq, ctx_len, latent_queries, out_buf=None):
    n_lat = batch * seq
    kr_blk = P_KR // 128
    scale = (D_NOPE + D_ROPE) ** -0.5 * float(np.log2(np.e))
    if latent_queries:
        tq = MLA_SUB_Q * min(MLA_SUBS_PER_STEP, seq // MLA_SUB_Q)
        nq, n_lat_keys, q_row0 = seq, seq, 0
    else:
        tq = ctx_len
        nq, n_lat_keys, q_row0 = ctx_len, 0, n_lat // tq
    assert tq % MLA_SUB_Q == 0 and nq % tq == 0
    nqb = nq // tq
    ctx_blk0 = n_lat // ctx_len
    in_specs = [pl.BlockSpec((tq, 256), lambda b, h, i: (q_row0 + b * nqb + i, h))]
    args = [q_all]
    if latent_queries:
        in_specs += [
            pl.BlockSpec((seq, 256), lambda b, h, i: (b, h)),
            pl.BlockSpec((seq, 128), lambda b, h, i: (b, kr_blk)),
            pl.BlockSpec((seq, 128), lambda b, h, i: (0, 0)),
        ]
        args += [kv_all, p_all, trig]
    in_specs += [
        pl.BlockSpec((ctx_len, 256), lambda b, h, i: (ctx_blk0 + b, h)),
        pl.BlockSpec((ctx_len, 128), lambda b, h, i: (ctx_blk0 + b, kr_blk)),
    ]
    args += [kv_all, p_all]
    if latent_queries:
        in_specs.append(pl.BlockSpec((tq, 128), lambda b, h, i: (i, 0)))
        args.append(trig)
    in_specs += [pl.BlockSpec((1, 256), lambda b, h, i: (0, 0))] * 2
    args += [gq, gk]
    aliases = {}
    if out_buf is not None:
        in_specs.append(pl.BlockSpec(memory_space=pl.ANY))
        aliases = {len(args): 0}
        args.append(out_buf)
    nk = n_lat_keys + ctx_len
    body = functools.partial(_mla_body, n_lat_keys=n_lat_keys, n_ctx_keys=ctx_len, rope_q=latent_queries,
                             scale=scale, key_chunk=min(256, ctx_len), aliased=out_buf is not None)
    return pl.pallas_call(
        body,
        out_shape=jax.ShapeDtypeStruct((q_all.shape[0], W_A), BF16),
        grid=(batch, H_A, nqb),
        in_specs=in_specs,
        out_specs=pl.BlockSpec((tq, D_V_A), lambda b, h, i: (q_row0 + b * nqb + i, h)),
        scratch_shapes=[pltpu.VMEM((nk, 256), BF16), pltpu.VMEM((nk, D_V_A), BF16)],
        input_output_aliases=aliases,
        compiler_params=_cparams(("arbitrary", "arbitrary", "arbitrary")),
        name="mla_attention",
    )(*args)


def _conv_silu_body(x_ref, prev_ref, next_ref, w_ref, o_ref, *, lat_blocks, seq_blocks, ctx_blocks):
    i = pl.program_id(0)
    j = jnp.where(i < lat_blocks, i % seq_blocks, (i - lat_blocks) % ctx_blocks)
    nb = jnp.where(i < lat_blocks, seq_blocks, ctx_blocks)
    x = x_ref[...]
    rows = x.shape[0]
    row = lax.broadcasted_iota(jnp.int32, x.shape, 0)
    prev_row = jnp.where(j == 0, 0.0, prev_ref[7:8, :])
    next_row = jnp.where(j == nb - 1, 0.0, next_ref[0:1, :])
    xp = jnp.where(row == 0, prev_row, pltpu.roll(x, 1, axis=0))
    xn = jnp.where(row == rows - 1, next_row, pltpu.roll(x, rows - 1, axis=0))
    w = w_ref[...]
    y = xp * w[0:1, :] + x * w[1:2, :] + xn * w[2:3, :]
    y = y * jax.nn.sigmoid(y)
    half = y.shape[1] // 2
    o_ref[:, :half] = y[:, :half].astype(o_ref.dtype)
    o_ref[:, half:] = (y[:, half:] * (DQK_B ** -0.5)).astype(o_ref.dtype)


def conv_silu(p_all, conv_w, *, n_lat, seq, ctx_len):
    n = p_all.shape[0]
    width = 2 * H_B * DQK_B
    rb = ROW_BLOCK
    nblk = n // rb
    sub = rb // 8
    last8 = n // 8 - 1
    body = functools.partial(_conv_silu_body, lat_blocks=n_lat // rb, seq_blocks=seq // rb,
                             ctx_blocks=ctx_len // rb)
    return pl.pallas_call(
        body,
        out_shape=jax.ShapeDtypeStruct((n, width), BF16),
        grid=(nblk,),
        in_specs=[
            pl.BlockSpec((rb, width), lambda i: (i, P_QK // width)),
            pl.BlockSpec((8, width), lambda i: (jnp.maximum(i * sub - 1, 0), P_QK // width)),
            pl.BlockSpec((8, width), lambda i: (jnp.minimum((i + 1) * sub, last8), P_QK // width)),
            pl.BlockSpec((3, width), lambda i: (0, 0)),
        ],
        out_specs=pl.BlockSpec((rb, width), lambda i: (i, 0)),
        compiler_params=_cparams(("parallel",)),
        name="conv_silu",
    )(p_all, p_all, p_all, conv_w)


MLSTM_HEADS_PER_STEP = 2


def _log_sigmoid(x):
    return jnp.minimum(x, 0.0) - jnp.log(1.0 + jnp.exp(-jnp.abs(x)))


def _mlstm_body(qf, kf, vf, gcf, grf, qb, kb, vb, gcb, grb, bc_ref, br_ref, of_ref, ob_ref, c_st, n_st, m_st):
    @pl.when(pl.program_id(2) == 0)
    def _():
        c_st[...] = jnp.zeros_like(c_st)
        n_st[...] = jnp.zeros_like(n_st)
        m_st[...] = jnp.full_like(m_st, M_INIT)

    for hh in range(MLSTM_HEADS_PER_STEP):
        sf, sb = hh, MLSTM_HEADS_PER_STEP + hh
        _mlstm_chunk(0, hh, qf, kf, vf, gcf, grf, bc_ref, br_ref, of_ref, c_st.at[sf], n_st.at[sf], m_st.at[sf])
        _mlstm_chunk(1, hh, qb, kb, vb, gcb, grb, bc_ref, br_ref, ob_ref, c_st.at[sb], n_st.at[sb], m_st.at[sb])


def _mlstm_chunk(direction, hh, q_ref, k_ref, v_ref, gc_ref, gr_ref, bc_ref, br_ref, o_ref, c_st, n_st, m_st):
    h = pl.program_id(1) * MLSTM_HEADS_PER_STEP + hh
    qk_l = slice(hh * DQK_B, (hh + 1) * DQK_B)
    v_l = slice(hh * DV_B, (hh + 1) * DV_B)
    L = MLSTM_CHUNK
    hi = lax.Precision.HIGHEST
    gi = direction * (2 * H_B) + h
    gcol = gc_ref[...] + bc_ref[...]
    lane = lax.broadcasted_iota(jnp.int32, gcol.shape, 1)
    li_c = jnp.sum(jnp.where(lane == gi, gcol, 0.0), axis=1, keepdims=True)
    lf_c = _log_sigmoid(jnp.sum(jnp.where(lane == gi + H_B, gcol, 0.0), axis=1, keepdims=True))
    grow = gr_ref[...] + br_ref[...]
    sub = lax.broadcasted_iota(jnp.int32, grow.shape, 0)
    li_r = jnp.sum(jnp.where(sub == gi, grow, 0.0), axis=0, keepdims=True)
    lf_r = _log_sigmoid(jnp.sum(jnp.where(sub == gi + H_B, grow, 0.0), axis=0, keepdims=True))

    t_i = lax.broadcasted_iota(jnp.int32, (L, L), 0)
    s_i = lax.broadcasted_iota(jnp.int32, (L, L), 1)
    sgn = 1 - 2 * direction
    incl = (s_i - t_i) * sgn <= 0
    incl_f = incl.astype(F32)
    incl_t = ((t_i - s_i) * sgn <= 0).astype(F32)
    bcum_c = jnp.dot(incl_f, jnp.broadcast_to(lf_c, (L, L)), precision=hi, preferred_element_type=F32)
    bcum_r = jnp.dot(jnp.broadcast_to(lf_r, (8, L)), incl_t, precision=hi, preferred_element_type=F32)[0:1, :]
    b_last = jnp.sum(lf_r, axis=1, keepdims=True)

    m_prev = m_st[...]
    a_c = bcum_c[:, 0:1] + m_prev
    dlog = jnp.where(incl, bcum_c - bcum_r + li_r, -jnp.inf)
    m_t = jnp.maximum(a_c, jnp.max(dlog, axis=1, keepdims=True))
    w_intra = jnp.exp(dlog - m_t)
    w_inter = jnp.exp(a_c - m_t)

    q = q_ref[:, qk_l]
    k = k_ref[:, qk_l]
    v = v_ref[:, v_l].astype(BF16)
    c_prev = c_st[...]
    n_prev = n_st[...]
    s = lax.dot_general(q, k, (((1,), (1,)), ((), ())), preferred_element_type=F32) * w_intra
    num = (w_inter * jnp.dot(q, c_prev.astype(BF16), preferred_element_type=F32)
           + jnp.dot(s.astype(BF16), v, preferred_element_type=F32))
    den = (w_inter * jnp.sum(q.astype(F32) * n_prev, axis=1, keepdims=True)
           + jnp.sum(s, axis=1, keepdims=True))
    o_ref[:, v_l] = num / jnp.maximum(jnp.abs(den), jnp.exp(-m_t))

    g_r = b_last - bcum_r + li_r
    g_c = b_last - bcum_c[:, 0:1] + li_c
    m_new = jnp.maximum(b_last + m_prev, jnp.max(g_r, axis=1, keepdims=True))
    decay = jnp.exp(b_last + m_prev - m_new)
    kw = k.astype(F32) * jnp.exp(g_c - m_new)
    c_st[...] = decay * c_prev + lax.dot_general(kw.astype(BF16), v, (((0,), (0,)), ((), ())),
                                                 preferred_element_type=F32)
    n_st[...] = decay * n_prev + jnp.sum(kw, axis=0, keepdims=True)
    m_st[...] = m_new


def mlstm_scan(qk, p_all, gates_t, gate_b, *, batch, seq, ctx_len):
    n = qk.shape[0]
    L = MLSTM_CHUNK
    cc, lc = ctx_len // L, seq // L
    n_lat_blk = batch * lc

    def rowblk(b, d, c):
        ctx_blk = n_lat_blk + b * cc + (c if d == 0 else cc - 1 - c)
        lat_blk = b * lc + (c - cc if d == 0 else lc - 1 - (c - cc))
        return jnp.where(c < cc, ctx_blk, lat_blk)

    hps = MLSTM_HEADS_PER_STEP
    wqk, wv = hps * DQK_B, hps * DV_B
    assert H_B % hps == 0

    def chunk_specs(d):
        return [
            pl.BlockSpec((L, wqk), lambda b, h, c: (rowblk(b, d, c), h)),
            pl.BlockSpec((L, wqk), lambda b, h, c: (rowblk(b, d, c), H_B // hps + h)),
            pl.BlockSpec((L, wv), lambda b, h, c: (rowblk(b, d, c), P_V // wv + h)),
            pl.BlockSpec((L, 128), lambda b, h, c: (rowblk(b, d, c), P_GATE // 128)),
            pl.BlockSpec((4 * H_B, L), lambda b, h, c: (0, rowblk(b, d, c))),
        ]

    def out_spec(d):
        return pl.BlockSpec((L, wv), lambda b, h, c: (rowblk(b, d, c), h))

    gb_col = jnp.zeros((1, 128), F32).at[0, :4 * H_B].set(gate_b)
    gb_row = gate_b.reshape(4 * H_B, 1)
    chunk_args = [qk, qk, p_all, p_all, gates_t]
    return pl.pallas_call(
        _mlstm_body,
        out_shape=(jax.ShapeDtypeStruct((n, W_B), F32), jax.ShapeDtypeStruct((n, W_B), F32)),
        grid=(batch, H_B // hps, cc + lc),
        in_specs=chunk_specs(0) + chunk_specs(1) + [
            pl.BlockSpec((1, 128), lambda b, h, c: (0, 0)),
            pl.BlockSpec((4 * H_B, 1), lambda b, h, c: (0, 0)),
        ],
        out_specs=(out_spec(0), out_spec(1)),
        scratch_shapes=[pltpu.VMEM((2 * hps, DQK_B, DV_B), F32), pltpu.VMEM((2 * hps, 1, DQK_B), F32),
                        pltpu.VMEM((2 * hps, 1, 1), F32)],
        compiler_params=_cparams(("arbitrary",) * 3),
        name="mlstm_scan",
    )(*chunk_args, *chunk_args, gb_col, gb_row)


def _mlstm_finish_body(hf_ref, hb_ref, o_ref, g_ref, out_ref):
    hs = hf_ref[...] + hb_ref[...]
    g = g_ref[...]
    og = o_ref[...]
    for hd in range(H_B):
        sl = slice(hd * DV_B, (hd + 1) * DV_B)
        x = hs[:, sl]
        ms = jnp.mean(x * x, axis=-1, keepdims=True)
        hn = x * lax.rsqrt(ms + NORM_EPS) * g[:, sl]
        out_ref[:, sl] = (jax.nn.sigmoid(og[:, sl]) * hn).astype(out_ref.dtype)


def mlstm_finish(h_fwd, h_bwd, p_all, out_norm_g):
    n = h_fwd.shape[0]
    rb = ROW_BLOCK
    return pl.pallas_call(
        _mlstm_finish_body,
        out_shape=jax.ShapeDtypeStruct((n, W_B), BF16),
        grid=(n // rb,),
        in_specs=[
            pl.BlockSpec((rb, W_B), lambda i: (i, 0)),
            pl.BlockSpec((rb, W_B), lambda i: (i, 0)),
            pl.BlockSpec((rb, W_B), lambda i: (i, P_O // W_B)),
            pl.BlockSpec((1, W_B), lambda i: (0, 0)),
        ],
        out_specs=pl.BlockSpec((rb, W_B), lambda i: (i, 0)),
        compiler_params=_cparams(("parallel",)),
        name="mlstm_finish",
    )(h_fwd, h_bwd, p_all, out_norm_g.reshape(1, W_B))


NA_ROWS_PER_STEP = 4
NA_TILES_PER_STEP = 8


def _na_geometry(rows_n):
    r_t = NA_ROWS_PER_STEP
    kh = min(NA_KH, rows_n)
    win = min(r_t + kh - 1, rows_n)
    tiles = rows_n // r_t
    starts, sigs, cls = [], [], []
    for t in range(tiles):
        rq0 = t * r_t
        start = int(np.clip(rq0 - kh // 2, 0, rows_n - win))
        r0 = np.clip(np.arange(rq0, rq0 + r_t) - kh // 2, 0, rows_n - kh)
        sig = (start - rq0, tuple((r0 - rq0).tolist()))
        if sig not in sigs:
            sigs.append(sig)
        starts.append(start)
        cls.append(sigs.index(sig))
    return kh, win, np.array(starts, np.int32), np.array(cls, np.int32), sigs


def na_bias_rows(rpb):
    cols = np.arange(GRID_W)
    c0 = np.clip(cols - NA_KW // 2, 0, GRID_W - NA_KW)
    col_valid = (cols[None, :] >= c0[:, None]) & (cols[None, :] < c0[:, None] + NA_KW)
    col_off = cols[None, :] - cols[:, None] + (NA_KW - 1)
    c_sel = (col_valid[:, :, None] & (col_off[:, :, None] == np.arange(2 * NA_KW - 1))).astype(np.float32)
    t = jnp.einsum('hrs,cds->hrcd', rpb, c_sel, precision=lax.Precision.HIGHEST)
    t = t + np.where(col_valid, 0.0, NEG).astype(np.float32)[None, None]
    return jnp.concatenate([t, t], axis=-1)


def _na_bias_plan(rows_n):
    r_t = NA_ROWS_PER_STEP
    kh, win, _, _, sigs = _na_geometry(rows_n)
    plan = []
    for (dstart, dr0) in sigs:
        cls_plan = []
        for a in range(r_t):
            row = []
            for j in range(win):
                krow = dstart + j
                ok = dr0[a] <= krow < dr0[a] + kh
                row.append(krow - a + (NA_KH - 1) if ok else None)
            cls_plan.append(row)
        plan.append(cls_plan)
    return plan


def _na_build_bias(rows_ref, btab, plan):
    neg = jnp.full((GRID_W, 2 * GRID_W), NEG, F32)
    lane = lax.broadcasted_iota(jnp.int32, (GRID_W, 2 * GRID_W), 1)

    def piece(ro):
        return neg if ro is None else rows_ref[0, ro]

    for k, cls_plan in enumerate(plan):
        for a, row in enumerate(cls_plan):
            rs = slice(a * GRID_W, (a + 1) * GRID_W)
            for j in range(0, len(row) - 1, 2):
                both = jnp.where(lane < GRID_W, piece(row[j]), piece(row[j + 1]))
                btab[k, rs, j * GRID_W:(j + 2) * GRID_W] = both
            if len(row) % 2:
                j = len(row) - 1
                btab[k, rs, j * GRID_W:(j + 1) * GRID_W] = piece(row[j])[:, :GRID_W]


def _head_rmsnorm(x, g):
    xf = x.astype(F32)
    ms = jnp.mean(xf * xf, axis=-1, keepdims=True)
    return xf * lax.rsqrt(ms + NORM_EPS) * g


def _na_body(start_ref, cls_ref, q_ref, k_ref, v_ref, kc_ref, vc_ref, brow_ref, gq_ref, gk_ref, *rest,
             seq, ctx_len, win_rows, scale, bias_plan, tps):
    o_ref, kn, knc, btab = rest[-4:]
    i = pl.program_id(2)
    chunk = min(512, seq)

    @pl.when(i == 0)
    def _():
        _na_build_bias(brow_ref, btab, bias_plan)
        gk = gk_ref[...]

        def body(c, carry):
            rows = pl.ds(pl.multiple_of(c * chunk, chunk), chunk)
            kn[rows, :] = _head_rmsnorm(k_ref[rows, :], gk).astype(BF16)
            return carry

        lax.fori_loop(0, seq // chunk, body, 0)
        knc[...] = _head_rmsnorm(kc_ref[...], gk).astype(BF16)

    nt = (((1,), (1,)), ((), ()))
    tq = NA_ROWS_PER_STEP * GRID_W
    for tl in range(tps):
        rows = slice(tl * tq, (tl + 1) * tq)
        q = (_head_rmsnorm(q_ref[rows, :], gq_ref[...]) * scale).astype(BF16)
        krow0 = pl.multiple_of(start_ref[i * tps + tl] * GRID_W, GRID_W)
        kw = kn[pl.ds(krow0, win_rows), :]
        vw = v_ref[pl.ds(krow0, win_rows), :]
        s = lax.dot_general(q, kw, nt, preferred_element_type=F32) + btab[cls_ref[i * tps + tl]]
        sc = lax.dot_general(q, knc[...], nt, preferred_element_type=F32)
        m = jnp.maximum(jnp.max(s, axis=-1, keepdims=True), jnp.max(sc, axis=-1, keepdims=True))
        p = jnp.exp(s - m)
        pc = jnp.exp(sc - m)
        l = jnp.sum(p, axis=-1, keepdims=True) + jnp.sum(pc, axis=-1, keepdims=True)
        o = (jnp.dot(p.astype(BF16), vw, preferred_element_type=F32)
             + jnp.dot(pc.astype(BF16), vc_ref[...], preferred_element_type=F32))
        o_ref[rows, :] = (o / l).astype(o_ref.dtype)


def na_attention(qkv, bias_rows, gq, gk, *, batch, seq, ctx_len, out_buf=None):
    rows_n = seq // GRID_W
    r_t = NA_ROWS_PER_STEP
    _, win, starts, cls, _ = _na_geometry(rows_n)
    tps = min(NA_TILES_PER_STEP, rows_n // r_t)
    tq = tps * r_t * GRID_W
    tiles = rows_n // (r_t * tps)
    assert rows_n % (r_t * tps) == 0
    n_lat = batch * seq
    ctx_blk0 = n_lat // ctx_len
    plan = _na_bias_plan(rows_n)
    body = functools.partial(_na_body, seq=seq, ctx_len=ctx_len, win_rows=win * GRID_W, scale=DH_C ** -0.5,
                             bias_plan=plan, tps=tps)
    n_ro = 2 * NA_KH - 1
    in_specs = [
        pl.BlockSpec((tq, DH_C), lambda b, h, i, st, cl: (b * tiles + i, h)),
        pl.BlockSpec((seq, DH_C), lambda b, h, i, st, cl: (b, H_C + h)),
        pl.BlockSpec((seq, DH_C), lambda b, h, i, st, cl: (b, 2 * H_C + h)),
        pl.BlockSpec((ctx_len, DH_C), lambda b, h, i, st, cl: (ctx_blk0 + b, H_C + h)),
        pl.BlockSpec((ctx_len, DH_C), lambda b, h, i, st, cl: (ctx_blk0 + b, 2 * H_C + h)),
        pl.BlockSpec((1, n_ro, GRID_W, 2 * GRID_W), lambda b, h, i, st, cl: (h, 0, 0, 0)),
        pl.BlockSpec((1, DH_C), lambda b, h, i, st, cl: (0, 0)),
        pl.BlockSpec((1, DH_C), lambda b, h, i, st, cl: (0, 0)),
    ]
    args = [jnp.asarray(starts), jnp.asarray(cls), qkv, qkv, qkv, qkv, qkv, bias_rows, gq, gk]
    aliases = {}
    out_rows = n_lat
    if out_buf is not None:
        in_specs.append(pl.BlockSpec(memory_space=pl.ANY))
        aliases = {len(args): 0}
        args.append(out_buf)
        out_rows = out_buf.shape[0]
    grid_spec = pltpu.PrefetchScalarGridSpec(
        num_scalar_prefetch=2,
        grid=(batch, H_C, tiles),
        in_specs=in_specs,
        out_specs=pl.BlockSpec((tq, DH_C), lambda b, h, i, st, cl: (b * tiles + i, h)),
        scratch_shapes=[pltpu.VMEM((seq, DH_C), BF16), pltpu.VMEM((ctx_len, DH_C), BF16),
                        pltpu.VMEM((len(plan), r_t * GRID_W, win * GRID_W), F32)],
    )
    return pl.pallas_call(
        body,
        out_shape=jax.ShapeDtypeStruct((out_rows, H_C * DH_C), BF16),
        grid_spec=grid_spec,
        input_output_aliases=aliases,
        compiler_params=_cparams(("arbitrary", "arbitrary", "arbitrary")),
        name="na_attention",
    )(*args)


def _ctx_attn_body(q_ref, k_ref, v_ref, gq_ref, gk_ref, buf_ref, o_ref, *, scale):
    del buf_ref
    q = (_head_rmsnorm(q_ref[...], gq_ref[...]) * scale).astype(BF16)
    k = _head_rmsnorm(k_ref[...], gk_ref[...]).astype(BF16)
    s = lax.dot_general(q, k, (((1,), (1,)), ((), ())), preferred_element_type=F32)
    m = jnp.max(s, axis=-1, keepdims=True)
    p = jnp.exp(s - m)
    l = jnp.sum(p, axis=-1, keepdims=True)
    o = jnp.dot(p.astype(BF16), v_ref[...], preferred_element_type=F32)
    o_ref[...] = (o / l).astype(o_ref.dtype)


def na_ctx_attention(qkv, gq, gk, out_buf, *, batch, seq, ctx_len):
    ctx_blk0 = batch * seq // ctx_len
    return pl.pallas_call(
        functools.partial(_ctx_attn_body, scale=DH_C ** -0.5),
        out_shape=jax.ShapeDtypeStruct(out_buf.shape, out_buf.dtype),
        grid=(batch, H_C),
        in_specs=[
            pl.BlockSpec((ctx_len, DH_C), lambda b, h: (ctx_blk0 + b, h)),
            pl.BlockSpec((ctx_len, DH_C), lambda b, h: (ctx_blk0 + b, H_C + h)),
            pl.BlockSpec((ctx_len, DH_C), lambda b, h: (ctx_blk0 + b, 2 * H_C + h)),
            pl.BlockSpec((1, DH_C), lambda b, h: (0, 0)),
            pl.BlockSpec((1, DH_C), lambda b, h: (0, 0)),
            pl.BlockSpec(memory_space=pl.ANY),
        ],
        out_specs=pl.BlockSpec((ctx_len, DH_C), lambda b, h: (ctx_blk0 + b, h)),
        input_output_aliases={5: 0},
        compiler_params=_cparams(("parallel", "parallel")),
        name="na_ctx_attention",
    )(qkv, qkv, qkv, gq, gk, out_buf)


def _slab_copy(src_ref, src_tok, dst_ref, dst_tok, sem):
    def rows(tok):
        off = tok * SLAB_STRIDE
        return pl.ds(off if isinstance(off, int) else pl.multiple_of(off, 8), SLAB_ROWS)

    return pltpu.make_async_copy(src_ref.at[rows(src_tok), :], dst_ref.at[rows(dst_tok), :], sem)


def _load_slabs(ref, rows, dtype):
    return jnp.concatenate([ref[pl.ds(j, rows, stride=SLAB_STRIDE), :].astype(dtype) for j in range(SLAB_ROWS)],
                           axis=1)


def moe_slot_plan(route, counts, n_tiles):
    cnt = counts[0, :N_EXPERTS].astype(jnp.int32)
    padded = (cnt + MOE_TILE - 1) // MOE_TILE * MOE_TILE
    ends = jnp.cumsum(padded)
    base = ends - padded
    n_used = ends[-1] // MOE_TILE
    tile_start = jnp.arange(n_tiles, dtype=jnp.int32) * MOE_TILE
    last_start = jnp.maximum(ends[-1] - MOE_TILE, 0)
    tile_expert = jnp.searchsorted(ends, jnp.minimum(tile_start, last_start), side='right').astype(jnp.int32)
    tile_expert = jnp.minimum(tile_expert, N_EXPERTS - 1)
    ids = route[:, 0:2].astype(jnp.int32)
    pos = jnp.take(base, ids, axis=0) + route[:, 4:6].astype(jnp.int32)
    return pos[:, 0], pos[:, 1], tile_expert, n_used.reshape(1).astype(jnp.int32)


def _slot_source_body(p1_ref, p2_ref, src_ref, *, n_tok):
    def clear(i, c):
        src_ref[i] = 0
        return c

    lax.fori_loop(0, src_ref.shape[0], clear, 0, unroll=8)

    def fill(n, c):
        src_ref[p1_ref[n]] = n
        src_ref[p2_ref[n]] = n
        return c

    lax.fori_loop(0, n_tok, fill, 0, unroll=4)


def moe_slot_source(pos1, pos2, n_slots):
    smem = pl.BlockSpec(memory_space=pltpu.SMEM)
    return pl.pallas_call(
        functools.partial(_slot_source_body, n_tok=pos1.shape[0]),
        out_shape=jax.ShapeDtypeStruct((n_slots,), jnp.int32),
        in_specs=[smem, smem],
        out_specs=smem,
        name="moe_slot_source",
    )(pos1, pos2)


def _gather_slabs(src_ref, tok_ref, tok0, dst_ref, sem, n):
    def issue(r2, c):
        for p in range(2):
            r = 2 * r2 + p
            _slab_copy(src_ref, tok_ref[tok0 + r], dst_ref, r, sem).start(priority=p)
        return c

    lax.fori_loop(0, n // 2, issue, 0, unroll=4)


def _wait_slabs(src_ref, dst_ref, sem, n):
    rows = pl.ds(0, n * SLAB_ROWS)
    pltpu.make_async_copy(src_ref.at[rows, :], dst_ref.at[rows, :], sem).wait()


def _moe_expert_body(te_ref, nu_ref, src_ref, hs_ref, wg_ref, wu_ref, wd_ref, o_ref, xbuf, wg_bf, wu_bf, wd_bf,
                     sems):
    t = pl.program_id(0)
    tm = MOE_TILE
    slot = t % 2

    @pl.when(t < nu_ref[0])
    def _():
        @pl.when(t == 0)
        def _():
            _gather_slabs(hs_ref, src_ref, 0, xbuf.at[0], sems.at[0], tm)

        @pl.when(t + 1 < nu_ref[0])
        def _():
            _gather_slabs(hs_ref, src_ref, (t + 1) * tm, xbuf.at[1 - slot], sems.at[1 - slot], tm)

        @pl.when((t == 0) | (te_ref[t] != te_ref[jnp.maximum(t - 1, 0)]))
        def _():
            wg_bf[...] = wg_ref[...].astype(BF16)
            wu_bf[...] = wu_ref[...].astype(BF16)
            wd_bf[...] = wd_ref[...].astype(BF16)

        _wait_slabs(hs_ref, xbuf.at[slot], sems.at[slot], tm)
        x = _load_slabs(xbuf.at[slot], tm, BF16)
        nt = (((1,), (1,)), ((), ()))
        a = lax.dot_general(x, wg_bf[...], nt, preferred_element_type=F32)
        u = lax.dot_general(x, wu_bf[...], nt, preferred_element_type=F32)
        y = (a * jax.nn.sigmoid(a) * u).astype(BF16)
        _store_slabs(o_ref, jnp.dot(y, wd_bf[...], preferred_element_type=F32), tm)

    @pl.when(t >= nu_ref[0])
    def _():
        o_ref[...] = jnp.zeros_like(o_ref)


def moe_experts(hs, src, tile_expert, n_used, w_gate_t, w_up_t, w_down, layer):
    n_tiles = tile_expert.shape[0]
    f, d = w_gate_t.shape[-2:]
    blk = MOE_TILE * SLAB_STRIDE
    wspec = pl.BlockSpec((None, None, f, d), lambda t, te, nu, src: (layer, te[t], 0, 0))
    grid_spec = pltpu.PrefetchScalarGridSpec(
        num_scalar_prefetch=3,
        grid=(n_tiles,),
        in_specs=[pl.BlockSpec(memory_space=pl.ANY), wspec, wspec, wspec],
        out_specs=pl.BlockSpec((blk, 128), lambda t, te, nu, src: (t, 0)),
        scratch_shapes=[pltpu.VMEM((2, blk, 128), F32), pltpu.VMEM((f, d), BF16), pltpu.VMEM((f, d), BF16),
                        pltpu.VMEM((f, d), BF16), pltpu.SemaphoreType.DMA((2,))],
    )
    return pl.pallas_call(
        _moe_expert_body,
        out_shape=jax.ShapeDtypeStruct((n_tiles * blk, 128), F32),
        grid_spec=grid_spec,
        compiler_params=_dma_cparams(("arbitrary",)),
        name="moe_experts",
    )(tile_expert, n_used, src, hs, w_gate_t, w_up_t, w_down)


def _moe_combine_body(p1_ref, p2_ref, ys_ref, route_ref, res_ref, gate_ref, o_ref, abuf, bbuf, sems):
    i = pl.program_id(0)
    rb = ROW_BLOCK
    slot = i % 2

    def gather(blk, s):
        _gather_slabs(ys_ref, p1_ref, blk * rb, abuf.at[s], sems.at[0, s], rb)
        _gather_slabs(ys_ref, p2_ref, blk * rb, bbuf.at[s], sems.at[1, s], rb)

    @pl.when(i == 0)
    def _():
        gather(0, 0)

    @pl.when(i + 1 < pl.num_programs(0))
    def _():
        gather(i + 1, 1 - slot)

    route = route_ref[...]
    lane = lax.broadcasted_iota(jnp.int32, route.shape, 1)
    w1 = jnp.sum(jnp.where(lane == 2, route, 0.0), axis=1, keepdims=True)
    w2 = jnp.sum(jnp.where(lane == 3, route, 0.0), axis=1, keepdims=True)
    _wait_slabs(ys_ref, abuf.at[slot], sems.at[0, slot], rb)
    _wait_slabs(ys_ref, bbuf.at[slot], sems.at[1, slot], rb)
    for j in range(SLAB_ROWS):
        sl = slice(j * 128, (j + 1) * 128)
        rows = pl.ds(j, rb, stride=SLAB_STRIDE)
        mix = w1 * abuf[slot, rows, :] + w2 * bbuf[slot, rows, :]
        o_ref[:, sl] = res_ref[:, sl] + gate_ref[0][:, sl] * mix


def moe_combine(ys, pos1, pos2, route, resid, mods, k_gate, dims):
    n, d = route.shape[0], resid.shape[1]
    n_lat, seq, batch = dims
    rb = ROW_BLOCK
    mrow = functools.partial(_mod_row, rows_per_block=rb, n_lat=n_lat, seq=seq, batch=batch)
    grid_spec = pltpu.PrefetchScalarGridSpec(
        num_scalar_prefetch=2,
        grid=(n // rb,),
        in_specs=[
            pl.BlockSpec(memory_space=pl.ANY),
            pl.BlockSpec((rb, 128), lambda i, p1, p2: (i, 0)),
            pl.BlockSpec((rb, d), lambda i, p1, p2: (i, 0)),
            pl.BlockSpec((1, 1, d), lambda i, p1, p2: (mrow(i) * 6 + k_gate, 0, 0)),
        ],
        out_specs=pl.BlockSpec((rb, d), lambda i, p1, p2: (i, 0)),
        scratch_shapes=[pltpu.VMEM((2, rb * SLAB_STRIDE, 128), F32), pltpu.VMEM((2, rb * SLAB_STRIDE, 128), F32),
                        pltpu.SemaphoreType.DMA((2, 2))],
    )
    return pl.pallas_call(
        _moe_combine_body,
        out_shape=jax.ShapeDtypeStruct((n, d), F32),
        grid_spec=grid_spec,
        compiler_params=_dma_cparams(("arbitrary",)),
        name="moe_combine",
    )(pos1, pos2, ys, route, resid, mods)


def _swap16(a):
    lead = a.shape[:-1]
    return jnp.flip(a.reshape(*lead, D_ROPE // 32, 2, 16), axis=-2).reshape(*lead, D_ROPE)


W_IN_TILE = 512
_C_KR = Q_LORA + KV_LORA
_C_QB = _C_KR + D_ROPE
_C_G = _C_QB + 2 * H_B * DQK_B + 2 * W_B


def _w_in_tail_t(wt_all, j):
    wt = wt_all[j]
    kr = wt[_C_KR:_C_QB]
    kr_sw = jnp.flip(kr.reshape(D_ROPE // 32, 2, 16, -1), axis=1).reshape(D_ROPE, -1)
    gates = wt[_C_G:]
    pad = W_IN_TILE - 2 * D_ROPE - gates.shape[0]
    return jnp.concatenate([kr, kr_sw, jnp.pad(gates, ((0, pad), (0, 0)))], axis=0)


def _win_body(h_ref, wt_ref, tail_ref, o_ref, wbf_ref, *, n_main):
    j = pl.program_id(0)

    @pl.when((pl.program_id(1) == 0) & (j < n_main))
    def _():
        wbf_ref[...] = wt_ref[0].astype(BF16)

    @pl.when((pl.program_id(1) == 0) & (j >= n_main))
    def _():
        wbf_ref[...] = tail_ref[...].astype(BF16)

    o_ref[...] = lax.dot_general(h_ref[...], wbf_ref[...], (((1,), (1,)), ((), ())),
                                 preferred_element_type=F32).astype(o_ref.dtype)


def input_projection(h, wt_all, layer):
    m, d = h.shape
    tm = _row_tile(m)
    tn = W_IN_TILE
    n_a = (_C_G - _C_QB) // tn
    n_b = _C_KR // tn
    n_main = n_a + n_b
    assert (_C_G - _C_QB) % tn == 0 and _C_KR % tn == 0 and (n_main + 1) * tn == P_WIDTH and _C_QB % 8 == 0

    def w_row(j):
        row = jnp.where(j < n_a, _C_QB + j * tn, jnp.where(j < n_main, (j - n_a) * tn, 0))
        return pl.multiple_of(row, 8)

    return pl.pallas_call(
        functools.partial(_win_body, n_main=n_main),
        out_shape=jax.ShapeDtypeStruct((m, P_WIDTH), F32),
        grid=(n_main + 1, m // tm),
        in_specs=[
            pl.BlockSpec((tm, d), lambda j, i: (i, 0)),
            pl.BlockSpec((pl.Element(1), pl.Element(tn), pl.Element(d)), lambda j, i: (layer, w_row(j), 0)),
            pl.BlockSpec((tn, d), lambda j, i: (0, 0), pipeline_mode=pl.Buffered(1)),
        ],
        out_specs=pl.BlockSpec((tm, tn), lambda j, i: (i, j)),
        scratch_shapes=[pltpu.VMEM((tn, d), BF16)],
        compiler_params=_cparams(("arbitrary", "arbitrary")),
        name="input_projection",
    )(h, wt_all, _w_in_tail_t(wt_all, layer))


def _ext_w_q_up(w):
    w3 = w.reshape(Q_LORA, H_A, D_NOPE + D_ROPE)
    return jnp.concatenate([w3, _swap16(w3[:, :, D_NOPE:])], axis=2).reshape(Q_LORA, H_A * 256)


def _ext_gain(g):
    return jnp.concatenate([g, _swap16(g[D_NOPE:])]).reshape(1, 256)


def _rope_tables(seq):
    t = jnp.arange(seq)
    half = D_ROPE // 4
    inv = ROPE_THETA ** (-jnp.arange(half, dtype=F32) / half)
    ar = (t // GRID_W).astype(F32)[:, None] * inv[None, :]
    ac = (t % GRID_W).astype(F32)[:, None] * inv[None, :]
    cr, sr, cc, sc = jnp.cos(ar), jnp.sin(ar), jnp.cos(ac), jnp.sin(ac)
    return jnp.concatenate([cr, cr, cc, cc, -sr, sr, -sc, sc], axis=1)


def kernel(x, c, ctx, c_ctx, ada_w, ada_b, norm1_g, norm2_g, ab_w_in, mla_q_norm_g, mla_kv_norm_g, mla_w_q_up, mla_w_kv_up, mla_qn_g, mla_kn_g, mlstm_conv_w, mlstm_gate_b, mlstm_out_norm_g, ab_w_out, na_w_qkv, na_qn_g, na_kn_g, na_rpb, na_w_out, moe_w_rg, moe_b_rg, moe_w_re, moe_b_re, moe_w_gate, moe_w_up, moe_w_down):
    batch, seq, d = x.shape
    ctx_len = ctx.shape[1]
    depth = ada_w.shape[0]
    n_lat = batch * seq
    dims = (n_lat, seq, batch)
    assert d == D_MODEL and batch + 1 <= MOD_ROWS and seq % 512 == 0 and ctx_len == ROW_BLOCK

    xs = jnp.concatenate([x.reshape(n_lat, d), ctx.reshape(batch * ctx_len, d)], axis=0)
    cvec = jnp.concatenate([c, c_ctx[None, :]], axis=0)
    mods_all = ada_mods(cvec, ada_w, ada_b).reshape(depth, MOD_ROWS * 6, 1, d)
    trig = _rope_tables(seq)

    w_gate_t, w_up_t = jnp.swapaxes(moe_w_gate, 2, 3), jnp.swapaxes(moe_w_up, 2, 3)
    w_in_t = jnp.swapaxes(ab_w_in, 1, 2)
    kw = dict(batch=batch, seq=seq, ctx_len=ctx_len)
    for layer in range(depth):
        j = layer // 2
        last = layer == depth - 1
        mods = mods_all[layer]
        h = norm_mod(xs, norm1_g[layer], mods, (0, 1), dims)
        if layer % 2 == 0:
            p_all = input_projection(h, w_in_t, j)
            tm_big = _row_tile(p_all.shape[0])
            q_all = matmul(p_all, _ext_w_q_up(mla_w_q_up[j]), out_dtype=BF16, a_col=P_CQ // Q_LORA, k=Q_LORA,
                           rms_gain=mla_q_norm_g[j], tm=tm_big)
            kv_all = matmul(p_all, mla_w_kv_up, layer=j, out_dtype=BF16, a_col=P_CKV // KV_LORA, k=KV_LORA,
                            rms_gain=mla_kv_norm_g[j], tm=tm_big)
            gq, gk = _ext_gain(mla_qn_g[j]), _ext_gain(mla_kn_g[j])
            mix_a = jnp.zeros((xs.shape[0], W_A), BF16)
            mix_a = mla_attention(q_all, kv_all, p_all, trig, gq, gk, latent_queries=True, out_buf=mix_a, **kw)
            mix_a = mla_attention(q_all, kv_all, p_all, trig, gq, gk, latent_queries=False, out_buf=mix_a, **kw)
            qk = conv_silu(p_all, mlstm_conv_w[j], n_lat=n_lat, seq=seq, ctx_len=ctx_len)
            gates_t = p_all[:, P_GATE:P_GATE + 4 * H_B].T
            h_fwd, h_bwd = mlstm_scan(qk, p_all, gates_t, mlstm_gate_b[j], **kw)
            mix_b = mlstm_finish(h_fwd, h_bwd, p_all, mlstm_out_norm_g[j])
            xs = matmul(mix_a, ab_w_out, layer=j, a2=mix_b, out_dtype=F32, resid=xs, mods=mods, k_gate=2,
                        dims=dims)
        else:
            qkv = matmul(h, na_w_qkv, layer=j, out_dtype=BF16, tm=_row_tile(h.shape[0]))
            gq, gk = na_qn_g[j].reshape(1, DH_C), na_kn_g[j].reshape(1, DH_C)
            bias = na_bias_rows(na_rpb[j])
            if last:
                mixed = na_attention(qkv, bias, gq, gk, **kw)
            else:
                mixed = jnp.zeros((xs.shape[0], H_C * DH_C), BF16)
                mixed = na_attention(qkv, bias, gq, gk, out_buf=mixed, **kw)
                mixed = na_ctx_attention(qkv, gq, gk, mixed, **kw)
            xs = matmul(mixed, na_w_out, layer=j, out_dtype=F32, resid=xs, mods=mods, k_gate=2, dims=dims)

        w_r = jnp.pad(jnp.concatenate([moe_w_re[layer], moe_w_rg[layer]], axis=1),
                      ((0, 0), (0, 128 - N_EXPERTS - N_GROUPS)))
        b_r = jnp.pad(jnp.concatenate([moe_b_re[layer], moe_b_rg[layer]]), (0, 128 - N_EXPERTS - N_GROUPS))
        hs, route, counts = norm_router(xs, norm2_g[layer], mods, (3, 4), dims, w_r, b_r.reshape(1, 128))
        n_tiles = 2 * xs.shape[0] // MOE_TILE + N_EXPERTS
        pos1, pos2, tile_expert, n_used = moe_slot_plan(route, counts, n_tiles)
        src = moe_slot_source(pos1, pos2, n_tiles * MOE_TILE)
        ys = moe_experts(hs, src, tile_expert, n_used, w_gate_t, w_up_t, moe_w_down, layer)
        xs = moe_combine(ys, pos1, pos2, route, xs, mods, 5, dims)
    return xs[:n_lat].reshape(batch, seq, d) if xs.shape[0] != n_lat else xs.reshape(batch, seq, d)
```
